```python
import math
import jax, jax.numpy as jnp
from jax import lax
import numpy as np

D_MODEL = 1024
BATCH = 8
SEQ = 8192
DEPTH = 1

HEAD_DIM = 64
ATTN_WIDTH = D_MODEL // 2
ATTN_HEADS = ATTN_WIDTH // HEAD_DIM
KV_HEADS = 2
GQA_GROUP = ATTN_HEADS // KV_HEADS
ROT_DIM = HEAD_DIM // 4
ROPE_THETA = 500000.0
WINDOW = 128
BLOCK = 128
SSM_INNER = D_MODEL - ATTN_WIDTH
SSM_HEAD_DIM = 64
SSM_HEADS = SSM_INNER // SSM_HEAD_DIM
SSM_GROUPS = 2
SSM_HPG = SSM_HEADS // SSM_GROUPS
D_STATE = 128
SSM_CONV = 5
CHUNK = 128
D_FF = ((8 * D_MODEL // 3 + 127) // 128) * 128
FFN_CONV = 3
EPS = 1e-5
NEG = -1e30

Q_COLS = ATTN_HEADS * HEAD_DIM
KV_COLS = KV_HEADS * HEAD_DIM
XBC_COLS = SSM_INNER + 2 * SSM_GROUPS * D_STATE
DT_COLS = 2 * SSM_HEADS
IN_COLS = Q_COLS + 2 * KV_COLS + SSM_INNER + XBC_COLS + DT_COLS
IN_SPLITS = (Q_COLS, Q_COLS + KV_COLS, Q_COLS + 2 * KV_COLS,
             Q_COLS + 2 * KV_COLS + SSM_INNER,
             Q_COLS + 2 * KV_COLS + SSM_INNER + XBC_COLS)

kernel_name = "hymba_ssd_swa_convffn_encoder"


def rmsnorm(x, g):
    xf = x.astype(jnp.float32)
    y = xf * lax.rsqrt(jnp.mean(xf * xf, axis=-1, keepdims=True) + EPS)
    return (y * g.astype(jnp.float32)).astype(x.dtype)


def dwconv(u, w, b):
    K, C = w.shape
    y = lax.conv_general_dilated(
        u, w[:, None, :].astype(u.dtype), window_strides=(1,),
        padding=[(K // 2, K // 2)], dimension_numbers=("NWC", "WIO", "NWC"),
        feature_group_count=C)
    return y + b.astype(u.dtype)


def rope_tables(S):
    pos = jnp.arange(S, dtype=jnp.float32)
    inv = ROPE_THETA ** (-jnp.arange(0, ROT_DIM, 2, dtype=jnp.float32) / ROT_DIM)
    ang = pos[:, None] * inv[None, :]
    return jnp.cos(ang), jnp.sin(ang)


def apply_partial_rope(t, cos, sin):
    half = ROT_DIM // 2
    c = cos[None, :, None, :].astype(t.dtype)
    s = sin[None, :, None, :].astype(t.dtype)
    t1, t2, rest = t[..., :half], t[..., half:ROT_DIM], t[..., ROT_DIM:]
    return jnp.concatenate([t1 * c - t2 * s, t2 * c + t1 * s, rest], axis=-1)


def windowed_attention(q, k, v, sink, cos, sin):
    Bsz, S, _ = q.shape
    nb = S // BLOCK
    q = apply_partial_rope(q.reshape(Bsz, S, ATTN_HEADS, HEAD_DIM), cos, sin)
    k = apply_partial_rope(k.reshape(Bsz, S, KV_HEADS, HEAD_DIM), cos, sin)
    v = v.reshape(Bsz, S, KV_HEADS, HEAD_DIM)
    qb = q.reshape(Bsz, nb, BLOCK, KV_HEADS, GQA_GROUP, HEAD_DIM)

    def band(t):
        tp = jnp.pad(t, ((0, 0), (BLOCK, BLOCK), (0, 0), (0, 0)))
        tp = tp.reshape(Bsz, nb + 2, BLOCK, KV_HEADS, HEAD_DIM)
        return jnp.concatenate([tp[:, :-2], tp[:, 1:-1], tp[:, 2:]], axis=2)

    kw, vw = band(k), band(v)
    scale = 1.0 / math.sqrt(HEAD_DIM)
    scores = jnp.einsum("bnqkgd,bnskd->bnkgqs", qb, kw).astype(jnp.float32) * scale
    blk = jnp.arange(nb)[:, None, None]
    qpos = blk * BLOCK + jnp.arange(BLOCK)[None, :, None]
    kpos = (blk - 1) * BLOCK + jnp.arange(3 * BLOCK)[None, None, :]
    valid = (jnp.abs(kpos - qpos) <= WINDOW) & (kpos >= 0) & (kpos < S)
    scores = jnp.where(valid[None, :, None, None], scores, NEG)
    sink_l = sink.astype(jnp.float32).reshape(KV_HEADS, GQA_GROUP)[None, None, :, :, None, None]
    m = jnp.maximum(jnp.max(scores, axis=-1, keepdims=True), sink_l)
    p = jnp.exp(scores - m)
    denom = jnp.sum(p, axis=-1, keepdims=True) + jnp.exp(sink_l - m)
    probs = (p / denom).astype(v.dtype)
    o = jnp.einsum("bnkgqs,bnskd->bnqkgd", probs, vw)
    return o.reshape(Bsz, S, ATTN_WIDTH)


def ssd_chunked(xs, dt, a_log, bm, cm):
    Bsz, S = xs.shape[:2]
    nc = S // CHUNK
    A = -jnp.exp(a_log.astype(jnp.float32)).reshape(SSM_GROUPS, SSM_HPG)
    a = (dt * A).reshape(Bsz, nc, CHUNK, SSM_GROUPS, SSM_HPG)
    X = (xs * dt[..., None]).reshape(Bsz, nc, CHUNK, SSM_GROUPS, SSM_HPG, SSM_HEAD_DIM)
    Bc = bm.reshape(Bsz, nc, CHUNK, SSM_GROUPS, D_STATE)
    Cc = cm.reshape(Bsz, nc, CHUNK, SSM_GROUPS, D_STATE)
    acs = jnp.cumsum(a, axis=2)
    idx = jnp.arange(CHUNK)
    lower = (idx[:, None] >= idx[None, :])[None, None, :, :, None, None]
    seg = acs[:, :, :, None] - acs[:, :, None, :]
    L = jnp.exp(jnp.where(lower, seg, -jnp.inf))
    cb = jnp.einsum("bclgn,bcsgn->bclsg", Cc, Bc)
    y_diag = jnp.einsum("bclsgr,bcsgrp->bclgrp", cb[..., None] * L, X)
    decay_states = jnp.exp(acs[:, :, -1:] - acs)
    states = jnp.einsum("bclgn,bclgrp->bcgrpn", Bc, X * decay_states[..., None])
    chunk_decay = jnp.exp(acs[:, :, -1])

    def step(h, inp):
        dec, st = inp
        return h * dec[..., None, None] + st, h

    h0 = jnp.zeros((Bsz, SSM_GROUPS, SSM_HPG, SSM_HEAD_DIM, D_STATE), jnp.float32)
    _, prev = lax.scan(step, h0, (jnp.moveaxis(chunk_decay, 1, 0), jnp.moveaxis(states, 1, 0)))
    prev = jnp.moveaxis(prev, 0, 1)
    y_off = jnp.einsum("bclgn,bcgrpn->bclgrp", Cc, prev) * jnp.exp(acs)[..., None]
    return (y_diag + y_off).reshape(Bsz, S, SSM_GROUPS, SSM_HPG, SSM_HEAD_DIM)


def ssd_mixer(z, xbc, dt_raw, conv_w, conv_b, dt_bias_f, dt_bias_b, a_log_f, a_log_b, d_skip, norm_g):
    Bsz, S, _ = z.shape
    xbc = jax.nn.silu(dwconv(xbc, conv_w, conv_b))
    xs, bm, cm = jnp.split(xbc, [SSM_INNER, SSM_INNER + SSM_GROUPS * D_STATE], axis=-1)
    xs = xs.astype(jnp.float32).reshape(Bsz, S, SSM_GROUPS, SSM_HPG, SSM_HEAD_DIM)
    bm = bm.astype(jnp.float32).reshape(Bsz, S, SSM_GROUPS, D_STATE)
    cm = cm.astype(jnp.float32).reshape(Bsz, S, SSM_GROUPS, D_STATE)
    dt_raw = dt_raw.astype(jnp.float32)
    dt_f = jax.nn.softplus(dt_raw[..., :SSM_HEADS] + dt_bias_f.astype(jnp.float32))
    dt_b = jax.nn.softplus(dt_raw[..., SSM_HEADS:] + dt_bias_b.astype(jnp.float32))
    dt_f = dt_f.reshape(Bsz, S, SSM_GROUPS, SSM_HPG)
    dt_b = dt_b.reshape(Bsz, S, SSM_GROUPS, SSM_HPG)
    flip = lambda t: jnp.flip(t, axis=1)
    y_f = ssd_chunked(xs, dt_f, a_log_f, bm, cm)
    y_b = flip(ssd_chunked(flip(xs), flip(dt_b), a_log_b, flip(bm), flip(cm)))
    y = y_f + y_b + xs * d_skip.astype(jnp.float32).reshape(SSM_GROUPS, SSM_HPG, 1)
    y = y.reshape(Bsz, S, SSM_INNER) * jax.nn.silu(z.astype(jnp.float32))
    yg = y.reshape(Bsz, S, SSM_GROUPS, SSM_INNER // SSM_GROUPS)
    yg = yg * lax.rsqrt(jnp.mean(yg * yg, axis=-1, keepdims=True) + EPS)
    y = yg.reshape(Bsz, S, SSM_INNER) * norm_g.astype(jnp.float32)
    return y.astype(z.dtype)


def setup_inputs(seed: int = 0) -> dict:
    key = jax.random.key(seed)
    ks = jax.random.split(key, 20)
    f32 = jnp.float32

    def nrm(k, shape, scale):
        return jax.random.normal(k, shape, f32) * scale

    def gain(k, shape):
        return 1.0 + 0.02 * jax.random.normal(k, shape, f32)

    def dt_bias(k):
        dt0 = jnp.exp(jax.random.uniform(k, (DEPTH, SSM_HEADS), f32, math.log(1e-3), math.log(1e-1)))
        return dt0 + jnp.log(-jnp.expm1(-dt0))

    def a_log(k):
        return jnp.log(jax.random.uniform(k, (DEPTH, SSM_HEADS), f32, 1.0, 16.0))

    return {
        "x": jax.random.normal(ks[0], (BATCH, SEQ, D_MODEL), f32),
        "norm1_g": gain(ks[1], (DEPTH, D_MODEL)),
        "w_in": nrm(ks[2], (DEPTH, D_MODEL, IN_COLS), D_MODEL ** -0.5),
        "attn_sink": nrm(ks[3], (DEPTH, ATTN_HEADS), 0.5),
        "attn_out_g": gain(ks[4], (DEPTH, ATTN_WIDTH)),
        "ssd_conv_w": nrm(ks[5], (DEPTH, SSM_CONV, XBC_COLS), SSM_CONV ** -0.5),
        "ssd_conv_b": nrm(ks[6], (DEPTH, XBC_COLS), 0.02),
        "ssd_dt_bias_fwd": dt_bias(ks[7]),
        "ssd_dt_bias_bwd": dt_bias(ks[8]),
        "ssd_a_log_fwd": a_log(ks[9]),
        "ssd_a_log_bwd": a_log(ks[10]),
        "ssd_d": gain(ks[11], (DEPTH, SSM_HEADS)),
        "ssd_norm_g": gain(ks[12], (DEPTH, SSM_INNER)),
        "w_out": nrm(ks[13], (DEPTH, ATTN_WIDTH + SSM_INNER, D_MODEL), (ATTN_WIDTH + SSM_INNER) ** -0.5),
        "norm2_g": gain(ks[14], (DEPTH, D_MODEL)),
        "w_up": nrm(ks[15], (DEPTH, D_MODEL, 2 * D_FF), D_MODEL ** -0.5),
        "ffn_conv_w": nrm(ks[16], (DEPTH, FFN_CONV, 2 * D_FF), FFN_CONV ** -0.5),
        "ffn_conv_b": nrm(ks[17], (DEPTH, 2 * D_FF), 0.02),
        "w_down": nrm(ks[18], (DEPTH, D_FF, D_MODEL), D_FF ** -0.5),
        "final_norm_g": gain(ks[19], (D_MODEL,)),
    }


def reference(x, norm1_g, w_in, attn_sink, attn_out_g, ssd_conv_w, ssd_conv_b,
              ssd_dt_bias_fwd, ssd_dt_bias_bwd, ssd_a_log_fwd, ssd_a_log_bwd, ssd_d,
              ssd_norm_g, w_out, norm2_g, w_up, ffn_conv_w, ffn_conv_b, w_down, final_norm_g):
    S = x.shape[1]
    cos, sin = rope_tables(S)
    for l in range(DEPTH):
        h = rmsnorm(x, norm1_g[l])
        proj = h @ w_in[l].astype(h.dtype)
        q, k, v, z, xbc, dt_raw = jnp.split(proj, IN_SPLITS, axis=-1)
        attn = rmsnorm(windowed_attention(q, k, v, attn_sink[l], cos, sin), attn_out_g[l])
        ssd = ssd_mixer(z, xbc, dt_raw, ssd_conv_w[l], ssd_conv_b[l], ssd_dt_bias_fwd[l],
                        ssd_dt_bias_bwd[l], ssd_a_log_fwd[l], ssd_a_log_bwd[l], ssd_d[l], ssd_norm_g[l])
        mixed = jnp.concatenate([attn, ssd], axis=-1)
        x = x + mixed @ w_out[l].astype(mixed.dtype)
        h = rmsnorm(x, norm2_g[l])
        u = dwconv(h @ w_up[l].astype(h.dtype), ffn_conv_w[l], ffn_conv_b[l])
        gate, val = jnp.split(u, 2, axis=-1)
        x = x + (jax.nn.silu(gate) * val) @ w_down[l].astype(h.dtype)
    return rmsnorm(x, final_norm_g)
```

```python
import functools
import math

import jax
import jax.numpy as jnp
from jax import lax
from jax.experimental import pallas as pl
from jax.experimental.pallas import tpu as pltpu

D_MODEL = 1024
HEAD_DIM = 64
ATTN_WIDTH = 512
ATTN_HEADS = 8
KV_HEADS = 2
GQA_GROUP = 4
ROT_DIM = 16
ROPE_THETA = 500000.0
WINDOW = 128
BLOCK = 128
SSM_INNER = 512
SSM_HEAD_DIM = 64
SSM_HEADS = 8
SSM_GROUPS = 2
SSM_HPG = 4
D_STATE = 128
SSM_CONV = 5
CHUNK = 128
D_FF = 2816
FFN_CONV = 3
EPS = 1e-5
NEG = -1e30

KV_COLS = KV_HEADS * HEAD_DIM
XBC_COLS = SSM_INNER + 2 * SSM_GROUPS * D_STATE
MAIN_COLS = ATTN_WIDTH + 2 * KV_COLS + SSM_INNER + XBC_COLS
DT_COLS = 2 * SSM_HEADS
GROUP_W = SSM_HPG * SSM_HEAD_DIM

LANE = 128
HALO = 16
VMEM_LIMIT = 56 * 1024 * 1024

F32 = jnp.float32
BF16 = jnp.bfloat16


def _rms(x, g):
    return x * lax.rsqrt(jnp.mean(x * x, axis=-1, keepdims=True) + EPS) * g


def _silu(x):
    return x * (1.0 / (1.0 + jnp.exp(-x)))


def _softplus(x):
    return jnp.maximum(x, 0.0) + jnp.log1p(jnp.exp(-jnp.abs(x)))


def _inproj_kernel(x_ref, g_ref, w_ref, wdt_ref, cos_ref, sa_ref, sb_ref,
                   q_ref, kv_ref, z_ref, xbc_ref, dt_ref, dtT_ref):
    hb = _rms(x_ref[...], g_ref[...]).astype(BF16)
    proj = jnp.dot(hb, w_ref[...], preferred_element_type=F32)
    dt = jnp.dot(hb, wdt_ref[...], preferred_element_type=F32)
    c, sa, sb = cos_ref[...], sa_ref[...], sb_ref[...]

    def rope(t):
        n = t.shape[1]
        reps = n // LANE
        cc = jnp.concatenate([c] * reps, axis=1) if reps > 1 else c
        aa = jnp.concatenate([sa] * reps, axis=1) if reps > 1 else sa
        bb = jnp.concatenate([sb] * reps, axis=1) if reps > 1 else sb
        half = ROT_DIM // 2
        return t * cc + pltpu.roll(t, n - half, 1) * aa + pltpu.roll(t, half, 1) * bb

    o = 0
    q = rope(proj[:, o:o + ATTN_WIDTH]) * (1.0 / math.sqrt(HEAD_DIM))
    o += ATTN_WIDTH
    k = rope(proj[:, o:o + KV_COLS])
    o += KV_COLS
    v = proj[:, o:o + KV_COLS]
    o += KV_COLS
    q_ref[...] = q.astype(BF16)
    kv_ref[:, :KV_COLS] = k.astype(BF16)
    kv_ref[:, KV_COLS:] = v.astype(BF16)
    z_ref[...] = proj[:, o:o + SSM_INNER].astype(BF16)
    o += SSM_INNER
    xbc_ref[...] = proj[:, o:o + XBC_COLS].astype(BF16)
    dt_ref[...] = dt[:, :DT_COLS]
    dtT_ref[...] = dt.T[:DT_COLS, :]


def _rope_tables(S):
    half = ROT_DIM // 2
    pos = jnp.arange(S, dtype=F32)
    inv = ROPE_THETA ** (-jnp.arange(0, ROT_DIM, 2, dtype=F32) / ROT_DIM)
    ang = pos[:, None] * inv[None, :]
    cos, sin = jnp.cos(ang), jnp.sin(ang)
    rest = HEAD_DIM - ROT_DIM
    ones, zeros, zh = jnp.ones((S, rest), F32), jnp.zeros((S, rest), F32), jnp.zeros((S, half), F32)
    c = jnp.concatenate([cos, cos, ones], axis=1)
    sa = jnp.concatenate([-sin, zh, zeros], axis=1)
    sb = jnp.concatenate([zh, sin, zeros], axis=1)
    rep = LANE // HEAD_DIM
    return tuple(jnp.tile(t, (1, rep)) for t in (c, sa, sb))


def _in_proj(x2d, g1, w_main, w_dt, tables, S, tm):
    T = x2d.shape[0]
    n_seq_tiles = S // tm
    row = lambda i: (i, 0)
    const = lambda i: (0, 0)
    tab = lambda i: (i % n_seq_tiles, 0)
    return pl.pallas_call(
        _inproj_kernel,
        grid=(T // tm,),
        in_specs=[
            pl.BlockSpec((tm, D_MODEL), row),
            pl.BlockSpec((1, D_MODEL), const),
            pl.BlockSpec((D_MODEL, MAIN_COLS), const),
            pl.BlockSpec((D_MODEL, LANE), const),
            pl.BlockSpec((tm, LANE), tab),
            pl.BlockSpec((tm, LANE), tab),
            pl.BlockSpec((tm, LANE), tab),
        ],
        out_specs=[
            pl.BlockSpec((tm, ATTN_WIDTH), row),
            pl.BlockSpec((tm, 2 * KV_COLS), row),
            pl.BlockSpec((tm, SSM_INNER), row),
            pl.BlockSpec((tm, XBC_COLS), row),
            pl.BlockSpec((tm, DT_COLS), row),
            pl.BlockSpec((DT_COLS, tm), lambda i: (0, i)),
        ],
        out_shape=[
            jax.ShapeDtypeStruct((T, ATTN_WIDTH), BF16),
            jax.ShapeDtypeStruct((T, 2 * KV_COLS), BF16),
            jax.ShapeDtypeStruct((T, SSM_INNER), BF16),
            jax.ShapeDtypeStruct((T, XBC_COLS), BF16),
            jax.ShapeDtypeStruct((T, DT_COLS), F32),
            jax.ShapeDtypeStruct((DT_COLS, T), F32),
        ],
        compiler_params=pltpu.CompilerParams(
            dimension_semantics=("arbitrary",), vmem_limit_bytes=VMEM_LIMIT),
        name="in_proj",
    )(x2d, g1, w_main, w_dt, *tables)


def _attn_kernel(sink_ref, q_ref, kvp_ref, kvm_ref, kvn_ref, g_ref, o_ref, kv_s, *, tq):
    i = pl.program_id(1)
    n = pl.num_programs(1)
    kv_s[0:BLOCK, :] = kvp_ref[...]
    kv_s[BLOCK:BLOCK + tq, :] = kvm_ref[...]
    kv_s[BLOCK + tq:, :] = kvn_ref[...]

    band = 3 * BLOCK
    row = lax.broadcasted_iota(jnp.int32, (GQA_GROUP * BLOCK, band), 0) % BLOCK
    col = lax.broadcasted_iota(jnp.int32, (GQA_GROUP * BLOCK, band), 1)
    in_window = jnp.abs(col - BLOCK - row) <= WINDOW
    first_key = jnp.where(i > 0, 0, BLOCK)
    last_key = jnp.where(i < n - 1, band, 2 * BLOCK)

    nsub = tq // BLOCK
    for j in range(nsub):
        valid = in_window
        if j == 0:
            valid = valid & (col >= first_key)
        if j == nsub - 1:
            valid = valid & (col < last_key)
        qj = q_ref[j * BLOCK:(j + 1) * BLOCK, :]
        kvj = kv_s[j * BLOCK:j * BLOCK + band, :]
        outs = []
        for kh in range(KV_HEADS):
            kk = kvj[:, kh * HEAD_DIM:(kh + 1) * HEAD_DIM]
            vv = kvj[:, KV_COLS + kh * HEAD_DIM:KV_COLS + (kh + 1) * HEAD_DIM]
            heads = [kh * GQA_GROUP + g for g in range(GQA_GROUP)]
            qs = jnp.concatenate([qj[:, h * HEAD_DIM:(h + 1) * HEAD_DIM] for h in heads], axis=0)
            s = lax.dot_general(qs, kk, (((1,), (1,)), ((), ())), preferred_element_type=F32)
            s = jnp.where(valid, s, NEG)
            sink = jnp.concatenate([jnp.full((BLOCK, 1), sink_ref[h], F32) for h in heads], axis=0)
            m = jnp.maximum(jnp.max(s, axis=-1, keepdims=True), sink)
            p = jnp.exp(s - m)
            denom = jnp.sum(p, axis=-1, keepdims=True) + jnp.exp(sink - m)
            o = jnp.dot(p.astype(BF16), vv, preferred_element_type=F32) / denom
            outs += [o[g * BLOCK:(g + 1) * BLOCK, :] for g in range(GQA_GROUP)]
        oj = jnp.concatenate(outs, axis=1)
        o_ref[j * BLOCK:(j + 1) * BLOCK, :] = _rms(oj, g_ref[...]).astype(BF16)


def _attention(q, kv, sink, g, B, S, tq):
    T = q.shape[0]
    nq = S // tq
    sub = tq // BLOCK
    nblk = S // BLOCK
    main = lambda b, i: (b * nq + i, 0)
    prev = lambda b, i: (b * nblk + jnp.maximum(i * sub - 1, 0), 0)
    nxt = lambda b, i: (b * nblk + jnp.minimum((i + 1) * sub, nblk - 1), 0)
    return pl.pallas_call(
        functools.partial(_attn_kernel, tq=tq),
        grid=(B, nq),
        in_specs=[
            pl.BlockSpec(memory_space=pltpu.SMEM),
            pl.BlockSpec((tq, ATTN_WIDTH), main),
            pl.BlockSpec((BLOCK, 2 * KV_COLS), prev),
            pl.BlockSpec((tq, 2 * KV_COLS), main),
            pl.BlockSpec((BLOCK, 2 * KV_COLS), nxt),
            pl.BlockSpec((1, ATTN_WIDTH), lambda b, i: (0, 0)),
        ],
        out_specs=pl.BlockSpec((tq, ATTN_WIDTH), main),
        out_shape=jax.ShapeDtypeStruct((T, ATTN_WIDTH), BF16),
        scratch_shapes=[pltpu.VMEM((tq + 2 * BLOCK, 2 * KV_COLS), BF16)],
        compiler_params=pltpu.CompilerParams(
            dimension_semantics=("arbitrary", "arbitrary"), vmem_limit_bytes=VMEM_LIMIT),
        name="attention",
    )(sink, q, kv, kv, kv, g)


def _ssd_kernel(*refs, reverse, ts):
    if reverse:
        (xp_ref, xm_ref, xn_ref, dt_ref, dtT_ref, cw_ref, cb_ref, bias_ref, biasT_ref, a_ref, aT_ref,
         e_ref, z_ref, yf_ref, ng_ref, o_ref, xext_s, h_s) = refs
    else:
        (xp_ref, xm_ref, xn_ref, dt_ref, dtT_ref, cw_ref, cb_ref, bias_ref, biasT_ref, a_ref, aT_ref,
         e_ref, dskip_ref, o_ref, xext_s, h_s) = refs
    t = pl.program_id(1)
    nt = pl.num_programs(1)
    tile = (nt - 1 - t) if reverse else t
    d0 = SSM_HEADS if reverse else 0

    @pl.when(t == 0)
    def _():
        h_s[...] = jnp.zeros_like(h_s)

    xext_s[0:HALO, :] = jnp.where(tile > 0, xp_ref[...].astype(F32), 0.0)
    xext_s[HALO:HALO + ts, :] = xm_ref[...].astype(F32)
    xext_s[HALO + ts:, :] = jnp.where(tile < nt - 1, xn_ref[...].astype(F32), 0.0)

    li = lax.broadcasted_iota(jnp.int32, (CHUNK, CHUNK), 0)
    si = lax.broadcasted_iota(jnp.int32, (CHUNK, CHUNK), 1)
    causal = (li <= si) if reverse else (li >= si)
    tri = causal.astype(F32)
    triT = ((si <= li) if reverse else (si >= li)).astype(F32)
    last = 0 if reverse else CHUNK - 1
    hi = lax.Precision.HIGHEST

    nchunk = ts // CHUNK
    order = range(nchunk - 1, -1, -1) if reverse else range(nchunk)
    for c in order:
        r0 = c * CHUNK
        acc = cb_ref[...] + xext_s[r0 + HALO - 2:r0 + HALO - 2 + CHUNK, :] * cw_ref[0:1, :]
        for k in range(1, SSM_CONV):
            off = r0 + HALO - SSM_CONV // 2 + k
            acc = acc + xext_s[off:off + CHUNK, :] * cw_ref[k:k + 1, :]
        xc = _silu(acc)
        xs = xc[:, :SSM_INNER]
        bm = xc[:, SSM_INNER:SSM_INNER + SSM_GROUPS * D_STATE]
        cm = xc[:, SSM_INNER + SSM_GROUPS * D_STATE:]

        dt = _softplus(dt_ref[r0:r0 + CHUNK, :] + bias_ref[...])
        dtT = _softplus(dtT_ref[:, r0:r0 + CHUNK] + biasT_ref[...])
        acs = jnp.dot(tri, dt * a_ref[...], precision=hi, preferred_element_type=F32)
        acsT = jnp.dot(dtT * aT_ref[...], triT, precision=hi, preferred_element_type=F32)
        dt_e = jnp.dot(dt, e_ref[...], precision=hi, preferred_element_type=F32)
        acs_e = jnp.dot(acs, e_ref[...], precision=hi, preferred_element_type=F32)
        acs_last = acs_e[last:last + 1, :]
        x_dt = xs * dt_e
        x_b = x_dt.astype(BF16)
        x_decay = (x_dt * jnp.exp(acs_last - acs_e)).astype(BF16)
        chunk_decay = jnp.exp(acs_last)
        exp_acs = jnp.exp(acs_e)

        ys = []
        for g in range(SSM_GROUPS):
            bg = bm[:, g * D_STATE:(g + 1) * D_STATE]
            cg = cm[:, g * D_STATE:(g + 1) * D_STATE].astype(BF16)
            cb = lax.dot_general(cg, bg.astype(BF16), (((1,), (1,)), ((), ())),
                                 preferred_element_type=F32)
            gs = slice(g * GROUP_W, (g + 1) * GROUP_W)
            st = jnp.dot(bg.T.astype(BF16), x_decay[:, gs], preferred_element_type=F32)
            h_prev = h_s[g]
            y_off = jnp.dot(cg, h_prev.astype(BF16), preferred_element_type=F32)
            h_s[g] = h_prev * chunk_decay[:, gs] + st
            y_diag = []
            for rr in range(SSM_HPG):
                r = g * SSM_HPG + rr
                seg = acs[:, d0 + r:d0 + r + 1] - acsT[d0 + r:d0 + r + 1, :]
                lmat = jnp.exp(jnp.where(causal, seg, -jnp.inf))
                mm = (cb * lmat).astype(BF16)
                y_diag.append(jnp.dot(mm, x_b[:, r * SSM_HEAD_DIM:(r + 1) * SSM_HEAD_DIM],
                                      preferred_element_type=F32))
            ys.append(jnp.concatenate(y_diag, axis=1) + y_off * exp_acs[:, gs])
        y = jnp.concatenate(ys, axis=1)

        if not reverse:
            o_ref[r0:r0 + CHUNK, :] = y + xs * dskip_ref[...]
        else:
            y = (y + yf_ref[r0:r0 + CHUNK, :]) * _silu(z_ref[r0:r0 + CHUNK, :].astype(F32))
            parts = []
            for g in range(SSM_GROUPS):
                yg = y[:, g * GROUP_W:(g + 1) * GROUP_W]
                parts.append(yg * lax.rsqrt(jnp.mean(yg * yg, axis=-1, keepdims=True) + EPS))
            o_ref[r0:r0 + CHUNK, :] = (jnp.concatenate(parts, axis=1) * ng_ref[...]).astype(BF16)


def _ssd_pass(xbc, dt, dtT, conv_w, conv_b, bias, biasT, a, aT, e, extra, B, S, ts, reverse):
    T = xbc.shape[0]
    nt = S // ts
    hb = ts // HALO
    nh = S // HALO
    seq = (lambda t: nt - 1 - t) if reverse else (lambda t: t)
    main = lambda b, t: (b * nt + seq(t), 0)
    prev = lambda b, t: (b * nh + jnp.maximum(seq(t) * hb - 1, 0), 0)
    nxt = lambda b, t: (b * nh + jnp.minimum((seq(t) + 1) * hb, nh - 1), 0)
    const = lambda b, t: (0, 0)
    in_specs = [
        pl.BlockSpec((HALO, XBC_COLS), prev),
        pl.BlockSpec((ts, XBC_COLS), main),
        pl.BlockSpec((HALO, XBC_COLS), nxt),
        pl.BlockSpec((ts, DT_COLS), main),
        pl.BlockSpec((DT_COLS, ts), lambda b, t: (0, b * nt + seq(t))),
        pl.BlockSpec((SSM_CONV, XBC_COLS), const),
        pl.BlockSpec((1, XBC_COLS), const),
        pl.BlockSpec((1, DT_COLS), const),
        pl.BlockSpec((DT_COLS, 1), const),
        pl.BlockSpec((1, DT_COLS), const),
        pl.BlockSpec((DT_COLS, 1), const),
        pl.BlockSpec((DT_COLS, SSM_INNER), const),
    ]
    args = [xbc, xbc, xbc, dt, dtT, conv_w, conv_b, bias, biasT, a, aT, e]
    if reverse:
        z, yf, ng = extra
        in_specs += [pl.BlockSpec((ts, SSM_INNER), main), pl.BlockSpec((ts, SSM_INNER), main),
                     pl.BlockSpec((1, SSM_INNER), const)]
        args += [z, yf, ng]
        out_dtype = BF16
    else:
        (dskip,) = extra
        in_specs += [pl.BlockSpec((1, SSM_INNER), const)]
        args += [dskip]
        out_dtype = F32
    return pl.pallas_call(
        functools.partial(_ssd_kernel, reverse=reverse, ts=ts),
        grid=(B, nt),
        in_specs=in_specs,
        out_specs=pl.BlockSpec((ts, SSM_INNER), main),
        out_shape=jax.ShapeDtypeStruct((T, SSM_INNER), out_dtype),
        scratch_shapes=[pltpu.VMEM((ts + 2 * HALO, XBC_COLS), F32),
                        pltpu.VMEM((SSM_GROUPS, D_STATE, GROUP_W), F32)],
        compiler_params=pltpu.CompilerParams(
            dimension_semantics=("arbitrary", "arbitrary"), vmem_limit_bytes=VMEM_LIMIT),
        name="ssd_bwd" if reverse else "ssd_fwd",
    )(*args)


def _outproj_kernel(x_ref, a_ref, s_ref, wa_ref, ws_ref, g_ref, x1_ref, h_ref):
    x1 = (x_ref[...]
          + jnp.dot(a_ref[...], wa_ref[...], preferred_element_type=F32)
          + jnp.dot(s_ref[...], ws_ref[...], preferred_element_type=F32))
    x1_ref[...] = x1
    h_ref[...] = _rms(x1, g_ref[...]).astype(BF16)


def _out_proj(x2d, attn, ssd, w_attn, w_ssd, g2, tm):
    T = x2d.shape[0]
    row = lambda i: (i, 0)
    const = lambda i: (0, 0)
    return pl.pallas_call(
        _outproj_kernel,
        grid=(T // tm,),
        in_specs=[
            pl.BlockSpec((tm, D_MODEL), row),
            pl.BlockSpec((tm, ATTN_WIDTH), row),
            pl.BlockSpec((tm, SSM_INNER), row),
            pl.BlockSpec((ATTN_WIDTH, D_MODEL), const),
            pl.BlockSpec((SSM_INNER, D_MODEL), const),
            pl.BlockSpec((1, D_MODEL), const),
        ],
        out_specs=[pl.BlockSpec((tm, D_MODEL), row), pl.BlockSpec((tm, D_MODEL), row)],
        out_shape=[jax.ShapeDtypeStruct((T, D_MODEL), F32), jax.ShapeDtypeStruct((T, D_MODEL), BF16)],
        compiler_params=pltpu.CompilerParams(
            dimension_semantics=("arbitrary",), vmem_limit_bytes=VMEM_LIMIT),
        name="out_proj",
    )(x2d, attn, ssd, w_attn, w_ssd, g2)


FFN_COLS = 256


def _ffn_kernel(hp_ref, hm_ref, hn_ref, x1_ref, wup_ref, cw_ref, cb_ref, wdn_ref, fg_ref, o_ref,
                hext_s, u_s, acc_s, *, tm, n_seq_tiles):
    tile = pl.program_id(0) % n_seq_tiles
    hext_s[0:HALO, :] = jnp.where(tile > 0, hp_ref[...], jnp.zeros_like(hp_ref))
    hext_s[HALO:HALO + tm, :] = hm_ref[...]
    hext_s[HALO + tm:, :] = jnp.where(tile < n_seq_tiles - 1, hn_ref[...], jnp.zeros_like(hn_ref))
    acc_s[...] = x1_ref[...]
    f = FFN_COLS
    for c in range(D_FF // f):
        gcols = slice(c * f, (c + 1) * f)
        vcols = slice(D_FF + c * f, D_FF + (c + 1) * f)
        hext = hext_s[...]
        u_s[:, :f] = jnp.dot(hext, wup_ref[:, gcols], preferred_element_type=F32)
        u_s[:, f:] = jnp.dot(hext, wup_ref[:, vcols], preferred_element_type=F32)

        def conv(lo, cols):
            out = cb_ref[:, cols]
            for k in range(FFN_CONV):
                off = HALO - FFN_CONV // 2 + k
                out = out + u_s[off:off + tm, lo:lo + f] * cw_ref[k:k + 1, cols]
            return out

        glu = (_silu(conv(0, gcols)) * conv(f, vcols)).astype(BF16)
        acc_s[...] += jnp.dot(glu, wdn_ref[gcols, :], preferred_element_type=F32)
    o_ref[...] = _rms(acc_s[...], fg_ref[...])


def _ffn(h2, x1, w_up, conv_w, conv_b, w_down, fg, S, tm):
    T = h2.shape[0]
    n_seq_tiles = S // tm
    hb = tm // HALO
    nh = T // HALO
    row = lambda i: (i, 0)
    const = lambda i: (0, 0)
    prev = lambda i: (jnp.maximum(i * hb - 1, 0), 0)
    nxt = lambda i: (jnp.minimum((i + 1) * hb, nh - 1), 0)
    return pl.pallas_call(
        functools.partial(_ffn_kernel, tm=tm, n_seq_tiles=n_seq_tiles),
        grid=(T // tm,),
        in_specs=[
            pl.BlockSpec((HALO, D_MODEL), prev),
            pl.BlockSpec((tm, D_MODEL), row),
            pl.BlockSpec((HALO, D_MODEL), nxt),
            pl.BlockSpec((tm, D_MODEL), row),
            pl.BlockSpec((D_MODEL, 2 * D_FF), const),
            pl.BlockSpec((FFN_CONV, 2 * D_FF), const),
            pl.BlockSpec((1, 2 * D_FF), const),
            pl.BlockSpec((D_FF, D_MODEL), const),
            pl.BlockSpec((1, D_MODEL), const),
        ],
        out_specs=pl.BlockSpec((tm, D_MODEL), row),
        out_shape=jax.ShapeDtypeStruct((T, D_MODEL), F32),
        scratch_shapes=[pltpu.VMEM((tm + 2 * HALO, D_MODEL), BF16),
                        pltpu.VMEM((tm + 2 * HALO, 2 * FFN_COLS), F32),
                        pltpu.VMEM((tm, D_MODEL), F32)],
        compiler_params=pltpu.CompilerParams(
            dimension_semantics=("arbitrary",), vmem_limit_bytes=VMEM_LIMIT),
        name="ffn",
    )(h2, h2, h2, x1, w_up, conv_w, conv_b, w_down, fg)


def _layer(x2d, B, S, norm1_g, w_in, attn_sink, attn_out_g, ssd_conv_w, ssd_conv_b,
           dt_bias_f, dt_bias_b, a_log_f, a_log_b, ssd_d, ssd_norm_g, w_out, norm2_g,
           w_up, ffn_conv_w, ffn_conv_b, w_down, out_g, tables, tm, ts, tq):
    row = lambda v: v.reshape(1, -1).astype(F32)
    w_main = w_in[:, :MAIN_COLS].astype(BF16)
    w_dt = jnp.pad(w_in[:, MAIN_COLS:], ((0, 0), (0, LANE - DT_COLS))).astype(BF16)
    q, kv, z, xbc, dt, dtT = _in_proj(x2d, row(norm1_g), w_main, w_dt, tables, S, tm)

    attn = _attention(q, kv, attn_sink.astype(F32), row(attn_out_g), B, S, tq)

    bias = jnp.concatenate([dt_bias_f, dt_bias_b]).astype(F32)
    a = -jnp.exp(jnp.concatenate([a_log_f, a_log_b]).astype(F32))
    head_of_col = jnp.arange(SSM_INNER) // SSM_HEAD_DIM
    cw, cb = ssd_conv_w.astype(F32), row(ssd_conv_b)
    ys = None
    for reverse in (False, True):
        d0 = SSM_HEADS if reverse else 0
        e = (jnp.arange(DT_COLS)[:, None] == (head_of_col + d0)[None, :]).astype(F32)
        extra = (z, ys, row(ssd_norm_g)) if reverse else (row(jnp.repeat(ssd_d.astype(F32), SSM_HEAD_DIM)),)
        ys = _ssd_pass(xbc, dt, dtT, cw, cb, bias.reshape(1, -1), bias.reshape(-1, 1),
                       a.reshape(1, -1), a.reshape(-1, 1), e, extra, B, S, ts, reverse)

    w_o = w_out.astype(BF16)
    x1, h2 = _out_proj(x2d, attn, ys, w_o[:ATTN_WIDTH], w_o[ATTN_WIDTH:], row(norm2_g), tm)
    return _ffn(h2, x1, w_up.astype(BF16), ffn_conv_w.astype(F32), row(ffn_conv_b),
                w_down.astype(BF16), out_g, S, tm)


def kernel(x, norm1_g, w_in, attn_sink, attn_out_g, ssd_conv_w, ssd_conv_b, ssd_dt_bias_fwd,
           ssd_dt_bias_bwd, ssd_a_log_fwd, ssd_a_log_bwd, ssd_d, ssd_norm_g, w_out, norm2_g, w_up,
           ffn_conv_w, ffn_conv_b, w_down, final_norm_g, *, tm=512, ts=512, tq=512):
    B, S, _ = x.shape
    depth = w_in.shape[0]
    assert depth == 1, "the fused final RMSNorm assumes a single layer"
    tables = _rope_tables(S)
    x2d = x.reshape(B * S, D_MODEL)
    out = _layer(x2d, B, S, norm1_g[0], w_in[0], attn_sink[0], attn_out_g[0], ssd_conv_w[0], ssd_conv_b[0],
                 ssd_dt_bias_fwd[0], ssd_dt_bias_bwd[0], ssd_a_log_fwd[0], ssd_a_log_bwd[0], ssd_d[0],
                 ssd_norm_g[0], w_out[0], norm2_g[0], w_up[0], ffn_conv_w[0], ffn_conv_b[0], w_down[0],
                 final_norm_g.reshape(1, -1).astype(F32), tables, tm, ts, tq)
    return out.reshape(B, S, D_MODEL)
```

```python
import functools
import math

import jax
import jax.numpy as jnp
from jax import lax
from jax.experimental import pallas as pl
from jax.experimental.pallas import tpu as pltpu

D_MODEL = 1024
HEAD_DIM = 64
ATTN_WIDTH = 512
ATTN_HEADS = 8
KV_HEADS = 2
GQA_GROUP = 4
ROT_DIM = 16
ROPE_THETA = 500000.0
WINDOW = 128
BLOCK = 128
SSM_INNER = 512
SSM_HEAD_DIM = 64
SSM_HEADS = 8
SSM_GROUPS = 2
SSM_HPG = 4
D_STATE = 128
SSM_CONV = 5
CHUNK = 128
D_FF = 2816
FFN_CONV = 3
EPS = 1e-5
NEG = -1e30

KV_COLS = KV_HEADS * HEAD_DIM
XBC_COLS = SSM_INNER + 2 * SSM_GROUPS * D_STATE
MAIN_COLS = ATTN_WIDTH + 2 * KV_COLS + SSM_INNER + XBC_COLS
DT_COLS = 2 * SSM_HEADS
GROUP_W = SSM_HPG * SSM_HEAD_DIM

LANE = 128
HALO = 16
VMEM_LIMIT = 56 * 1024 * 1024

F32 = jnp.float32
BF16 = jnp.bfloat16


def _rms(x, g):
    return x * lax.rsqrt(jnp.mean(x * x, axis=-1, keepdims=True) + EPS) * g


def _silu(x):
    return x * (1.0 / (1.0 + jnp.exp(-x)))


def _softplus(x):
    return jnp.maximum(x, 0.0) + jnp.log1p(jnp.exp(-jnp.abs(x)))


def _inproj_kernel(x_ref, g_ref, w_ref, wdt_ref, cos_ref, sa_ref, sb_ref,
                   q_ref, k_ref, vT_ref, z_ref, xbc_ref, dt_ref, dtT_ref):
    hb = _rms(x_ref[...], g_ref[...]).astype(BF16)
    proj = jnp.dot(hb, w_ref[...], preferred_element_type=F32)
    dt = jnp.dot(hb, wdt_ref[...], preferred_element_type=F32)
    c, sa, sb = cos_ref[...], sa_ref[...], sb_ref[...]

    def rope(t):
        n = t.shape[1]
        reps = n // LANE
        cc = jnp.concatenate([c] * reps, axis=1) if reps > 1 else c
        aa = jnp.concatenate([sa] * reps, axis=1) if reps > 1 else sa
        bb = jnp.concatenate([sb] * reps, axis=1) if reps > 1 else sb
        half = ROT_DIM // 2
        return t * cc + pltpu.roll(t, n - half, 1) * aa + pltpu.roll(t, half, 1) * bb

    o = 0
    q = rope(proj[:, o:o + ATTN_WIDTH]) * (1.0 / math.sqrt(HEAD_DIM))
    o += ATTN_WIDTH
    k = rope(proj[:, o:o + KV_COLS])
    o += KV_COLS
    v = proj[:, o:o + KV_COLS]
    o += KV_COLS
    q_ref[...] = q.astype(BF16)
    k_ref[...] = k.astype(BF16)
    vT_ref[...] = v.T.astype(BF16)
    z_ref[...] = proj[:, o:o + SSM_INNER].astype(BF16)
    o += SSM_INNER
    xbc_ref[...] = proj[:, o:o + XBC_COLS].astype(BF16)
    dt_ref[...] = dt[:, :DT_COLS]
    dtT_ref[...] = dt.T[:DT_COLS, :]


def _rope_tables(S):
    half = ROT_DIM // 2
    pos = jnp.arange(S, dtype=F32)
    inv = ROPE_THETA ** (-jnp.arange(0, ROT_DIM, 2, dtype=F32) / ROT_DIM)
    ang = pos[:, None] * inv[None, :]
    cos, sin = jnp.cos(ang), jnp.sin(ang)
    rest = HEAD_DIM - ROT_DIM
    ones, zeros, zh = jnp.ones((S, rest), F32), jnp.zeros((S, rest), F32), jnp.zeros((S, half), F32)
    c = jnp.concatenate([cos, cos, ones], axis=1)
    sa = jnp.concatenate([-sin, zh, zeros], axis=1)
    sb = jnp.concatenate([zh, sin, zeros], axis=1)
    rep = LANE // HEAD_DIM
    return tuple(jnp.tile(t, (1, rep)) for t in (c, sa, sb))


def _in_proj(x2d, g1, w_main, w_dt, tables, S, tm):
    T = x2d.shape[0]
    n_seq_tiles = S // tm
    row = lambda i: (i, 0)
    const = lambda i: (0, 0)
    tab = lambda i: (i % n_seq_tiles, 0)
    return pl.pallas_call(
        _inproj_kernel,
        grid=(T // tm,),
        in_specs=[
            pl.BlockSpec((tm, D_MODEL), row),
            pl.BlockSpec((1, D_MODEL), const),
            pl.BlockSpec((D_MODEL, MAIN_COLS), const),
            pl.BlockSpec((D_MODEL, LANE), const),
            pl.BlockSpec((tm, LANE), tab),
            pl.BlockSpec((tm, LANE), tab),
            pl.BlockSpec((tm, LANE), tab),
        ],
        out_specs=[
            pl.BlockSpec((tm, ATTN_WIDTH), row),
            pl.BlockSpec((tm, KV_COLS), row),
            pl.BlockSpec((KV_COLS, tm), lambda i: (0, i)),
            pl.BlockSpec((tm, SSM_INNER), row),
            pl.BlockSpec((tm, XBC_COLS), row),
            pl.BlockSpec((tm, DT_COLS), row),
            pl.BlockSpec((DT_COLS, tm), lambda i: (0, i)),
        ],
        out_shape=[
            jax.ShapeDtypeStruct((T, ATTN_WIDTH), BF16),
            jax.ShapeDtypeStruct((T, KV_COLS), BF16),
            jax.ShapeDtypeStruct((KV_COLS, T), BF16),
            jax.ShapeDtypeStruct((T, SSM_INNER), BF16),
            jax.ShapeDtypeStruct((T, XBC_COLS), BF16),
            jax.ShapeDtypeStruct((T, DT_COLS), F32),
            jax.ShapeDtypeStruct((DT_COLS, T), F32),
        ],
        compiler_params=pltpu.CompilerParams(
            dimension_semantics=("arbitrary",), vmem_limit_bytes=VMEM_LIMIT),
        name="in_proj",
    )(x2d, g1, w_main, w_dt, *tables)


def _attn_kernel(sink_ref, q_ref, kp_ref, km_ref, kn_ref, vp_ref, vm_ref, vn_ref, g_ref, o_ref,
                 k_s, vT_s, *, tq):
    i = pl.program_id(1)
    n = pl.num_programs(1)
    k_s[0:BLOCK, :] = kp_ref[...]
    k_s[BLOCK:BLOCK + tq, :] = km_ref[...]
    k_s[BLOCK + tq:, :] = kn_ref[...]
    vT_s[:, 0:BLOCK] = vp_ref[...]
    vT_s[:, BLOCK:BLOCK + tq] = vm_ref[...]
    vT_s[:, BLOCK + tq:] = vn_ref[...]

    band = 3 * BLOCK
    key = lax.broadcasted_iota(jnp.int32, (band, GQA_GROUP * BLOCK), 0)
    qry = lax.broadcasted_iota(jnp.int32, (band, GQA_GROUP * BLOCK), 1) % BLOCK
    in_window = jnp.abs(key - BLOCK - qry) <= WINDOW
    first_key = jnp.where(i > 0, 0, BLOCK)
    last_key = jnp.where(i < n - 1, band, 2 * BLOCK)

    nsub = tq // BLOCK
    for j in range(nsub):
        valid = in_window
        if j == 0:
            valid = valid & (key >= first_key)
        if j == nsub - 1:
            valid = valid & (key < last_key)
        qj = q_ref[j * BLOCK:(j + 1) * BLOCK, :]
        kj = k_s[j * BLOCK:j * BLOCK + band, :]
        vTj = vT_s[:, j * BLOCK:j * BLOCK + band]
        outs = []
        for kh in range(KV_HEADS):
            heads = [kh * GQA_GROUP + g for g in range(GQA_GROUP)]
            kk = kj[:, kh * HEAD_DIM:(kh + 1) * HEAD_DIM]
            qs = jnp.concatenate([qj[:, h * HEAD_DIM:(h + 1) * HEAD_DIM] for h in heads], axis=0)
            sT = lax.dot_general(kk, qs, (((1,), (1,)), ((), ())), preferred_element_type=F32)
            sT = jnp.where(valid, sT, NEG)
            sink = sink_ref[kh:kh + 1, :]
            m = jnp.maximum(jnp.max(sT, axis=0, keepdims=True), sink)
            p = jnp.exp(sT - m)
            denom = jnp.sum(p, axis=0, keepdims=True) + jnp.exp(sink - m)
            oT = jnp.dot(vTj[kh * HEAD_DIM:(kh + 1) * HEAD_DIM, :], p.astype(BF16),
                         preferred_element_type=F32) / denom
            outs += [oT[:, g * BLOCK:(g + 1) * BLOCK] for g in range(GQA_GROUP)]
        oT_all = jnp.concatenate(outs, axis=0)
        inv = lax.rsqrt(jnp.mean(oT_all * oT_all, axis=0, keepdims=True) + EPS)
        o_ref[j * BLOCK:(j + 1) * BLOCK, :] = ((oT_all * inv).T * g_ref[...]).astype(BF16)


def _attention(q, k, vT, sink_rows, g, B, S, tq):
    T = q.shape[0]
    nq = S // tq
    sub = tq // BLOCK
    nblk = S // BLOCK
    main = lambda b, i: (b * nq + i, 0)
    prev = lambda b, i: (b * nblk + jnp.maximum(i * sub - 1, 0), 0)
    nxt = lambda b, i: (b * nblk + jnp.minimum((i + 1) * sub, nblk - 1), 0)
    swap = lambda f: (lambda b, i: f(b, i)[::-1])
    const = lambda b, i: (0, 0)
    return pl.pallas_call(
        functools.partial(_attn_kernel, tq=tq),
        grid=(B, nq),
        in_specs=[
            pl.BlockSpec((KV_HEADS, GQA_GROUP * BLOCK), const),
            pl.BlockSpec((tq, ATTN_WIDTH), main),
            pl.BlockSpec((BLOCK, KV_COLS), prev),
            pl.BlockSpec((tq, KV_COLS), main),
            pl.BlockSpec((BLOCK, KV_COLS), nxt),
            pl.BlockSpec((KV_COLS, BLOCK), swap(prev)),
            pl.BlockSpec((KV_COLS, tq), swap(main)),
            pl.BlockSpec((KV_COLS, BLOCK), swap(nxt)),
            pl.BlockSpec((1, ATTN_WIDTH), const),
        ],
        out_specs=pl.BlockSpec((tq, ATTN_WIDTH), main),
        out_shape=jax.ShapeDtypeStruct((T, ATTN_WIDTH), BF16),
        scratch_shapes=[pltpu.VMEM((tq + 2 * BLOCK, KV_COLS), BF16),
                        pltpu.VMEM((KV_COLS, tq + 2 * BLOCK), BF16)],
        compiler_params=pltpu.CompilerParams(
            dimension_semantics=("arbitrary", "arbitrary"), vmem_limit_bytes=VMEM_LIMIT),
        name="attention",
    )(sink_rows, q, k, k, k, vT, vT, vT, g)


def _ssd_kernel(*refs, reverse, ts):
    if reverse:
        (xp_ref, xm_ref, xn_ref, dt_ref, dtT_ref, cw_ref, cb_ref, bias_ref, biasT_ref, a_ref, aT_ref,
         e_ref, z_ref, yf_ref, ng_ref, o_ref, xext_s, h_s) = refs
    else:
        (xp_ref, xm_ref, xn_ref, dt_ref, dtT_ref, cw_ref, cb_ref, bias_ref, biasT_ref, a_ref, aT_ref,
         e_ref, dskip_ref, o_ref, xext_s, h_s) = refs
    t = pl.program_id(1)
    nt = pl.num_programs(1)
    tile = (nt - 1 - t) if reverse else t
    d0 = SSM_HEADS if reverse else 0

    @pl.when(t == 0)
    def _():
        h_s[...] = jnp.zeros_like(h_s)

    xext_s[0:HALO, :] = jnp.where(tile > 0, xp_ref[...].astype(F32), 0.0)
    xext_s[HALO:HALO + ts, :] = xm_ref[...].astype(F32)
    xext_s[HALO + ts:, :] = jnp.where(tile < nt - 1, xn_ref[...].astype(F32), 0.0)

    li = lax.broadcasted_iota(jnp.int32, (CHUNK, CHUNK), 0)
    si = lax.broadcasted_iota(jnp.int32, (CHUNK, CHUNK), 1)
    causal = (li <= si) if reverse else (li >= si)
    tri = causal.astype(F32)
    triT = ((si <= li) if reverse else (si >= li)).astype(F32)
    last = 0 if reverse else CHUNK - 1
    hi = lax.Precision.HIGHEST

    nchunk = ts // CHUNK
    order = range(nchunk - 1, -1, -1) if reverse else range(nchunk)
    for c in order:
        r0 = c * CHUNK
        acc = cb_ref[...] + xext_s[r0 + HALO - 2:r0 + HALO - 2 + CHUNK, :] * cw_ref[0:1, :]
        for k in range(1, SSM_CONV):
            off = r0 + HALO - SSM_CONV // 2 + k
            acc = acc + xext_s[off:off + CHUNK, :] * cw_ref[k:k + 1, :]
        xc = _silu(acc)
        xs = xc[:, :SSM_INNER]
        bm = xc[:, SSM_INNER:SSM_INNER + SSM_GROUPS * D_STATE]
        cm = xc[:, SSM_INNER + SSM_GROUPS * D_STATE:]

        dt = _softplus(dt_ref[r0:r0 + CHUNK, :] + bias_ref[...])
        dtT = _softplus(dtT_ref[:, r0:r0 + CHUNK] + biasT_ref[...])
        acs = jnp.dot(tri, dt * a_ref[...], precision=hi, preferred_element_type=F32)
        acsT = jnp.dot(dtT * aT_ref[...], triT, precision=hi, preferred_element_type=F32)
        dt_e = jnp.dot(dt, e_ref[...], precision=hi, preferred_element_type=F32)
        acs_e = jnp.dot(acs, e_ref[...], precision=hi, preferred_element_type=F32)
        acs_last = acs_e[last:last + 1, :]
        x_dt = xs * dt_e
        x_b = x_dt.astype(BF16)
        x_decay = (x_dt * jnp.exp(acs_last - acs_e)).astype(BF16)
        chunk_decay = jnp.exp(acs_last)
        exp_acs = jnp.exp(acs_e)

        ys = []
        for g in range(SSM_GROUPS):
            bg = bm[:, g * D_STATE:(g + 1) * D_STATE]
            cg = cm[:, g * D_STATE:(g + 1) * D_STATE].astype(BF16)
            cb = lax.dot_general(cg, bg.astype(BF16), (((1,), (1,)), ((), ())),
                                 preferred_element_type=F32)
            gs = slice(g * GROUP_W, (g + 1) * GROUP_W)
            st = jnp.dot(bg.T.astype(BF16), x_decay[:, gs], preferred_element_type=F32)
            h_prev = h_s[g]
            y_off = jnp.dot(cg, h_prev.astype(BF16), preferred_element_type=F32)
            h_s[g] = h_prev * chunk_decay[:, gs] + st
            y_diag = []
            for rr in range(SSM_HPG):
                r = g * SSM_HPG + rr
                seg = acs[:, d0 + r:d0 + r + 1] - acsT[d0 + r:d0 + r + 1, :]
                lmat = jnp.exp(jnp.where(causal, seg, -jnp.inf))
                mm = (cb * lmat).astype(BF16)
                y_diag.append(jnp.dot(mm, x_b[:, r * SSM_HEAD_DIM:(r + 1) * SSM_HEAD_DIM],
                                      preferred_element_type=F32))
            ys.append(jnp.concatenate(y_diag, axis=1) + y_off * exp_acs[:, gs])
        y = jnp.concatenate(ys, axis=1)

        if not reverse:
            o_ref[r0:r0 + CHUNK, :] = y + xs * dskip_ref[...]
        else:
            y = (y + yf_ref[r0:r0 + CHUNK, :]) * _silu(z_ref[r0:r0 + CHUNK, :].astype(F32))
            parts = []
            for g in range(SSM_GROUPS):
                yg = y[:, g * GROUP_W:(g + 1) * GROUP_W]
                parts.append(yg * lax.rsqrt(jnp.mean(yg * yg, axis=-1, keepdims=True) + EPS))
            o_ref[r0:r0 + CHUNK, :] = (jnp.concatenate(parts, axis=1) * ng_ref[...]).astype(BF16)


def _ssd_pass(xbc, dt, dtT, conv_w, conv_b, bias, biasT, a, aT, e, extra, B, S, ts, reverse):
    T = xbc.shape[0]
    nt = S // ts
    hb = ts // HALO
    nh = S // HALO
    seq = (lambda t: nt - 1 - t) if reverse else (lambda t: t)
    main = lambda b, t: (b * nt + seq(t), 0)
    prev = lambda b, t: (b * nh + jnp.maximum(seq(t) * hb - 1, 0), 0)
    nxt = lambda b, t: (b * nh + jnp.minimum((seq(t) + 1) * hb, nh - 1), 0)
    const = lambda b, t: (0, 0)
    in_specs = [
        pl.BlockSpec((HALO, XBC_COLS), prev),
        pl.BlockSpec((ts, XBC_COLS), main),
        pl.BlockSpec((HALO, XBC_COLS), nxt),
        pl.BlockSpec((ts, DT_COLS), main),
        pl.BlockSpec((DT_COLS, ts), lambda b, t: (0, b * nt + seq(t))),
        pl.BlockSpec((SSM_CONV, XBC_COLS), const),
        pl.BlockSpec((1, XBC_COLS), const),
        pl.BlockSpec((1, DT_COLS), const),
        pl.BlockSpec((DT_COLS, 1), const),
        pl.BlockSpec((1, DT_COLS), const),
        pl.BlockSpec((DT_COLS, 1), const),
        pl.BlockSpec((DT_COLS, SSM_INNER), const),
    ]
    args = [xbc, xbc, xbc, dt, dtT, conv_w, conv_b, bias, biasT, a, aT, e]
    if reverse:
        z, yf, ng = extra
        in_specs += [pl.BlockSpec((ts, SSM_INNER), main), pl.BlockSpec((ts, SSM_INNER), main),
                     pl.BlockSpec((1, SSM_INNER), const)]
        args += [z, yf, ng]
        out_dtype = BF16
    else:
        (dskip,) = extra
        in_specs += [pl.BlockSpec((1, SSM_INNER), const)]
        args += [dskip]
        out_dtype = F32
    return pl.pallas_call(
        functools.partial(_ssd_kernel, reverse=reverse, ts=ts),
        grid=(B, nt),
        in_specs=in_specs,
        out_specs=pl.BlockSpec((ts, SSM_INNER), main),
        out_shape=jax.ShapeDtypeStruct((T, SSM_INNER), out_dtype),
        scratch_shapes=[pltpu.VMEM((ts + 2 * HALO, XBC_COLS), F32),
                        pltpu.VMEM((SSM_GROUPS, D_STATE, GROUP_W), F32)],
        compiler_params=pltpu.CompilerParams(
            dimension_semantics=("arbitrary", "arbitrary"), vmem_limit_bytes=VMEM_LIMIT),
        name="ssd_bwd" if reverse else "ssd_fwd",
    )(*args)


def _outproj_kernel(x_ref, a_ref, s_ref, wa_ref, ws_ref, g_ref, x1_ref, h_ref):
    x1 = (x_ref[...]
          + jnp.dot(a_ref[...], wa_ref[...], preferred_element_type=F32)
          + jnp.dot(s_ref[...], ws_ref[...], preferred_element_type=F32))
    x1_ref[...] = x1
    h_ref[...] = _rms(x1, g_ref[...]).astype(BF16)


def _out_proj(x2d, attn, ssd, w_attn, w_ssd, g2, tm):
    T = x2d.shape[0]
    row = lambda i: (i, 0)
    const = lambda i: (0, 0)
    return pl.pallas_call(
        _outproj_kernel,
        grid=(T // tm,),
        in_specs=[
            pl.BlockSpec((tm, D_MODEL), row),
            pl.BlockSpec((tm, ATTN_WIDTH), row),
            pl.BlockSpec((tm, SSM_INNER), row),
            pl.BlockSpec((ATTN_WIDTH, D_MODEL), const),
            pl.BlockSpec((SSM_INNER, D_MODEL), const),
            pl.BlockSpec((1, D_MODEL), const),
        ],
        out_specs=[pl.BlockSpec((tm, D_MODEL), row), pl.BlockSpec((tm, D_MODEL), row)],
        out_shape=[jax.ShapeDtypeStruct((T, D_MODEL), F32), jax.ShapeDtypeStruct((T, D_MODEL), BF16)],
        compiler_params=pltpu.CompilerParams(
            dimension_semantics=("arbitrary",), vmem_limit_bytes=VMEM_LIMIT),
        name="out_proj",
    )(x2d, attn, ssd, w_attn, w_ssd, g2)


FFN_COLS = 256


def _ffn_kernel(hp_ref, hm_ref, hn_ref, x1_ref, wup_ref, cw_ref, cb_ref, wdn_ref, fg_ref, o_ref,
                hext_s, u_s, glu_s, *, tm, n_seq_tiles):
    tile = pl.program_id(0) % n_seq_tiles
    hext_s[0:HALO, :] = jnp.where(tile > 0, hp_ref[...], jnp.zeros_like(hp_ref))
    hext_s[HALO:HALO + tm, :] = hm_ref[...]
    hext_s[HALO + tm:, :] = jnp.where(tile < n_seq_tiles - 1, hn_ref[...], jnp.zeros_like(hn_ref))
    f = FFN_COLS
    nchunks = D_FF // f

    def up(c):
        u_s[c % 2] = jnp.dot(hext_s[...], wup_ref[:, 2 * c * f:2 * (c + 1) * f], preferred_element_type=F32)

    up(0)
    for c in range(nchunks):
        if c + 1 < nchunks:
            up(c + 1)
        cols = slice(2 * c * f, 2 * (c + 1) * f)
        uc = cb_ref[:, cols]
        for k in range(FFN_CONV):
            off = HALO - FFN_CONV // 2 + k
            uc = uc + u_s[c % 2, off:off + tm, :] * cw_ref[k:k + 1, cols]
        glu_s[:, c * f:(c + 1) * f] = (_silu(uc[:, :f]) * uc[:, f:]).astype(BF16)
    x2 = x1_ref[...] + jnp.dot(glu_s[...], wdn_ref[...], preferred_element_type=F32)
    o_ref[...] = _rms(x2, fg_ref[...])


def _ffn(h2, x1, w_up, conv_w, conv_b, w_down, fg, S, tm):
    T = h2.shape[0]
    n_seq_tiles = S // tm
    hb = tm // HALO
    nh = T // HALO
    row = lambda i: (i, 0)
    const = lambda i: (0, 0)
    prev = lambda i: (jnp.maximum(i * hb - 1, 0), 0)
    nxt = lambda i: (jnp.minimum((i + 1) * hb, nh - 1), 0)
    return pl.pallas_call(
        functools.partial(_ffn_kernel, tm=tm, n_seq_tiles=n_seq_tiles),
        grid=(T // tm,),
        in_specs=[
            pl.BlockSpec((HALO, D_MODEL), prev),
            pl.BlockSpec((tm, D_MODEL), row),
            pl.BlockSpec((HALO, D_MODEL), nxt),
            pl.BlockSpec((tm, D_MODEL), row),
            pl.BlockSpec((D_MODEL, 2 * D_FF), const),
            pl.BlockSpec((FFN_CONV, 2 * D_FF), const),
            pl.BlockSpec((1, 2 * D_FF), const),
            pl.BlockSpec((D_FF, D_MODEL), const),
            pl.BlockSpec((1, D_MODEL), const),
        ],
        out_specs=pl.BlockSpec((tm, D_MODEL), row),
        out_shape=jax.ShapeDtypeStruct((T, D_MODEL), F32),
        scratch_shapes=[pltpu.VMEM((tm + 2 * HALO, D_MODEL), BF16),
                        pltpu.VMEM((2, tm + 2 * HALO, 2 * FFN_COLS), F32),
                        pltpu.VMEM((tm, D_FF), BF16)],
        compiler_params=pltpu.CompilerParams(
            dimension_semantics=("arbitrary",), vmem_limit_bytes=VMEM_LIMIT),
        name="ffn",
    )(h2, h2, h2, x1, w_up, conv_w, conv_b, w_down, fg)


def _interleave_gate_value(a):
    lead = a.shape[:-1]
    a = a.reshape(*lead, 2, D_FF // FFN_COLS, FFN_COLS)
    return jnp.swapaxes(a, -3, -2).reshape(*lead, 2 * D_FF)


def _layer(x2d, B, S, norm1_g, w_in, attn_sink, attn_out_g, ssd_conv_w, ssd_conv_b,
           dt_bias_f, dt_bias_b, a_log_f, a_log_b, ssd_d, ssd_norm_g, w_out, norm2_g,
           w_up, ffn_conv_w, ffn_conv_b, w_down, out_g, tables, tm, ts, tq):
    row = lambda v: v.reshape(1, -1).astype(F32)
    w_main = w_in[:, :MAIN_COLS].astype(BF16)
    w_dt = jnp.pad(w_in[:, MAIN_COLS:], ((0, 0), (0, LANE - DT_COLS))).astype(BF16)
    q, k, vT, z, xbc, dt, dtT = _in_proj(x2d, row(norm1_g), w_main, w_dt, tables, S, tm)

    sink_rows = jnp.repeat(attn_sink.astype(F32), BLOCK).reshape(KV_HEADS, GQA_GROUP * BLOCK)
    attn = _attention(q, k, vT, sink_rows, row(attn_out_g), B, S, tq)

    bias = jnp.concatenate([dt_bias_f, dt_bias_b]).astype(F32)
    a = -jnp.exp(jnp.concatenate([a_log_f, a_log_b]).astype(F32))
    head_of_col = jnp.arange(SSM_INNER) // SSM_HEAD_DIM
    cw, cb = ssd_conv_w.astype(F32), row(ssd_conv_b)
    ys = None
    for reverse in (False, True):
        d0 = SSM_HEADS if reverse else 0
        e = (jnp.arange(DT_COLS)[:, None] == (head_of_col + d0)[None, :]).astype(F32)
        extra = (z, ys, row(ssd_norm_g)) if reverse else (row(jnp.repeat(ssd_d.astype(F32), SSM_HEAD_DIM)),)
        ys = _ssd_pass(xbc, dt, dtT, cw, cb, bias.reshape(1, -1), bias.reshape(-1, 1),
                       a.reshape(1, -1), a.reshape(-1, 1), e, extra, B, S, ts, reverse)

    w_o = w_out.astype(BF16)
    x1, h2 = _out_proj(x2d, attn, ys, w_o[:ATTN_WIDTH], w_o[ATTN_WIDTH:], row(norm2_g), tm)
    perm = _interleave_gate_value
    return _ffn(h2, x1, perm(w_up.astype(BF16)), perm(ffn_conv_w.astype(F32)), perm(row(ffn_conv_b)),
                w_down.astype(BF16), out_g, S, tm)


def kernel(x, norm1_g, w_in, attn_sink, attn_out_g, ssd_conv_w, ssd_conv_b, ssd_dt_bias_fwd,
           ssd_dt_bias_bwd, ssd_a_log_fwd, ssd_a_log_bwd, ssd_d, ssd_norm_g, w_out, norm2_g, w_up,
           ffn_conv_w, ffn_conv_b, w_down, final_norm_g, *, tm=512, ts=512, tq=512):
    B, S, _ = x.shape
    depth = w_in.shape[0]
    assert depth == 1, "the fused final RMSNorm assumes a single layer"
    tables = _rope_tables(S)
    x2d = x.reshape(B * S, D_MODEL)
    out = _layer(x2d, B, S, norm1_g[0], w_in[0], attn_sink[0], attn_out_g[0], ssd_conv_w[0], ssd_conv_b[0],
                 ssd_dt_bias_fwd[0], ssd_dt_bias_bwd[0], ssd_a_log_fwd[0], ssd_a_log_bwd[0], ssd_d[0],
                 ssd_norm_g[0], w_out[0], norm2_g[0], w_up[0], ffn_conv_w[0], ffn_conv_b[0], w_down[0],
                 final_norm_g.reshape(1, -1).astype(F32), tables, tm, ts, tq)
    return out.reshape(B, S, D_MODEL)
```

```python
import functools
import math

import jax
import jax.numpy as jnp
from jax import lax
from jax.experimental import pallas as pl
from jax.experimental.pallas import tpu as pltpu

D_MODEL = 1024
HEAD_DIM = 64
ATTN_WIDTH = 512
ATTN_HEADS = 8
KV_HEADS = 2
GQA_GROUP = 4
ROT_DIM = 16
ROPE_THETA = 500000.0
WINDOW = 128
BLOCK = 128
SSM_INNER = 512
SSM_HEAD_DIM = 64
SSM_HEADS = 8
SSM_GROUPS = 2
SSM_HPG = 4
D_STATE = 128
SSM_CONV = 5
CHUNK = 128
D_FF = 2816
FFN_CONV = 3
EPS = 1e-5
NEG = -1e30

KV_COLS = KV_HEADS * HEAD_DIM
XBC_COLS = SSM_INNER + 2 * SSM_GROUPS * D_STATE
QKVZ_COLS = ATTN_WIDTH + 2 * KV_COLS + SSM_INNER
DT_COLS = 2 * SSM_HEADS
SPLIT_TERMS = 3
GROUP_W = SSM_HPG * SSM_HEAD_DIM

LANE = 128
HALO = 16
CONV_COLS = 256
VMEM_LIMIT = 56 * 1024 * 1024

F32 = jnp.float32
BF16 = jnp.bfloat16


def _rms(x, g):
    return x * lax.rsqrt(jnp.mean(x * x, axis=-1, keepdims=True) + EPS) * g


def _silu(x):
    return x * (1.0 / (1.0 + jnp.exp(-x)))


def _softplus(x):
    return jnp.maximum(x, 0.0) + jnp.log1p(jnp.exp(-jnp.abs(x)))


def _inproj_kernel(xp_ref, x_ref, xn_ref, g_ref, w_ref, wx_ref, wdt_ref, cw_ref, cb_ref,
                   cos_ref, sa_ref, sb_ref,
                   q_ref, k_ref, vT_ref, z_ref, xc_ref, dtT_ref, hext_s, xbc_s,
                   *, tm, n_seq_tiles):
    tile = pl.program_id(0) % n_seq_tiles
    norm = lambda r: _rms(r[...], g_ref[...]).astype(BF16)
    hb = norm(x_ref)
    hext_s[0:HALO, :] = jnp.where(tile > 0, norm(xp_ref), jnp.zeros((HALO, D_MODEL), BF16))
    hext_s[HALO:HALO + tm, :] = hb
    hext_s[HALO + tm:, :] = jnp.where(tile < n_seq_tiles - 1, norm(xn_ref), jnp.zeros((HALO, D_MODEL), BF16))

    f = CONV_COLS
    nchunks = XBC_COLS // f

    def xbc_proj(c):
        xbc_s[c % 2] = jnp.dot(hext_s[...], wx_ref[:, c * f:(c + 1) * f], preferred_element_type=F32)

    def conv(c):
        cols = slice(c * f, (c + 1) * f)
        acc = cb_ref[:, cols]
        for kk in range(SSM_CONV):
            off = HALO - SSM_CONV // 2 + kk
            acc = acc + xbc_s[c % 2, off:off + tm, :] * cw_ref[kk:kk + 1, cols]
        xc_ref[:, cols] = _silu(acc).astype(BF16)

    xbc_proj(0)
    for c in range(nchunks - 1):
        xbc_proj(c + 1)
        conv(c)
    proj = jnp.dot(hb, w_ref[...], preferred_element_type=F32)
    dt = jnp.dot(hb, wdt_ref[...], preferred_element_type=F32)
    conv(nchunks - 1)
    c, sa, sb = cos_ref[...], sa_ref[...], sb_ref[...]

    def rope(t):
        n = t.shape[1]
        reps = n // LANE
        cc = jnp.concatenate([c] * reps, axis=1) if reps > 1 else c
        aa = jnp.concatenate([sa] * reps, axis=1) if reps > 1 else sa
        bb = jnp.concatenate([sb] * reps, axis=1) if reps > 1 else sb
        half = ROT_DIM // 2
        return t * cc + pltpu.roll(t, n - half, 1) * aa + pltpu.roll(t, half, 1) * bb

    o = 0
    q = rope(proj[:, o:o + ATTN_WIDTH]) * (1.0 / math.sqrt(HEAD_DIM))
    o += ATTN_WIDTH
    k = rope(proj[:, o:o + KV_COLS])
    o += KV_COLS
    v = proj[:, o:o + KV_COLS]
    o += KV_COLS
    q_ref[...] = q.astype(BF16)
    k_ref[...] = k.astype(BF16)
    vT_ref[...] = v.T.astype(BF16)
    z_ref[...] = proj[:, o:o + SSM_INNER].astype(BF16)
    dtT_ref[...] = dt.T[:DT_COLS, :]


def _rope_tables(S):
    half = ROT_DIM // 2
    pos = jnp.arange(S, dtype=F32)
    inv = ROPE_THETA ** (-jnp.arange(0, ROT_DIM, 2, dtype=F32) / ROT_DIM)
    ang = pos[:, None] * inv[None, :]
    cos, sin = jnp.cos(ang), jnp.sin(ang)
    rest = HEAD_DIM - ROT_DIM
    ones, zeros, zh = jnp.ones((S, rest), F32), jnp.zeros((S, rest), F32), jnp.zeros((S, half), F32)
    c = jnp.concatenate([cos, cos, ones], axis=1)
    sa = jnp.concatenate([-sin, zh, zeros], axis=1)
    sb = jnp.concatenate([zh, sin, zeros], axis=1)
    rep = LANE // HEAD_DIM
    return tuple(jnp.tile(t, (1, rep)) for t in (c, sa, sb))


def _in_proj(x2d, g1, w_qkvz, w_xbc, w_dt, conv_w, conv_b, tables, S, tm):
    T = x2d.shape[0]
    n_seq_tiles = S // tm
    hb = tm // HALO
    nh = T // HALO
    row = lambda i: (i, 0)
    col = lambda i: (0, i)
    const = lambda i: (0, 0)
    prev = lambda i: (jnp.maximum(i * hb - 1, 0), 0)
    nxt = lambda i: (jnp.minimum((i + 1) * hb, nh - 1), 0)
    tab = lambda i: (i % n_seq_tiles, 0)
    return pl.pallas_call(
        functools.partial(_inproj_kernel, tm=tm, n_seq_tiles=n_seq_tiles),
        grid=(T // tm,),
        in_specs=[
            pl.BlockSpec((HALO, D_MODEL), prev),
            pl.BlockSpec((tm, D_MODEL), row),
            pl.BlockSpec((HALO, D_MODEL), nxt),
            pl.BlockSpec((1, D_MODEL), const),
            pl.BlockSpec((D_MODEL, QKVZ_COLS), const),
            pl.BlockSpec((D_MODEL, XBC_COLS), const),
            pl.BlockSpec((D_MODEL, LANE), const),
            pl.BlockSpec((SSM_CONV, XBC_COLS), const),
            pl.BlockSpec((1, XBC_COLS), const),
            pl.BlockSpec((tm, LANE), tab),
            pl.BlockSpec((tm, LANE), tab),
            pl.BlockSpec((tm, LANE), tab),
        ],
        out_specs=[
            pl.BlockSpec((tm, ATTN_WIDTH), row),
            pl.BlockSpec((tm, KV_COLS), row),
            pl.BlockSpec((KV_COLS, tm), col),
            pl.BlockSpec((tm, SSM_INNER), row),
            pl.BlockSpec((tm, XBC_COLS), row),
            pl.BlockSpec((DT_COLS, tm), col),
        ],
        out_shape=[
            jax.ShapeDtypeStruct((T, ATTN_WIDTH), BF16),
            jax.ShapeDtypeStruct((T, KV_COLS), BF16),
            jax.ShapeDtypeStruct((KV_COLS, T), BF16),
            jax.ShapeDtypeStruct((T, SSM_INNER), BF16),
            jax.ShapeDtypeStruct((T, XBC_COLS), BF16),
            jax.ShapeDtypeStruct((DT_COLS, T), F32),
        ],
        scratch_shapes=[pltpu.VMEM((tm + 2 * HALO, D_MODEL), BF16),
                        pltpu.VMEM((2, tm + 2 * HALO, CONV_COLS), F32)],
        compiler_params=pltpu.CompilerParams(
            dimension_semantics=("arbitrary",), vmem_limit_bytes=VMEM_LIMIT),
        name="in_proj",
    )(x2d, x2d, x2d, g1, w_qkvz, w_xbc, w_dt, conv_w, conv_b, *tables)


def _attn_kernel(sink_ref, q_ref, kp_ref, km_ref, kn_ref, vp_ref, vm_ref, vn_ref, g_ref, o_ref,
                 k_s, vT_s, *, tq):
    i = pl.program_id(1)
    n = pl.num_programs(1)
    k_s[0:BLOCK, :] = kp_ref[...]
    k_s[BLOCK:BLOCK + tq, :] = km_ref[...]
    k_s[BLOCK + tq:, :] = kn_ref[...]
    vT_s[:, 0:BLOCK] = vp_ref[...]
    vT_s[:, BLOCK:BLOCK + tq] = vm_ref[...]
    vT_s[:, BLOCK + tq:] = vn_ref[...]

    band = 3 * BLOCK
    key = lax.broadcasted_iota(jnp.int32, (band, GQA_GROUP * BLOCK), 0)
    qry = lax.broadcasted_iota(jnp.int32, (band, GQA_GROUP * BLOCK), 1) % BLOCK
    in_window = jnp.abs(key - BLOCK - qry) <= WINDOW
    first_key = jnp.where(i > 0, 0, BLOCK)
    last_key = jnp.where(i < n - 1, band, 2 * BLOCK)

    nsub = tq // BLOCK
    for j in range(nsub):
        valid = in_window
        if j == 0:
            valid = valid & (key >= first_key)
        if j == nsub - 1:
            valid = valid & (key < last_key)
        qj = q_ref[j * BLOCK:(j + 1) * BLOCK, :]
        kj = k_s[j * BLOCK:j * BLOCK + band, :]
        vTj = vT_s[:, j * BLOCK:j * BLOCK + band]
        outs = []
        for kh in range(KV_HEADS):
            heads = [kh * GQA_GROUP + g for g in range(GQA_GROUP)]
            kk = kj[:, kh * HEAD_DIM:(kh + 1) * HEAD_DIM]
            qs = jnp.concatenate([qj[:, h * HEAD_DIM:(h + 1) * HEAD_DIM] for h in heads], axis=0)
            sT = lax.dot_general(kk, qs, (((1,), (1,)), ((), ())), preferred_element_type=F32)
            sT = jnp.where(valid, sT, NEG)
            sink = sink_ref[kh:kh + 1, :]
            m = jnp.maximum(jnp.max(sT, axis=0, keepdims=True), sink)
            p = jnp.exp(sT - m)
            denom = jnp.sum(p, axis=0, keepdims=True) + jnp.exp(sink - m)
            oT = jnp.dot(vTj[kh * HEAD_DIM:(kh + 1) * HEAD_DIM, :], p.astype(BF16),
                         preferred_element_type=F32) / denom
            outs += [oT[:, g * BLOCK:(g + 1) * BLOCK] for g in range(GQA_GROUP)]
        oT_all = jnp.concatenate(outs, axis=0)
        inv = lax.rsqrt(jnp.mean(oT_all * oT_all, axis=0, keepdims=True) + EPS)
        o_ref[j * BLOCK:(j + 1) * BLOCK, :] = ((oT_all * inv).T * g_ref[...]).astype(BF16)


def _attention(q, k, vT, sink_rows, g, B, S, tq):
    T = q.shape[0]
    nq = S // tq
    sub = tq // BLOCK
    nblk = S // BLOCK
    main = lambda b, i: (b * nq + i, 0)
    prev = lambda b, i: (b * nblk + jnp.maximum(i * sub - 1, 0), 0)
    nxt = lambda b, i: (b * nblk + jnp.minimum((i + 1) * sub, nblk - 1), 0)
    swap = lambda f: (lambda b, i: f(b, i)[::-1])
    const = lambda b, i: (0, 0)
    return pl.pallas_call(
        functools.partial(_attn_kernel, tq=tq),
        grid=(B, nq),
        in_specs=[
            pl.BlockSpec((KV_HEADS, GQA_GROUP * BLOCK), const),
            pl.BlockSpec((tq, ATTN_WIDTH), main),
            pl.BlockSpec((BLOCK, KV_COLS), prev),
            pl.BlockSpec((tq, KV_COLS), main),
            pl.BlockSpec((BLOCK, KV_COLS), nxt),
            pl.BlockSpec((KV_COLS, BLOCK), swap(prev)),
            pl.BlockSpec((KV_COLS, tq), swap(main)),
            pl.BlockSpec((KV_COLS, BLOCK), swap(nxt)),
            pl.BlockSpec((1, ATTN_WIDTH), const),
        ],
        out_specs=pl.BlockSpec((tq, ATTN_WIDTH), main),
        out_shape=jax.ShapeDtypeStruct((T, ATTN_WIDTH), BF16),
        scratch_shapes=[pltpu.VMEM((tq + 2 * BLOCK, KV_COLS), BF16),
                        pltpu.VMEM((KV_COLS, tq + 2 * BLOCK), BF16)],
        compiler_params=pltpu.CompilerParams(
            dimension_semantics=("arbitrary", "arbitrary"), vmem_limit_bytes=VMEM_LIMIT),
        name="attention",
    )(sink_rows, q, k, k, k, vT, vT, vT, g)


def _split_terms(v):
    hi = v.astype(BF16).astype(F32)
    r1 = v - hi
    mid = r1.astype(BF16).astype(F32)
    lo = (r1 - mid).astype(BF16).astype(F32)
    return jnp.concatenate([hi, mid, lo], axis=0)


def _ssd_kernel(*refs, reverse, ts):
    if reverse:
        (xc_ref, dtT_ref, biasT_ref, aT_ref, e_ref, z_ref, yf_ref, ng_ref, o_ref, h_s) = refs
    else:
        (xc_ref, dtT_ref, biasT_ref, aT_ref, e_ref, dskip_ref, o_ref, h_s) = refs
    t = pl.program_id(1)
    d0 = SSM_HEADS if reverse else 0

    @pl.when(t == 0)
    def _():
        h_s[...] = jnp.zeros_like(h_s)

    li = lax.broadcasted_iota(jnp.int32, (CHUNK, CHUNK), 0)
    si = lax.broadcasted_iota(jnp.int32, (CHUNK, CHUNK), 1)
    causal = (li <= si) if reverse else (li >= si)
    scan_op = jnp.where((li >= si) if reverse else (li <= si), 1.0, 0.0).astype(BF16)
    last = 0 if reverse else CHUNK - 1
    nchunk = ts // CHUNK
    nterm = SPLIT_TERMS * DT_COLS

    def chunks_of(vT):
        return [vT[:, c * CHUNK:(c + 1) * CHUNK] for c in range(nchunk)]

    def sum_terms(m):
        return m[0:DT_COLS] + m[DT_COLS:2 * DT_COLS] + m[2 * DT_COLS:nterm]

    dtT_all = _softplus(dtT_ref[...] + biasT_ref[...])
    a_terms = jnp.concatenate([_split_terms(v) for v in chunks_of(dtT_all * aT_ref[...])], axis=0)
    cs = jnp.dot(a_terms.astype(BF16), scan_op, preferred_element_type=F32)
    dtT_c = chunks_of(dtT_all)
    acsT_c = [sum_terms(cs[c * nterm:(c + 1) * nterm]) for c in range(nchunk)]
    acs_c = [v.T for v in acsT_c]
    decayT_c = [dtT_c[c] * jnp.exp(acsT_c[c][:, last:last + 1] - acsT_c[c]) for c in range(nchunk)]

    def expand(vT_c):
        terms = jnp.concatenate([_split_terms(v).T for v in vT_c], axis=0)
        return jnp.dot(terms.astype(BF16), e_ref[...], preferred_element_type=F32)

    exp_acs_all = expand([jnp.exp(v) for v in acsT_c])
    decay_all = expand(decayT_c)

    order = range(nchunk - 1, -1, -1) if reverse else range(nchunk)
    for c in order:
        r0 = c * CHUNK
        xc = xc_ref[r0:r0 + CHUNK, :]
        x_b = xc[:, :SSM_INNER]
        xs = x_b.astype(F32)
        bm = xc[:, SSM_INNER:SSM_INNER + SSM_GROUPS * D_STATE]
        cm = xc[:, SSM_INNER + SSM_GROUPS * D_STATE:]

        dtT, acsT, acs = dtT_c[c], acsT_c[c], acs_c[c]
        exp_acs = exp_acs_all[r0:r0 + CHUNK, :]
        x_decay = (xs * decay_all[r0:r0 + CHUNK, :]).astype(BF16)
        chunk_decay = exp_acs[last:last + 1, :]

        ys = []
        for g in range(SSM_GROUPS):
            bg = bm[:, g * D_STATE:(g + 1) * D_STATE]
            cg = cm[:, g * D_STATE:(g + 1) * D_STATE]
            cb = lax.dot_general(cg, bg, (((1,), (1,)), ((), ())), preferred_element_type=F32)
            gs = slice(g * GROUP_W, (g + 1) * GROUP_W)
            bgT = bg.astype(F32).T.astype(BF16)
            st = jnp.dot(bgT, x_decay[:, gs], preferred_element_type=F32)
            h_prev = h_s[g]
            y_off = jnp.dot(cg, h_prev.astype(BF16), preferred_element_type=F32)
            h_s[g] = h_prev * chunk_decay[:, gs] + st
            y_diag = []
            for rr in range(SSM_HPG):
                r = g * SSM_HPG + rr
                seg = acs[:, d0 + r:d0 + r + 1] - acsT[d0 + r:d0 + r + 1, :]
                lmat = jnp.exp(jnp.where(causal, seg, -jnp.inf)) * dtT[d0 + r:d0 + r + 1, :]
                mm = (cb * lmat).astype(BF16)
                y_diag.append(jnp.dot(mm, x_b[:, r * SSM_HEAD_DIM:(r + 1) * SSM_HEAD_DIM],
                                      preferred_element_type=F32))
            ys.append(jnp.concatenate(y_diag, axis=1) + y_off * exp_acs[:, gs])
        y = jnp.concatenate(ys, axis=1)

        if not reverse:
            o_ref[r0:r0 + CHUNK, :] = y + xs * dskip_ref[...]
        else:
            y = (y + yf_ref[r0:r0 + CHUNK, :]) * _silu(z_ref[r0:r0 + CHUNK, :].astype(F32))
            parts = []
            for g in range(SSM_GROUPS):
                yg = y[:, g * GROUP_W:(g + 1) * GROUP_W]
                parts.append(yg * lax.rsqrt(jnp.mean(yg * yg, axis=-1, keepdims=True) + EPS))
            o_ref[r0:r0 + CHUNK, :] = (jnp.concatenate(parts, axis=1) * ng_ref[...]).astype(BF16)


def _ssd_pass(xc, dtT, biasT, aT, e, extra, B, S, ts, reverse):
    T = xc.shape[0]
    nt = S // ts
    seq = (lambda t: nt - 1 - t) if reverse else (lambda t: t)
    main = lambda b, t: (b * nt + seq(t), 0)
    const = lambda b, t: (0, 0)
    in_specs = [
        pl.BlockSpec((ts, XBC_COLS), main),
        pl.BlockSpec((DT_COLS, ts), lambda b, t: (0, b * nt + seq(t))),
        pl.BlockSpec((DT_COLS, 1), const),
        pl.BlockSpec((DT_COLS, 1), const),
        pl.BlockSpec((SPLIT_TERMS * DT_COLS, SSM_INNER), const),
    ]
    args = [xc, dtT, biasT, aT, e]
    if reverse:
        z, yf, ng = extra
        in_specs += [pl.BlockSpec((ts, SSM_INNER), main), pl.BlockSpec((ts, SSM_INNER), main),
                     pl.BlockSpec((1, SSM_INNER), const)]
        args += [z, yf, ng]
        out_dtype = BF16
    else:
        (dskip,) = extra
        in_specs += [pl.BlockSpec((1, SSM_INNER), const)]
        args += [dskip]
        out_dtype = F32
    return pl.pallas_call(
        functools.partial(_ssd_kernel, reverse=reverse, ts=ts),
        grid=(B, nt),
        in_specs=in_specs,
        out_specs=pl.BlockSpec((ts, SSM_INNER), main),
        out_shape=jax.ShapeDtypeStruct((T, SSM_INNER), out_dtype),
        scratch_shapes=[pltpu.VMEM((SSM_GROUPS, D_STATE, GROUP_W), F32)],
        compiler_params=pltpu.CompilerParams(
            dimension_semantics=("arbitrary", "arbitrary"), vmem_limit_bytes=VMEM_LIMIT),
        name="ssd_bwd" if reverse else "ssd_fwd",
    )(*args)


def _outproj_kernel(x_ref, a_ref, s_ref, wa_ref, ws_ref, g_ref, x1_ref, h_ref):
    x1 = (x_ref[...]
          + jnp.dot(a_ref[...], wa_ref[...], preferred_element_type=F32)
          + jnp.dot(s_ref[...], ws_ref[...], preferred_element_type=F32))
    x1_ref[...] = x1
    h_ref[...] = _rms(x1, g_ref[...]).astype(BF16)


def _out_proj(x2d, attn, ssd, w_attn, w_ssd, g2, tm):
    T = x2d.shape[0]
    row = lambda i: (i, 0)
    const = lambda i: (0, 0)
    return pl.pallas_call(
        _outproj_kernel,
        grid=(T // tm,),
        in_specs=[
            pl.BlockSpec((tm, D_MODEL), row),
            pl.BlockSpec((tm, ATTN_WIDTH), row),
            pl.BlockSpec((tm, SSM_INNER), row),
            pl.BlockSpec((ATTN_WIDTH, D_MODEL), const),
            pl.BlockSpec((SSM_INNER, D_MODEL), const),
            pl.BlockSpec((1, D_MODEL), const),
        ],
        out_specs=[pl.BlockSpec((tm, D_MODEL), row), pl.BlockSpec((tm, D_MODEL), row)],
        out_shape=[jax.ShapeDtypeStruct((T, D_MODEL), F32), jax.ShapeDtypeStruct((T, D_MODEL), BF16)],
        compiler_params=pltpu.CompilerParams(
            dimension_semantics=("arbitrary",), vmem_limit_bytes=VMEM_LIMIT),
        name="out_proj",
    )(x2d, attn, ssd, w_attn, w_ssd, g2)


FFN_COLS = 256


def _ffn_kernel(hp_ref, hm_ref, hn_ref, x1_ref, wup_ref, cw_ref, cb_ref, wdn_ref, fg_ref, o_ref,
                hext_s, u_s, glu_s, *, tm, n_seq_tiles):
    tile = pl.program_id(0) % n_seq_tiles
    hext_s[0:HALO, :] = jnp.where(tile > 0, hp_ref[...], jnp.zeros_like(hp_ref))
    hext_s[HALO:HALO + tm, :] = hm_ref[...]
    hext_s[HALO + tm:, :] = jnp.where(tile < n_seq_tiles - 1, hn_ref[...], jnp.zeros_like(hn_ref))
    f = FFN_COLS
    nchunks = D_FF // f

    def up(c):
        hext = hext_s[...]
        for half, base in enumerate((0, D_FF)):
            u_s[c % 2, :, half * f:(half + 1) * f] = jnp.dot(
                hext, wup_ref[:, base + c * f:base + (c + 1) * f], preferred_element_type=F32)

    def conv(c, half, base):
        cols = slice(base + c * f, base + (c + 1) * f)
        out = cb_ref[:, cols]
        for k in range(FFN_CONV):
            off = HALO - FFN_CONV // 2 + k
            out = out + u_s[c % 2, off:off + tm, half * f:(half + 1) * f] * cw_ref[k:k + 1, cols]
        return out

    up(0)
    for c in range(nchunks):
        if c + 1 < nchunks:
            up(c + 1)
        glu_s[:, c * f:(c + 1) * f] = (_silu(conv(c, 0, 0)) * conv(c, 1, D_FF)).astype(BF16)
    x2 = x1_ref[...] + jnp.dot(glu_s[...], wdn_ref[...], preferred_element_type=F32)
    o_ref[...] = _rms(x2, fg_ref[...])


def _ffn(h2, x1, w_up, conv_w, conv_b, w_down, fg, S, tm):
    T = h2.shape[0]
    n_seq_tiles = S // tm
    hb = tm // HALO
    nh = T // HALO
    row = lambda i: (i, 0)
    const = lambda i: (0, 0)
    prev = lambda i: (jnp.maximum(i * hb - 1, 0), 0)
    nxt = lambda i: (jnp.minimum((i + 1) * hb, nh - 1), 0)
    return pl.pallas_call(
        functools.partial(_ffn_kernel, tm=tm, n_seq_tiles=n_seq_tiles),
        grid=(T // tm,),
        in_specs=[
            pl.BlockSpec((HALO, D_MODEL), prev),
            pl.BlockSpec((tm, D_MODEL), row),
            pl.BlockSpec((HALO, D_MODEL), nxt),
            pl.BlockSpec((tm, D_MODEL), row),
            pl.BlockSpec((D_MODEL, 2 * D_FF), const),
            pl.BlockSpec((FFN_CONV, 2 * D_FF), const),
            pl.BlockSpec((1, 2 * D_FF), const),
            pl.BlockSpec((D_FF, D_MODEL), const),
            pl.BlockSpec((1, D_MODEL), const),
        ],
        out_specs=pl.BlockSpec((tm, D_MODEL), row),
        out_shape=jax.ShapeDtypeStruct((T, D_MODEL), F32),
        scratch_shapes=[pltpu.VMEM((tm + 2 * HALO, D_MODEL), BF16),
                        pltpu.VMEM((2, tm + 2 * HALO, 2 * FFN_COLS), F32),
                        pltpu.VMEM((tm, D_FF), BF16)],
        compiler_params=pltpu.CompilerParams(
            dimension_semantics=("arbitrary",), vmem_limit_bytes=VMEM_LIMIT),
        name="ffn",
    )(h2, h2, h2, x1, w_up, conv_w, conv_b, w_down, fg)


def _layer(x2d, B, S, norm1_g, w_in, attn_sink, attn_out_g, ssd_conv_w, ssd_conv_b,
           dt_bias_f, dt_bias_b, a_log_f, a_log_b, ssd_d, ssd_norm_g, w_out, norm2_g,
           w_up, ffn_conv_w, ffn_conv_b, w_down, out_g, tables, tm, ts, tq):
    row = lambda v: v.reshape(1, -1).astype(F32)
    w_qkvz = w_in[:, :QKVZ_COLS].astype(BF16)
    w_xbc = w_in[:, QKVZ_COLS:QKVZ_COLS + XBC_COLS].astype(BF16)
    w_dt = jnp.pad(w_in[:, QKVZ_COLS + XBC_COLS:], ((0, 0), (0, LANE - DT_COLS))).astype(BF16)
    q, k, vT, z, xc, dtT = _in_proj(x2d, row(norm1_g), w_qkvz, w_xbc, w_dt,
                                    ssd_conv_w.astype(F32), row(ssd_conv_b), tables, S, tm)

    sink_rows = jnp.repeat(attn_sink.astype(F32), BLOCK).reshape(KV_HEADS, GQA_GROUP * BLOCK)
    attn = _attention(q, k, vT, sink_rows, row(attn_out_g), B, S, tq)

    bias = jnp.concatenate([dt_bias_f, dt_bias_b]).astype(F32)
    a = -jnp.exp(jnp.concatenate([a_log_f, a_log_b]).astype(F32))
    head_of_col = jnp.arange(SSM_INNER) // SSM_HEAD_DIM
    ys = None
    for reverse in (False, True):
        d0 = SSM_HEADS if reverse else 0
        term_row = jnp.arange(SPLIT_TERMS * DT_COLS) % DT_COLS
        e = (term_row[:, None] == (head_of_col + d0)[None, :]).astype(BF16)
        extra = (z, ys, row(ssd_norm_g)) if reverse else (row(jnp.repeat(ssd_d.astype(F32), SSM_HEAD_DIM)),)
        ys = _ssd_pass(xc, dtT, bias.reshape(-1, 1), a.reshape(-1, 1), e, extra, B, S, ts, reverse)

    w_o = w_out.astype(BF16)
    x1, h2 = _out_proj(x2d, attn, ys, w_o[:ATTN_WIDTH], w_o[ATTN_WIDTH:], row(norm2_g), tm)
    return _ffn(h2, x1, w_up.astype(BF16), ffn_conv_w.astype(F32), row(ffn_conv_b),
                w_down.astype(BF16), out_g, S, tm)


def kernel(x, norm1_g, w_in, attn_sink, attn_out_g, ssd_conv_w, ssd_conv_b, ssd_dt_bias_fwd,
           ssd_dt_bias_bwd, ssd_a_log_fwd, ssd_a_log_bwd, ssd_d, ssd_norm_g, w_out, norm2_g, w_up,
           ffn_conv_w, ffn_conv_b, w_down, final_norm_g, *, tm=512, ts=512, tq=512):
    B, S, _ = x.shape
    depth = w_in.shape[0]
    assert depth == 1, "the fused final RMSNorm assumes a single layer"
    tables = _rope_tables(S)
    x2d = x.reshape(B * S, D_MODEL)
    out = _layer(x2d, B, S, norm1_g[0], w_in[0], attn_sink[0], attn_out_g[0], ssd_conv_w[0], ssd_conv_b[0],
                 ssd_dt_bias_fwd[0], ssd_dt_bias_bwd[0], ssd_a_log_fwd[0], ssd_a_log_bwd[0], ssd_d[0],
                 ssd_norm_g[0], w_out[0], norm2_g[0], w_up[0], ffn_conv_w[0], ffn_conv_b[0], w_down[0],
                 final_norm_g.reshape(1, -1).astype(F32), tables, tm, ts, tq)
    return out.reshape(B, S, D_MODEL)
```

```python
import functools
import math

import jax
import jax.numpy as jnp
from jax import lax
from jax.experimental import pallas as pl
from jax.experimental.pallas import tpu as pltpu

D_MODEL = 1024
HEAD_DIM = 64
ATTN_WIDTH = 512
ATTN_HEADS = 8
KV_HEADS = 2
GQA_GROUP = 4
ROT_DIM = 16
ROPE_THETA = 500000.0
WINDOW = 128
BLOCK = 128
SSM_INNER = 512
SSM_HEAD_DIM = 64
SSM_HEADS = 8
SSM_GROUPS = 2
SSM_HPG = 4
D_STATE = 128
SSM_CONV = 5
CHUNK = 128
D_FF = 2816
FFN_CONV = 3
EPS = 1e-5
NEG = -1e30

KV_COLS = KV_HEADS * HEAD_DIM
XBC_COLS = SSM_INNER + 2 * SSM_GROUPS * D_STATE
QKVZ_COLS = ATTN_WIDTH + 2 * KV_COLS + SSM_INNER
DT_COLS = 2 * SSM_HEADS
SPLIT_TERMS = 3
GROUP_W = SSM_HPG * SSM_HEAD_DIM

LANE = 128
SUBLANE = 8
HALO = 16
CONV_COLS = 256
VMEM_LIMIT = 56 * 1024 * 1024

F32 = jnp.float32
BF16 = jnp.bfloat16


def _rms(x, g):
    return x * lax.rsqrt(jnp.mean(x * x, axis=-1, keepdims=True) + EPS) * g


def _silu(x):
    return x * (1.0 / (1.0 + jnp.exp(-x)))


def _softplus(x):
    return jnp.maximum(x, 0.0) + jnp.log1p(jnp.exp(-jnp.abs(x)))


def _store_lane_blocks(ref, val):
    for j in range(ref.shape[0]):
        ref[j] = val[:, j * LANE:(j + 1) * LANE]


def _load_strided_rows(ref, start, size, stride):
    return jnp.concatenate([ref[j, pl.ds(start, size, stride=stride), :] for j in range(ref.shape[0])], axis=1)


def _inproj_kernel(xp_ref, x_ref, xn_ref, g_ref, w_ref, wx_ref, wdt_ref, cw_ref, cb_ref,
                   cos_ref, sa_ref, sb_ref,
                   q_ref, k_ref, vT_ref, z_ref, xc_ref, dtT_ref, hext_s, xbc_s,
                   *, tm, n_seq_tiles):
    tile = pl.program_id(0) % n_seq_tiles
    norm = lambda r: _rms(r[...], g_ref[...]).astype(BF16)
    hb = norm(x_ref)
    hext_s[0:HALO, :] = jnp.where(tile > 0, norm(xp_ref), jnp.zeros((HALO, D_MODEL), BF16))
    hext_s[HALO:HALO + tm, :] = hb
    hext_s[HALO + tm:, :] = jnp.where(tile < n_seq_tiles - 1, norm(xn_ref), jnp.zeros((HALO, D_MODEL), BF16))

    f = CONV_COLS
    nchunks = XBC_COLS // f

    def xbc_proj(c):
        xbc_s[c % 2] = jnp.dot(hext_s[...], wx_ref[:, c * f:(c + 1) * f], preferred_element_type=F32)

    def conv(c):
        cols = slice(c * f, (c + 1) * f)
        acc = cb_ref[:, cols]
        for kk in range(SSM_CONV):
            off = HALO - SSM_CONV // 2 + kk
            acc = acc + xbc_s[c % 2, off:off + tm, :] * cw_ref[kk:kk + 1, cols]
        xc_ref[:, cols] = _silu(acc).astype(BF16)

    xbc_proj(0)
    for c in range(nchunks - 1):
        xbc_proj(c + 1)
        conv(c)
    proj = jnp.dot(hb, w_ref[...], preferred_element_type=F32)
    dt = jnp.dot(hb, wdt_ref[...], preferred_element_type=F32)
    conv(nchunks - 1)
    c, sa, sb = cos_ref[...], sa_ref[...], sb_ref[...]

    def rope(t):
        n = t.shape[1]
        reps = n // LANE
        cc = jnp.concatenate([c] * reps, axis=1) if reps > 1 else c
        aa = jnp.concatenate([sa] * reps, axis=1) if reps > 1 else sa
        bb = jnp.concatenate([sb] * reps, axis=1) if reps > 1 else sb
        half = ROT_DIM // 2
        return t * cc + pltpu.roll(t, n - half, 1) * aa + pltpu.roll(t, half, 1) * bb

    o = 0
    q = rope(proj[:, o:o + ATTN_WIDTH]) * (1.0 / math.sqrt(HEAD_DIM))
    o += ATTN_WIDTH
    k = rope(proj[:, o:o + KV_COLS])
    o += KV_COLS
    v = proj[:, o:o + KV_COLS]
    o += KV_COLS
    q_ref[...] = q.astype(BF16)
    k_ref[...] = k.astype(BF16)
    vT_ref[...] = v.T.astype(BF16)
    z_ref[...] = proj[:, o:o + SSM_INNER].astype(BF16)
    dtT_ref[...] = dt.T[:DT_COLS, :]


def _rope_tables(S):
    half = ROT_DIM // 2
    pos = jnp.arange(S, dtype=F32)
    inv = ROPE_THETA ** (-jnp.arange(0, ROT_DIM, 2, dtype=F32) / ROT_DIM)
    ang = pos[:, None] * inv[None, :]
    cos, sin = jnp.cos(ang), jnp.sin(ang)
    rest = HEAD_DIM - ROT_DIM
    ones, zeros, zh = jnp.ones((S, rest), F32), jnp.zeros((S, rest), F32), jnp.zeros((S, half), F32)
    c = jnp.concatenate([cos, cos, ones], axis=1)
    sa = jnp.concatenate([-sin, zh, zeros], axis=1)
    sb = jnp.concatenate([zh, sin, zeros], axis=1)
    rep = LANE // HEAD_DIM
    return tuple(jnp.tile(t, (1, rep)) for t in (c, sa, sb))


def _in_proj(x2d, g1, w_qkvz, w_xbc, w_dt, conv_w, conv_b, tables, S, tm):
    T = x2d.shape[0]
    n_seq_tiles = S // tm
    hb = tm // HALO
    nh = T // HALO
    row = lambda i: (i, 0)
    col = lambda i: (0, i)
    const = lambda i: (0, 0)
    prev = lambda i: (jnp.maximum(i * hb - 1, 0), 0)
    nxt = lambda i: (jnp.minimum((i + 1) * hb, nh - 1), 0)
    tab = lambda i: (i % n_seq_tiles, 0)
    return pl.pallas_call(
        functools.partial(_inproj_kernel, tm=tm, n_seq_tiles=n_seq_tiles),
        grid=(T // tm,),
        in_specs=[
            pl.BlockSpec((HALO, D_MODEL), prev),
            pl.BlockSpec((tm, D_MODEL), row),
            pl.BlockSpec((HALO, D_MODEL), nxt),
            pl.BlockSpec((1, D_MODEL), const),
            pl.BlockSpec((D_MODEL, QKVZ_COLS), const),
            pl.BlockSpec((D_MODEL, XBC_COLS), const),
            pl.BlockSpec((D_MODEL, LANE), const),
            pl.BlockSpec((SSM_CONV, XBC_COLS), const),
            pl.BlockSpec((1, XBC_COLS), const),
            pl.BlockSpec((tm, LANE), tab),
            pl.BlockSpec((tm, LANE), tab),
            pl.BlockSpec((tm, LANE), tab),
        ],
        out_specs=[
            pl.BlockSpec((tm, ATTN_WIDTH), row),
            pl.BlockSpec((tm, KV_COLS), row),
            pl.BlockSpec((KV_COLS, tm), col),
            pl.BlockSpec((tm, SSM_INNER), row),
            pl.BlockSpec((tm, XBC_COLS), row),
            pl.BlockSpec((DT_COLS, tm), col),
        ],
        out_shape=[
            jax.ShapeDtypeStruct((T, ATTN_WIDTH), BF16),
            jax.ShapeDtypeStruct((T, KV_COLS), BF16),
            jax.ShapeDtypeStruct((KV_COLS, T), BF16),
            jax.ShapeDtypeStruct((T, SSM_INNER), BF16),
            jax.ShapeDtypeStruct((T, XBC_COLS), BF16),
            jax.ShapeDtypeStruct((DT_COLS, T), F32),
        ],
        scratch_shapes=[pltpu.VMEM((tm + 2 * HALO, D_MODEL), BF16),
                        pltpu.VMEM((2, tm + 2 * HALO, CONV_COLS), F32)],
        compiler_params=pltpu.CompilerParams(
            dimension_semantics=("arbitrary",), vmem_limit_bytes=VMEM_LIMIT),
        name="in_proj",
    )(x2d, x2d, x2d, g1, w_qkvz, w_xbc, w_dt, conv_w, conv_b, *tables)


def _attn_kernel(sink_ref, q_ref, kp_ref, km_ref, kn_ref, vp_ref, vm_ref, vn_ref, g_ref, o_ref,
                 k_s, vT_s, *, tq):
    i = pl.program_id(1)
    n = pl.num_programs(1)
    k_s[0:BLOCK, :] = kp_ref[...]
    k_s[BLOCK:BLOCK + tq, :] = km_ref[...]
    k_s[BLOCK + tq:, :] = kn_ref[...]
    vT_s[:, 0:BLOCK] = vp_ref[...]
    vT_s[:, BLOCK:BLOCK + tq] = vm_ref[...]
    vT_s[:, BLOCK + tq:] = vn_ref[...]

    band = 3 * BLOCK
    key = lax.broadcasted_iota(jnp.int32, (band, GQA_GROUP * BLOCK), 0)
    qry = lax.broadcasted_iota(jnp.int32, (band, GQA_GROUP * BLOCK), 1) % BLOCK
    in_window = jnp.abs(key - BLOCK - qry) <= WINDOW
    first_key = jnp.where(i > 0, 0, BLOCK)
    last_key = jnp.where(i < n - 1, band, 2 * BLOCK)

    nsub = tq // BLOCK
    for j in range(nsub):
        valid = in_window
        if j == 0:
            valid = valid & (key >= first_key)
        if j == nsub - 1:
            valid = valid & (key < last_key)
        qj = q_ref[j * BLOCK:(j + 1) * BLOCK, :]
        kj = k_s[j * BLOCK:j * BLOCK + band, :]
        vTj = vT_s[:, j * BLOCK:j * BLOCK + band]
        outs = []
        for kh in range(KV_HEADS):
            heads = [kh * GQA_GROUP + g for g in range(GQA_GROUP)]
            kk = kj[:, kh * HEAD_DIM:(kh + 1) * HEAD_DIM]
            qs = jnp.concatenate([qj[:, h * HEAD_DIM:(h + 1) * HEAD_DIM] for h in heads], axis=0)
            sT = lax.dot_general(kk, qs, (((1,), (1,)), ((), ())), preferred_element_type=F32)
            sT = jnp.where(valid, sT, NEG)
            sink = sink_ref[kh:kh + 1, :]
            m = jnp.maximum(jnp.max(sT, axis=0, keepdims=True), sink)
            p = jnp.exp(sT - m)
            denom = jnp.sum(p, axis=0, keepdims=True) + jnp.exp(sink - m)
            oT = jnp.dot(vTj[kh * HEAD_DIM:(kh + 1) * HEAD_DIM, :], p.astype(BF16),
                         preferred_element_type=F32) / denom
            outs += [oT[:, g * BLOCK:(g + 1) * BLOCK] for g in range(GQA_GROUP)]
        oT_all = jnp.concatenate(outs, axis=0)
        inv = lax.rsqrt(jnp.mean(oT_all * oT_all, axis=0, keepdims=True) + EPS)
        o_ref[j * BLOCK:(j + 1) * BLOCK, :] = ((oT_all * inv).T * g_ref[...]).astype(BF16)


def _attention(q, k, vT, sink_rows, g, B, S, tq):
    T = q.shape[0]
    nq = S // tq
    sub = tq // BLOCK
    nblk = S // BLOCK
    main = lambda b, i: (b * nq + i, 0)
    prev = lambda b, i: (b * nblk + jnp.maximum(i * sub - 1, 0), 0)
    nxt = lambda b, i: (b * nblk + jnp.minimum((i + 1) * sub, nblk - 1), 0)
    swap = lambda f: (lambda b, i: f(b, i)[::-1])
    const = lambda b, i: (0, 0)
    return pl.pallas_call(
        functools.partial(_attn_kernel, tq=tq),
        grid=(B, nq),
        in_specs=[
            pl.BlockSpec((KV_HEADS, GQA_GROUP * BLOCK), const),
            pl.BlockSpec((tq, ATTN_WIDTH), main),
            pl.BlockSpec((BLOCK, KV_COLS), prev),
            pl.BlockSpec((tq, KV_COLS), main),
            pl.BlockSpec((BLOCK, KV_COLS), nxt),
            pl.BlockSpec((KV_COLS, BLOCK), swap(prev)),
            pl.BlockSpec((KV_COLS, tq), swap(main)),
            pl.BlockSpec((KV_COLS, BLOCK), swap(nxt)),
            pl.BlockSpec((1, ATTN_WIDTH), const),
        ],
        out_specs=pl.BlockSpec((tq, ATTN_WIDTH), main),
        out_shape=jax.ShapeDtypeStruct((T, ATTN_WIDTH), BF16),
        scratch_shapes=[pltpu.VMEM((tq + 2 * BLOCK, KV_COLS), BF16),
                        pltpu.VMEM((KV_COLS, tq + 2 * BLOCK), BF16)],
        compiler_params=pltpu.CompilerParams(
            dimension_semantics=("arbitrary", "arbitrary"), vmem_limit_bytes=VMEM_LIMIT),
        name="attention",
    )(sink_rows, q, k, k, k, vT, vT, vT, g)


def _split_terms(v):
    hi = v.astype(BF16).astype(F32)
    r1 = v - hi
    mid = r1.astype(BF16).astype(F32)
    lo = (r1 - mid).astype(BF16).astype(F32)
    return jnp.concatenate([hi, mid, lo], axis=0)


def _ssd_kernel(*refs, reverse, ts):
    if reverse:
        (xc_ref, dtT_ref, biasT_ref, aT_ref, e_ref, z_ref, yf_ref, ng_ref, o_ref, h_s) = refs
    else:
        (xc_ref, dtT_ref, biasT_ref, aT_ref, e_ref, dskip_ref, o_ref, h_s) = refs
    t = pl.program_id(1)
    d0 = SSM_HEADS if reverse else 0

    @pl.when(t == 0)
    def _():
        h_s[...] = jnp.zeros_like(h_s)

    li = lax.broadcasted_iota(jnp.int32, (CHUNK, CHUNK), 0)
    si = lax.broadcasted_iota(jnp.int32, (CHUNK, CHUNK), 1)
    causal = (li <= si) if reverse else (li >= si)
    scan_op = jnp.where((li >= si) if reverse else (li <= si), 1.0, 0.0).astype(BF16)
    last = 0 if reverse else CHUNK - 1
    nchunk = ts // CHUNK
    nterm = SPLIT_TERMS * DT_COLS

    def chunks_of(vT):
        return [vT[:, c * CHUNK:(c + 1) * CHUNK] for c in range(nchunk)]

    def sum_terms(m):
        return m[0:DT_COLS] + m[DT_COLS:2 * DT_COLS] + m[2 * DT_COLS:nterm]

    dtT_all = _softplus(dtT_ref[...] + biasT_ref[...])
    a_terms = jnp.concatenate([_split_terms(v) for v in chunks_of(dtT_all * aT_ref[...])], axis=0)
    cs = jnp.dot(a_terms.astype(BF16), scan_op, preferred_element_type=F32)
    dtT_c = chunks_of(dtT_all)
    acsT_c = [sum_terms(cs[c * nterm:(c + 1) * nterm]) for c in range(nchunk)]
    acs_c = [v.T for v in acsT_c]
    decayT_c = [dtT_c[c] * jnp.exp(acsT_c[c][:, last:last + 1] - acsT_c[c]) for c in range(nchunk)]

    def expand(vT_c):
        terms = jnp.concatenate([_split_terms(v).T for v in vT_c], axis=0)
        return jnp.dot(terms.astype(BF16), e_ref[...], preferred_element_type=F32)

    exp_acs_all = expand([jnp.exp(v) for v in acsT_c])
    decay_all = expand(decayT_c)

    order = range(nchunk - 1, -1, -1) if reverse else range(nchunk)
    for c in order:
        r0 = c * CHUNK
        xc = xc_ref[r0:r0 + CHUNK, :]
        x_b = xc[:, :SSM_INNER]
        xs = x_b.astype(F32)
        bm = xc[:, SSM_INNER:SSM_INNER + SSM_GROUPS * D_STATE]
        cm = xc[:, SSM_INNER + SSM_GROUPS * D_STATE:]

        dtT, acsT, acs = dtT_c[c], acsT_c[c], acs_c[c]
        exp_acs = exp_acs_all[r0:r0 + CHUNK, :]
        x_decay = (xs * decay_all[r0:r0 + CHUNK, :]).astype(BF16)
        chunk_decay = exp_acs[last:last + 1, :]

        ys = []
        for g in range(SSM_GROUPS):
            bg = bm[:, g * D_STATE:(g + 1) * D_STATE]
            cg = cm[:, g * D_STATE:(g + 1) * D_STATE]
            cb = lax.dot_general(cg, bg, (((1,), (1,)), ((), ())), preferred_element_type=F32)
            gs = slice(g * GROUP_W, (g + 1) * GROUP_W)
            bgT = bg.astype(F32).T.astype(BF16)
            st = jnp.dot(bgT, x_decay[:, gs], preferred_element_type=F32)
            h_prev = h_s[g]
            y_off = jnp.dot(cg, h_prev.astype(BF16), preferred_element_type=F32)
            h_s[g] = h_prev * chunk_decay[:, gs] + st
            y_diag = []
            for rr in range(SSM_HPG):
                r = g * SSM_HPG + rr
                seg = acs[:, d0 + r:d0 + r + 1] - acsT[d0 + r:d0 + r + 1, :]
                lmat = jnp.exp(jnp.where(causal, seg, -jnp.inf)) * dtT[d0 + r:d0 + r + 1, :]
                mm = (cb * lmat).astype(BF16)
                y_diag.append(jnp.dot(mm, x_b[:, r * SSM_HEAD_DIM:(r + 1) * SSM_HEAD_DIM],
                                      preferred_element_type=F32))
            ys.append(jnp.concatenate(y_diag, axis=1) + y_off * exp_acs[:, gs])
        y = jnp.concatenate(ys, axis=1)

        if not reverse:
            o_ref[r0:r0 + CHUNK, :] = y + xs * dskip_ref[...]
        else:
            y = (y + yf_ref[r0:r0 + CHUNK, :]) * _silu(z_ref[r0:r0 + CHUNK, :].astype(F32))
            parts = []
            for g in range(SSM_GROUPS):
                yg = y[:, g * GROUP_W:(g + 1) * GROUP_W]
                parts.append(yg * lax.rsqrt(jnp.mean(yg * yg, axis=-1, keepdims=True) + EPS))
            o_ref[r0:r0 + CHUNK, :] = (jnp.concatenate(parts, axis=1) * ng_ref[...]).astype(BF16)


def _ssd_pass(xc, dtT, biasT, aT, e, extra, B, S, ts, reverse):
    T = xc.shape[0]
    nt = S // ts
    seq = (lambda t: nt - 1 - t) if reverse else (lambda t: t)
    main = lambda b, t: (b * nt + seq(t), 0)
    const = lambda b, t: (0, 0)
    in_specs = [
        pl.BlockSpec((ts, XBC_COLS), main),
        pl.BlockSpec((DT_COLS, ts), lambda b, t: (0, b * nt + seq(t))),
        pl.BlockSpec((DT_COLS, 1), const),
        pl.BlockSpec((DT_COLS, 1), const),
        pl.BlockSpec((SPLIT_TERMS * DT_COLS, SSM_INNER), const),
    ]
    args = [xc, dtT, biasT, aT, e]
    if reverse:
        z, yf, ng = extra
        in_specs += [pl.BlockSpec((ts, SSM_INNER), main), pl.BlockSpec((ts, SSM_INNER), main),
                     pl.BlockSpec((1, SSM_INNER), const)]
        args += [z, yf, ng]
        out_dtype = BF16
    else:
        (dskip,) = extra
        in_specs += [pl.BlockSpec((1, SSM_INNER), const)]
        args += [dskip]
        out_dtype = F32
    return pl.pallas_call(
        functools.partial(_ssd_kernel, reverse=reverse, ts=ts),
        grid=(B, nt),
        in_specs=in_specs,
        out_specs=pl.BlockSpec((ts, SSM_INNER), main),
        out_shape=jax.ShapeDtypeStruct((T, SSM_INNER), out_dtype),
        scratch_shapes=[pltpu.VMEM((SSM_GROUPS, D_STATE, GROUP_W), F32)],
        compiler_params=pltpu.CompilerParams(
            dimension_semantics=("arbitrary", "arbitrary"), vmem_limit_bytes=VMEM_LIMIT),
        name="ssd_bwd" if reverse else "ssd_fwd",
    )(*args)


def _outproj_kernel(x_ref, a_ref, s_ref, wa_ref, ws_ref, g_ref, perm_ref, x1_ref, h_ref):
    x1 = (x_ref[...]
          + jnp.dot(a_ref[...], wa_ref[...], preferred_element_type=F32)
          + jnp.dot(s_ref[...], ws_ref[...], preferred_element_type=F32))
    x1_ref[...] = x1
    h = _rms(x1, g_ref[...]).astype(BF16)
    h_ref[...] = jnp.dot(perm_ref[...], h, preferred_element_type=F32).astype(BF16)


def _interleave_matrix(tm):
    nv = tm // SUBLANE
    t = jnp.arange(tm)
    return ((t % nv) * SUBLANE + t // nv)[None, :] == jnp.arange(tm)[:, None]


def _out_proj(x2d, attn, ssd, w_attn, w_ssd, g2, tm):
    T = x2d.shape[0]
    row = lambda i: (i, 0)
    const = lambda i: (0, 0)
    return pl.pallas_call(
        _outproj_kernel,
        grid=(T // tm,),
        in_specs=[
            pl.BlockSpec((tm, D_MODEL), row),
            pl.BlockSpec((tm, ATTN_WIDTH), row),
            pl.BlockSpec((tm, SSM_INNER), row),
            pl.BlockSpec((ATTN_WIDTH, D_MODEL), const),
            pl.BlockSpec((SSM_INNER, D_MODEL), const),
            pl.BlockSpec((1, D_MODEL), const),
            pl.BlockSpec((tm, tm), const),
        ],
        out_specs=[pl.BlockSpec((tm, D_MODEL), row), pl.BlockSpec((tm, D_MODEL), row)],
        out_shape=[jax.ShapeDtypeStruct((T, D_MODEL), F32), jax.ShapeDtypeStruct((T, D_MODEL), BF16)],
        compiler_params=pltpu.CompilerParams(
            dimension_semantics=("arbitrary",), vmem_limit_bytes=VMEM_LIMIT),
        name="out_proj",
    )(x2d, attn, ssd, w_attn, w_ssd, g2, _interleave_matrix(tm).astype(BF16))


FFN_COLS = 256


def _ffn_kernel(hp_ref, hm_ref, hn_ref, x1_ref, wup_ref, cw_ref, cb_ref, wdn_ref, fg_ref, o_ref,
                hext_s, u_s, glu_s, d_s, *, tm, n_seq_tiles):
    tile = pl.program_id(0) % n_seq_tiles
    nv = tm // SUBLANE
    prev_row = tm + HALO - 1
    next_row = tm + HALO
    hext_s[0:tm, :] = hm_ref[...]
    hext_s[tm:tm + HALO, :] = jnp.where(tile > 0, hp_ref[...], jnp.zeros_like(hp_ref))
    hext_s[tm + HALO:, :] = jnp.where(tile < n_seq_tiles - 1, hn_ref[...], jnp.zeros_like(hn_ref))
    f = FFN_COLS
    nchunks = D_FF // f

    def up(c):
        hext = hext_s[...]
        for half, base in enumerate((0, D_FF)):
            u_s[c % 2, :, half * f:(half + 1) * f] = jnp.dot(
                hext, wup_ref[:, base + c * f:base + (c + 1) * f], preferred_element_type=F32)

    def conv(c, half, base):
        cols = slice(base + c * f, base + (c + 1) * f)
        lanes = slice(half * f, (half + 1) * f)
        u = lambda lo, hi: u_s[c % 2, lo:hi, lanes]
        first = jnp.concatenate([u(prev_row, prev_row + 1), u(tm - SUBLANE, tm - 1)], axis=0)
        last = jnp.concatenate([u(1, SUBLANE), u(next_row, next_row + 1)], axis=0)
        before = jnp.concatenate([first, u(0, tm - SUBLANE)], axis=0)
        after = jnp.concatenate([u(SUBLANE, tm), last], axis=0)
        return (cb_ref[:, cols] + before * cw_ref[0:1, cols] + u(0, tm) * cw_ref[1:2, cols]
                + after * cw_ref[2:3, cols])

    up(0)
    for c in range(nchunks):
        if c + 1 < nchunks:
            up(c + 1)
        glu_s[:, c * f:(c + 1) * f] = (_silu(conv(c, 0, 0)) * conv(c, 1, D_FF)).astype(BF16)
    _store_lane_blocks(d_s, jnp.dot(glu_s[...], wdn_ref[...], preferred_element_type=F32))
    for s in range(SUBLANE):
        rows = slice(s * nv, (s + 1) * nv)
        x2 = x1_ref[rows, :] + _load_strided_rows(d_s, s, nv, SUBLANE)
        o_ref[rows, :] = _rms(x2, fg_ref[...])


def _ffn(h2, x1, w_up, conv_w, conv_b, w_down, fg, S, tm):
    T = h2.shape[0]
    n_seq_tiles = S // tm
    hb = tm // HALO
    nh = T // HALO
    row = lambda i: (i, 0)
    const = lambda i: (0, 0)
    prev = lambda i: (jnp.maximum(i * hb - 1, 0), 0)
    nxt = lambda i: (jnp.minimum((i + 1) * hb, nh - 1), 0)
    return pl.pallas_call(
        functools.partial(_ffn_kernel, tm=tm, n_seq_tiles=n_seq_tiles),
        grid=(T // tm,),
        in_specs=[
            pl.BlockSpec((HALO, D_MODEL), prev),
            pl.BlockSpec((tm, D_MODEL), row),
            pl.BlockSpec((HALO, D_MODEL), nxt),
            pl.BlockSpec((tm, D_MODEL), row),
            pl.BlockSpec((D_MODEL, 2 * D_FF), const),
            pl.BlockSpec((FFN_CONV, 2 * D_FF), const),
            pl.BlockSpec((1, 2 * D_FF), const),
            pl.BlockSpec((D_FF, D_MODEL), const),
            pl.BlockSpec((1, D_MODEL), const),
        ],
        out_specs=pl.BlockSpec((tm, D_MODEL), row),
        out_shape=jax.ShapeDtypeStruct((T, D_MODEL), F32),
        scratch_shapes=[pltpu.VMEM((tm + 2 * HALO, D_MODEL), BF16),
                        pltpu.VMEM((2, tm + 2 * HALO, 2 * FFN_COLS), F32),
                        pltpu.VMEM((tm, D_FF), BF16),
                        pltpu.VMEM((D_MODEL // LANE, tm, LANE), F32)],
        compiler_params=pltpu.CompilerParams(
            dimension_semantics=("arbitrary",), vmem_limit_bytes=VMEM_LIMIT),
        name="ffn",
    )(h2, h2, h2, x1, w_up, conv_w, conv_b, w_down, fg)


def _layer(x2d, B, S, norm1_g, w_in, attn_sink, attn_out_g, ssd_conv_w, ssd_conv_b,
           dt_bias_f, dt_bias_b, a_log_f, a_log_b, ssd_d, ssd_norm_g, w_out, norm2_g,
           w_up, ffn_conv_w, ffn_conv_b, w_down, out_g, tables, tm, ts, tq):
    row = lambda v: v.reshape(1, -1).astype(F32)
    w_qkvz = w_in[:, :QKVZ_COLS].astype(BF16)
    w_xbc = w_in[:, QKVZ_COLS:QKVZ_COLS + XBC_COLS].astype(BF16)
    w_dt = jnp.pad(w_in[:, QKVZ_COLS + XBC_COLS:], ((0, 0), (0, LANE - DT_COLS))).astype(BF16)
    q, k, vT, z, xc, dtT = _in_proj(x2d, row(norm1_g), w_qkvz, w_xbc, w_dt,
                                    ssd_conv_w.astype(F32), row(ssd_conv_b), tables, S, tm)

    sink_rows = jnp.repeat(attn_sink.astype(F32), BLOCK).reshape(KV_HEADS, GQA_GROUP * BLOCK)
    attn = _attention(q, k, vT, sink_rows, row(attn_out_g), B, S, tq)

    bias = jnp.concatenate([dt_bias_f, dt_bias_b]).astype(F32)
    a = -jnp.exp(jnp.concatenate([a_log_f, a_log_b]).astype(F32))
    head_of_col = jnp.arange(SSM_INNER) // SSM_HEAD_DIM
    ys = None
    for reverse in (False, True):
        d0 = SSM_HEADS if reverse else 0
        term_row = jnp.arange(SPLIT_TERMS * DT_COLS) % DT_COLS
        e = (term_row[:, None] == (head_of_col + d0)[None, :]).astype(BF16)
        extra = (z, ys, row(ssd_norm_g)) if reverse else (row(jnp.repeat(ssd_d.astype(F32), SSM_HEAD_DIM)),)
        ys = _ssd_pass(xc, dtT, bias.reshape(-1, 1), a.reshape(-1, 1), e, extra, B, S, ts, reverse)

    w_o = w_out.astype(BF16)
    x1, h2 = _out_proj(x2d, attn, ys, w_o[:ATTN_WIDTH], w_o[ATTN_WIDTH:], row(norm2_g), tm)
    return _ffn(h2, x1, w_up.astype(BF16), ffn_conv_w.astype(F32), row(ffn_conv_b),
                w_down.astype(BF16), out_g, S, tm)


def kernel(x, norm1_g, w_in, attn_sink, attn_out_g, ssd_conv_w, ssd_conv_b, ssd_dt_bias_fwd,
           ssd_dt_bias_bwd, ssd_a_log_fwd, ssd_a_log_bwd, ssd_d, ssd_norm_g, w_out, norm2_g, w_up,
           ffn_conv_w, ffn_conv_b, w_down, final_norm_g, *, tm=512, ts=1024, tq=1024):
    B, S, _ = x.shape
    depth = w_in.shape[0]
    assert depth == 1, "the fused final RMSNorm assumes a single layer"
    tables = _rope_tables(S)
    x2d = x.reshape(B * S, D_MODEL)
    out = _layer(x2d, B, S, norm1_g[0], w_in[0], attn_sink[0], attn_out_g[0], ssd_conv_w[0], ssd_conv_b[0],
                 ssd_dt_bias_fwd[0], ssd_dt_bias_bwd[0], ssd_a_log_fwd[0], ssd_a_log_bwd[0], ssd_d[0],
                 ssd_norm_g[0], w_out[0], norm2_g[0], w_up[0], ffn_conv_w[0], ffn_conv_b[0], w_down[0],
                 final_norm_g.reshape(1, -1).astype(F32), tables, tm, ts, tq)
    return out.reshape(B, S, D_MODEL)
```

```python
import functools
import math

import jax
import jax.numpy as jnp
from jax import lax
from jax.experimental import pallas as pl
from jax.experimental.pallas import tpu as pltpu

D_MODEL = 1024
HEAD_DIM = 64
ATTN_WIDTH = 512
ATTN_HEADS = 8
KV_HEADS = 2
GQA_GROUP = 4
ROT_DIM = 16
ROPE_THETA = 500000.0
WINDOW = 128
BLOCK = 128
SSM_INNER = 512
SSM_HEAD_DIM = 64
SSM_HEADS = 8
SSM_GROUPS = 2
SSM_HPG = 4
D_STATE = 128
SSM_CONV = 5
CHUNK = 128
D_FF = 2816
FFN_CONV = 3
EPS = 1e-5
NEG = -1e30

KV_COLS = KV_HEADS * HEAD_DIM
XBC_COLS = SSM_INNER + 2 * SSM_GROUPS * D_STATE
QKVZ_COLS = ATTN_WIDTH + 2 * KV_COLS + SSM_INNER
DT_COLS = 2 * SSM_HEADS
SPLIT_TERMS = 3
GROUP_W = SSM_HPG * SSM_HEAD_DIM

LANE = 128
SUBLANE = 8
BF16_ROWS = 16
HALO = BF16_ROWS
assert WINDOW == BLOCK, "the banded attention masks assume one key block of reach on each side"
CONV_COLS = 256
VMEM_LIMIT = 56 * 1024 * 1024

F32 = jnp.float32
BF16 = jnp.bfloat16


def _rms(x, g):
    return x * lax.rsqrt(jnp.mean(x * x, axis=-1, keepdims=True) + EPS) * g


def _silu(x):
    return x * (1.0 / (1.0 + jnp.exp(-x)))


def _softplus(x):
    return jnp.maximum(x, 0.0) + jnp.log1p(jnp.exp(-jnp.abs(x)))


def _store_lane_blocks(ref, val):
    for j in range(ref.shape[0]):
        ref[j] = val[:, j * LANE:(j + 1) * LANE]


def _load_strided_rows(ref, start, size, stride):
    return jnp.concatenate([ref[j, pl.ds(start, size, stride=stride), :] for j in range(ref.shape[0])], axis=1)


def _inproj_kernel(xp_ref, x_ref, xn_ref, g_ref, w_ref, wx_ref, wdt_ref, cw_ref, cb_ref,
                   cos_ref, sa_ref, sb_ref,
                   perm_ref, q_ref, k_ref, vT_ref, z_ref, xc_ref, dtT_ref, hext_s, xbc_s, proj_s, xcp_s,
                   *, tm, n_seq_tiles):
    tile = pl.program_id(0) % n_seq_tiles
    nv = tm // SUBLANE
    norm = lambda r: _rms(r[...], g_ref[...]).astype(BF16)
    hb = norm(x_ref)
    hext_s[0:tm, :] = jnp.dot(perm_ref[...], hb, preferred_element_type=F32).astype(BF16)
    hext_s[tm:tm + HALO, :] = jnp.where(tile > 0, norm(xp_ref), jnp.zeros((HALO, D_MODEL), BF16))
    hext_s[tm + HALO:, :] = jnp.where(tile < n_seq_tiles - 1, norm(xn_ref), jnp.zeros((HALO, D_MODEL), BF16))

    f = CONV_COLS
    nchunks = XBC_COLS // f

    def xbc_proj(c):
        xbc_s[c % 2] = jnp.dot(hext_s[...], wx_ref[:, c * f:(c + 1) * f], preferred_element_type=F32)

    def conv(c):
        cols = slice(c * f, (c + 1) * f)
        u = lambda lo, hi: xbc_s[c % 2, lo:hi, :]
        tok = lambda t: u(tm + HALO + t, tm + HALO + t + 1)
        cat = lambda *parts: jnp.concatenate(parts, axis=0)
        wrap_m1 = cat(tok(-1), u(tm - SUBLANE, tm - 1))
        wrap_m2 = cat(tok(-2), u(tm - 2 * SUBLANE, tm - SUBLANE - 1))
        wrap_p1 = cat(u(1, SUBLANE), tok(0))
        wrap_p2 = cat(u(SUBLANE + 1, 2 * SUBLANE), tok(1))
        taps = [cat(wrap_m2, wrap_m1, u(0, tm - 2 * SUBLANE)),
                cat(wrap_m1, u(0, tm - SUBLANE)),
                u(0, tm),
                cat(u(SUBLANE, tm), wrap_p1),
                cat(u(2 * SUBLANE, tm), wrap_p1, wrap_p2)]
        acc = cb_ref[:, cols]
        for kk in range(SSM_CONV):
            acc = acc + taps[kk] * cw_ref[kk:kk + 1, cols]
        act = _silu(acc)
        for j in range(f // LANE):
            xcp_s[c * (f // LANE) + j] = act[:, j * LANE:(j + 1) * LANE]

    xbc_proj(0)
    proj_s[:, :QKVZ_COLS] = jnp.dot(hb, w_ref[...], preferred_element_type=F32)
    proj_s[:, QKVZ_COLS:] = jnp.dot(hb, wdt_ref[...], preferred_element_type=F32)
    for c in range(nchunks - 1):
        xbc_proj(c + 1)
        conv(c)
    conv(nchunks - 1)
    for s in range(SUBLANE):
        xc_ref[s * nv:(s + 1) * nv, :] = _load_strided_rows(xcp_s, s, nv, SUBLANE).astype(BF16)
    proj = proj_s[:, :QKVZ_COLS]
    dt = proj_s[:, QKVZ_COLS:]
    c, sa, sb = cos_ref[...], sa_ref[...], sb_ref[...]

    def rope(t):
        n = t.shape[1]
        reps = n // LANE
        cc = jnp.concatenate([c] * reps, axis=1) if reps > 1 else c
        aa = jnp.concatenate([sa] * reps, axis=1) if reps > 1 else sa
        bb = jnp.concatenate([sb] * reps, axis=1) if reps > 1 else sb
        half = ROT_DIM // 2
        return t * cc + pltpu.roll(t, n - half, 1) * aa + pltpu.roll(t, half, 1) * bb

    o = 0
    q = rope(proj[:, o:o + ATTN_WIDTH]) * (1.0 / math.sqrt(HEAD_DIM))
    o += ATTN_WIDTH
    k = rope(proj[:, o:o + KV_COLS])
    o += KV_COLS
    v = proj[:, o:o + KV_COLS]
    o += KV_COLS
    q_ref[...] = q.astype(BF16)
    k_ref[...] = k.astype(BF16)
    vT_ref[...] = v.T.astype(BF16)
    z_ref[...] = proj[:, o:o + SSM_INNER].astype(BF16)
    dtT_ref[...] = dt.T[:DT_COLS, :]


def _rope_tables(S):
    half = ROT_DIM // 2
    pos = jnp.arange(S, dtype=F32)
    inv = ROPE_THETA ** (-jnp.arange(0, ROT_DIM, 2, dtype=F32) / ROT_DIM)
    ang = pos[:, None] * inv[None, :]
    cos, sin = jnp.cos(ang), jnp.sin(ang)
    rest = HEAD_DIM - ROT_DIM
    ones, zeros, zh = jnp.ones((S, rest), F32), jnp.zeros((S, rest), F32), jnp.zeros((S, half), F32)
    c = jnp.concatenate([cos, cos, ones], axis=1)
    sa = jnp.concatenate([-sin, zh, zeros], axis=1)
    sb = jnp.concatenate([zh, sin, zeros], axis=1)
    rep = LANE // HEAD_DIM
    return tuple(jnp.tile(t, (1, rep)) for t in (c, sa, sb))


def _in_proj(x2d, g1, w_qkvz, w_xbc, w_dt, conv_w, conv_b, tables, S, tm):
    T = x2d.shape[0]
    n_seq_tiles = S // tm
    hb = tm // HALO
    nh = T // HALO
    row = lambda i: (i, 0)
    col = lambda i: (0, i)
    const = lambda i: (0, 0)
    prev = lambda i: (jnp.maximum(i * hb - 1, 0), 0)
    nxt = lambda i: (jnp.minimum((i + 1) * hb, nh - 1), 0)
    tab = lambda i: (i % n_seq_tiles, 0)
    return pl.pallas_call(
        functools.partial(_inproj_kernel, tm=tm, n_seq_tiles=n_seq_tiles),
        grid=(T // tm,),
        in_specs=[
            pl.BlockSpec((HALO, D_MODEL), prev),
            pl.BlockSpec((tm, D_MODEL), row),
            pl.BlockSpec((HALO, D_MODEL), nxt),
            pl.BlockSpec((1, D_MODEL), const),
            pl.BlockSpec((D_MODEL, QKVZ_COLS), const),
            pl.BlockSpec((D_MODEL, XBC_COLS), const),
            pl.BlockSpec((D_MODEL, LANE), const),
            pl.BlockSpec((SSM_CONV, XBC_COLS), const),
            pl.BlockSpec((1, XBC_COLS), const),
            pl.BlockSpec((tm, LANE), tab),
            pl.BlockSpec((tm, LANE), tab),
            pl.BlockSpec((tm, LANE), tab),
            pl.BlockSpec((tm, tm), const),
        ],
        out_specs=[
            pl.BlockSpec((tm, ATTN_WIDTH), row),
            pl.BlockSpec((tm, KV_COLS), row),
            pl.BlockSpec((KV_COLS, tm), col),
            pl.BlockSpec((tm, SSM_INNER), row),
            pl.BlockSpec((tm, XBC_COLS), row),
            pl.BlockSpec((DT_COLS, tm), col),
        ],
        out_shape=[
            jax.ShapeDtypeStruct((T, ATTN_WIDTH), BF16),
            jax.ShapeDtypeStruct((T, KV_COLS), BF16),
            jax.ShapeDtypeStruct((KV_COLS, T), BF16),
            jax.ShapeDtypeStruct((T, SSM_INNER), BF16),
            jax.ShapeDtypeStruct((T, XBC_COLS), BF16),
            jax.ShapeDtypeStruct((DT_COLS, T), F32),
        ],
        scratch_shapes=[pltpu.VMEM((tm + 2 * HALO, D_MODEL), BF16),
                        pltpu.VMEM((2, tm + 2 * HALO, CONV_COLS), F32),
                        pltpu.VMEM((tm, QKVZ_COLS + LANE), F32),
                        pltpu.VMEM((XBC_COLS // LANE, tm, LANE), F32)],
        compiler_params=pltpu.CompilerParams(
            dimension_semantics=("arbitrary",), vmem_limit_bytes=VMEM_LIMIT),
        name="in_proj",
    )(x2d, x2d, x2d, g1, w_qkvz, w_xbc, w_dt, conv_w, conv_b, *tables, _interleave_matrix(tm).astype(BF16))


def _attn_kernel(sink_ref, q_ref, kp_ref, km_ref, kn_ref, vp_ref, vm_ref, vn_ref, g_ref, o_ref,
                 k_s, vT_s, *, tq):
    i = pl.program_id(1)
    n = pl.num_programs(1)
    k_s[0:BLOCK, :] = kp_ref[...]
    k_s[BLOCK:BLOCK + tq, :] = km_ref[...]
    k_s[BLOCK + tq:, :] = kn_ref[...]
    vT_s[:, 0:BLOCK] = vp_ref[...]
    vT_s[:, BLOCK:BLOCK + tq] = vm_ref[...]
    vT_s[:, BLOCK + tq:] = vn_ref[...]

    band = 3 * BLOCK
    nq = GQA_GROUP * BLOCK
    key = lax.broadcasted_iota(jnp.int32, (BLOCK, nq), 0)
    qry = lax.broadcasted_iota(jnp.int32, (BLOCK, nq), 1) % BLOCK
    no_prev = jnp.where(i > 0, 0, BLOCK)
    no_next = jnp.where(i < n - 1, 0, BLOCK)
    ones = jnp.ones((BF16_ROWS, band), BF16)

    nsub = tq // BLOCK
    for j in range(nsub):
        prev_ok = key >= (qry + no_prev if j == 0 else qry)
        next_ok = key <= (qry - no_next if j == nsub - 1 else qry)
        qj = q_ref[j * BLOCK:(j + 1) * BLOCK, :]
        kj = k_s[j * BLOCK:j * BLOCK + band, :]
        vTj = vT_s[:, j * BLOCK:j * BLOCK + band]
        outs = []
        for kh in range(KV_HEADS):
            heads = [kh * GQA_GROUP + g for g in range(GQA_GROUP)]
            kk = kj[:, kh * HEAD_DIM:(kh + 1) * HEAD_DIM]
            qs = jnp.concatenate([qj[:, h * HEAD_DIM:(h + 1) * HEAD_DIM] for h in heads], axis=0)
            sT = lax.dot_general(kk, qs, (((1,), (1,)), ((), ())), preferred_element_type=F32)
            s_blocks = [jnp.where(prev_ok, sT[:BLOCK], NEG), sT[BLOCK:2 * BLOCK],
                        jnp.where(next_ok, sT[2 * BLOCK:], NEG)]
            sink = sink_ref[kh:kh + 1, :]
            s_max = jnp.maximum(jnp.maximum(s_blocks[0], s_blocks[1]), s_blocks[2])
            m = jnp.maximum(jnp.max(s_max, axis=0, keepdims=True), sink)
            p = jnp.concatenate([jnp.exp(sb - m) for sb in s_blocks], axis=0).astype(BF16)
            v_ext = jnp.concatenate([vTj[kh * HEAD_DIM:(kh + 1) * HEAD_DIM, :], ones], axis=0)
            o_ext = jnp.dot(v_ext, p, preferred_element_type=F32)
            denom = o_ext[HEAD_DIM:HEAD_DIM + 1, :] + jnp.exp(sink - m)
            oT = o_ext[:HEAD_DIM, :] / denom
            outs += [oT[:, g * BLOCK:(g + 1) * BLOCK] for g in range(GQA_GROUP)]
        oT_all = jnp.concatenate(outs, axis=0)
        inv = lax.rsqrt(jnp.mean(oT_all * oT_all, axis=0, keepdims=True) + EPS)
        o_ref[j * BLOCK:(j + 1) * BLOCK, :] = ((oT_all * inv).T * g_ref[...]).astype(BF16)


def _attention(q, k, vT, sink_rows, g, B, S, tq):
    T = q.shape[0]
    nq = S // tq
    sub = tq // BLOCK
    nblk = S // BLOCK
    main = lambda b, i: (b * nq + i, 0)
    prev = lambda b, i: (b * nblk + jnp.maximum(i * sub - 1, 0), 0)
    nxt = lambda b, i: (b * nblk + jnp.minimum((i + 1) * sub, nblk - 1), 0)
    swap = lambda f: (lambda b, i: f(b, i)[::-1])
    const = lambda b, i: (0, 0)
    return pl.pallas_call(
        functools.partial(_attn_kernel, tq=tq),
        grid=(B, nq),
        in_specs=[
            pl.BlockSpec((KV_HEADS, GQA_GROUP * BLOCK), const),
            pl.BlockSpec((tq, ATTN_WIDTH), main),
            pl.BlockSpec((BLOCK, KV_COLS), prev),
            pl.BlockSpec((tq, KV_COLS), main),
            pl.BlockSpec((BLOCK, KV_COLS), nxt),
            pl.BlockSpec((KV_COLS, BLOCK), swap(prev)),
            pl.BlockSpec((KV_COLS, tq), swap(main)),
            pl.BlockSpec((KV_COLS, BLOCK), swap(nxt)),
            pl.BlockSpec((1, ATTN_WIDTH), const),
        ],
        out_specs=pl.BlockSpec((tq, ATTN_WIDTH), main),
        out_shape=jax.ShapeDtypeStruct((T, ATTN_WIDTH), BF16),
        scratch_shapes=[pltpu.VMEM((tq + 2 * BLOCK, KV_COLS), BF16),
                        pltpu.VMEM((KV_COLS, tq + 2 * BLOCK), BF16)],
        compiler_params=pltpu.CompilerParams(
            dimension_semantics=("arbitrary", "arbitrary"), vmem_limit_bytes=VMEM_LIMIT),
        name="attention",
    )(sink_rows, q, k, k, k, vT, vT, vT, g)


def _split_terms(v):
    hi = v.astype(BF16).astype(F32)
    r1 = v - hi
    mid = r1.astype(BF16).astype(F32)
    lo = (r1 - mid).astype(BF16).astype(F32)
    return jnp.concatenate([hi, mid, lo], axis=0)


def _ssd_kernel(*refs, reverse, ts):
    if reverse:
        (xc_ref, dtT_ref, biasT_ref, aT_ref, e_ref, z_ref, yf_ref, ng_ref, o_ref, h_s) = refs
    else:
        (xc_ref, dtT_ref, biasT_ref, aT_ref, e_ref, dskip_ref, o_ref, h_s) = refs
    t = pl.program_id(1)
    d0 = SSM_HEADS if reverse else 0

    @pl.when(t == 0)
    def _():
        h_s[...] = jnp.zeros_like(h_s)

    li = lax.broadcasted_iota(jnp.int32, (CHUNK, CHUNK), 0)
    si = lax.broadcasted_iota(jnp.int32, (CHUNK, CHUNK), 1)
    causal = (li <= si) if reverse else (li >= si)
    scan_op = jnp.where((li >= si) if reverse else (li <= si), 1.0, 0.0).astype(BF16)
    last = 0 if reverse else CHUNK - 1
    nchunk = ts // CHUNK
    nterm = SPLIT_TERMS * DT_COLS

    def chunks_of(vT):
        return [vT[:, c * CHUNK:(c + 1) * CHUNK] for c in range(nchunk)]

    def sum_terms(m):
        return m[0:DT_COLS] + m[DT_COLS:2 * DT_COLS] + m[2 * DT_COLS:nterm]

    dtT_all = _softplus(dtT_ref[...] + biasT_ref[...])
    a_terms = jnp.concatenate([_split_terms(v) for v in chunks_of(dtT_all * aT_ref[...])], axis=0)
    cs = jnp.dot(a_terms.astype(BF16), scan_op, preferred_element_type=F32)
    dtT_c = chunks_of(dtT_all)
    acsT_c = [sum_terms(cs[c * nterm:(c + 1) * nterm]) for c in range(nchunk)]
    log2e = math.log2(math.e)
    tgt_c = [(v * log2e).T for v in acsT_c]
    src_c = [acsT_c[c] * log2e - jnp.log2(dtT_c[c]) for c in range(nchunk)]
    decayT_c = [dtT_c[c] * jnp.exp(acsT_c[c][:, last:last + 1] - acsT_c[c]) for c in range(nchunk)]

    def expand(vT_c):
        terms = jnp.concatenate([_split_terms(v).T for v in vT_c], axis=0)
        return jnp.dot(terms.astype(BF16), e_ref[...], preferred_element_type=F32)

    exp_acs_all = expand([jnp.exp(v) for v in acsT_c])
    decay_all = expand(decayT_c)

    order = range(nchunk - 1, -1, -1) if reverse else range(nchunk)
    for c in order:
        r0 = c * CHUNK
        xc = xc_ref[r0:r0 + CHUNK, :]
        x_b = xc[:, :SSM_INNER]
        xs = x_b.astype(F32)
        bm = xc[:, SSM_INNER:SSM_INNER + SSM_GROUPS * D_STATE]
        cm = xc[:, SSM_INNER + SSM_GROUPS * D_STATE:]

        tgt, src = tgt_c[c], src_c[c]
        exp_acs = exp_acs_all[r0:r0 + CHUNK, :]
        x_decay = (xs * decay_all[r0:r0 + CHUNK, :]).astype(BF16)
        chunk_decay = exp_acs[last:last + 1, :]

        ys = []
        for g in range(SSM_GROUPS):
            bg = bm[:, g * D_STATE:(g + 1) * D_STATE]
            cg = cm[:, g * D_STATE:(g + 1) * D_STATE]
            cb = lax.dot_general(cg, bg, (((1,), (1,)), ((), ())), preferred_element_type=F32)
            gs = slice(g * GROUP_W, (g + 1) * GROUP_W)
            bgT = bg.astype(F32).T.astype(BF16)
            st = jnp.dot(bgT, x_decay[:, gs], preferred_element_type=F32)
            h_prev = h_s[g]
            y_off = jnp.dot(cg, h_prev.astype(BF16), preferred_element_type=F32)
            h_s[g] = h_prev * chunk_decay[:, gs] + st
            y_diag = []
            for rr in range(SSM_HPG):
                r = g * SSM_HPG + rr
                seg = tgt[:, d0 + r:d0 + r + 1] - src[d0 + r:d0 + r + 1, :]
                lmat = jnp.exp2(jnp.where(causal, seg, -jnp.inf))
                mm = (cb * lmat).astype(BF16)
                y_diag.append(jnp.dot(mm, x_b[:, r * SSM_HEAD_DIM:(r + 1) * SSM_HEAD_DIM],
                                      preferred_element_type=F32))
            ys.append(jnp.concatenate(y_diag, axis=1) + y_off * exp_acs[:, gs])
        y = jnp.concatenate(ys, axis=1)

        if not reverse:
            o_ref[r0:r0 + CHUNK, :] = y + xs * dskip_ref[...]
        else:
            y = (y + yf_ref[r0:r0 + CHUNK, :]) * _silu(z_ref[r0:r0 + CHUNK, :].astype(F32))
            parts = []
            for g in range(SSM_GROUPS):
                yg = y[:, g * GROUP_W:(g + 1) * GROUP_W]
                parts.append(yg * lax.rsqrt(jnp.mean(yg * yg, axis=-1, keepdims=True) + EPS))
            o_ref[r0:r0 + CHUNK, :] = (jnp.concatenate(parts, axis=1) * ng_ref[...]).astype(BF16)


def _ssd_pass(xc, dtT, biasT, aT, e, extra, B, S, ts, reverse):
    T = xc.shape[0]
    nt = S // ts
    seq = (lambda t: nt - 1 - t) if reverse else (lambda t: t)
    main = lambda b, t: (b * nt + seq(t), 0)
    const = lambda b, t: (0, 0)
    in_specs = [
        pl.BlockSpec((ts, XBC_COLS), main),
        pl.BlockSpec((DT_COLS, ts), lambda b, t: (0, b * nt + seq(t))),
        pl.BlockSpec((DT_COLS, 1), const),
        pl.BlockSpec((DT_COLS, 1), const),
        pl.BlockSpec((SPLIT_TERMS * DT_COLS, SSM_INNER), const),
    ]
    args = [xc, dtT, biasT, aT, e]
    if reverse:
        z, yf, ng = extra
        in_specs += [pl.BlockSpec((ts, SSM_INNER), main), pl.BlockSpec((ts, SSM_INNER), main),
                     pl.BlockSpec((1, SSM_INNER), const)]
        args += [z, yf, ng]
        out_dtype = BF16
    else:
        (dskip,) = extra
        in_specs += [pl.BlockSpec((1, SSM_INNER), const)]
        args += [dskip]
        out_dtype = F32
    return pl.pallas_call(
        functools.partial(_ssd_kernel, reverse=reverse, ts=ts),
        grid=(B, nt),
        in_specs=in_specs,
        out_specs=pl.BlockSpec((ts, SSM_INNER), main),
        out_shape=jax.ShapeDtypeStruct((T, SSM_INNER), out_dtype),
        scratch_shapes=[pltpu.VMEM((SSM_GROUPS, D_STATE, GROUP_W), F32)],
        compiler_params=pltpu.CompilerParams(
            dimension_semantics=("arbitrary", "arbitrary"), vmem_limit_bytes=VMEM_LIMIT),
        name="ssd_bwd" if reverse else "ssd_fwd",
    )(*args)


def _outproj_kernel(x_ref, a_ref, s_ref, wa_ref, ws_ref, g_ref, perm_ref, x1_ref, h_ref):
    x1 = (x_ref[...]
          + jnp.dot(a_ref[...], wa_ref[...], preferred_element_type=F32)
          + jnp.dot(s_ref[...], ws_ref[...], preferred_element_type=F32))
    x1_ref[...] = x1
    h = _rms(x1, g_ref[...]).astype(BF16)
    h_ref[...] = jnp.dot(perm_ref[...], h, preferred_element_type=F32).astype(BF16)


def _interleave_matrix(tm):
    nv = tm // SUBLANE
    t = jnp.arange(tm)
    return ((t % nv) * SUBLANE + t // nv)[None, :] == jnp.arange(tm)[:, None]


def _out_proj(x2d, attn, ssd, w_attn, w_ssd, g2, tm):
    T = x2d.shape[0]
    row = lambda i: (i, 0)
    const = lambda i: (0, 0)
    return pl.pallas_call(
        _outproj_kernel,
        grid=(T // tm,),
        in_specs=[
            pl.BlockSpec((tm, D_MODEL), row),
            pl.BlockSpec((tm, ATTN_WIDTH), row),
            pl.BlockSpec((tm, SSM_INNER), row),
            pl.BlockSpec((ATTN_WIDTH, D_MODEL), const),
            pl.BlockSpec((SSM_INNER, D_MODEL), const),
            pl.BlockSpec((1, D_MODEL), const),
            pl.BlockSpec((tm, tm), const),
        ],
        out_specs=[pl.BlockSpec((tm, D_MODEL), row), pl.BlockSpec((tm, D_MODEL), row)],
        out_shape=[jax.ShapeDtypeStruct((T, D_MODEL), F32), jax.ShapeDtypeStruct((T, D_MODEL), BF16)],
        compiler_params=pltpu.CompilerParams(
            dimension_semantics=("arbitrary",), vmem_limit_bytes=VMEM_LIMIT),
        name="out_proj",
    )(x2d, attn, ssd, w_attn, w_ssd, g2, _interleave_matrix(tm).astype(BF16))


FFN_COLS = 256


def _ffn_kernel(hp_ref, hm_ref, hn_ref, x1_ref, wup_ref, cw_ref, cb_ref, wdn_ref, fg_ref, o_ref,
                hext_s, u_s, glu_s, d_s, *, tm, n_seq_tiles):
    tile = pl.program_id(0) % n_seq_tiles
    nv = tm // SUBLANE
    prev_row = tm + HALO - 1
    next_row = tm + HALO
    hext_s[0:tm, :] = hm_ref[...]
    hext_s[tm:tm + HALO, :] = jnp.where(tile > 0, hp_ref[...], jnp.zeros_like(hp_ref))
    hext_s[tm + HALO:, :] = jnp.where(tile < n_seq_tiles - 1, hn_ref[...], jnp.zeros_like(hn_ref))
    f = FFN_COLS
    nchunks = D_FF // f

    def up(c):
        hext = hext_s[...]
        for half, base in enumerate((0, D_FF)):
            u_s[c % 2, :, half * f:(half + 1) * f] = jnp.dot(
                hext, wup_ref[:, base + c * f:base + (c + 1) * f], preferred_element_type=F32)

    def conv(c, half, base):
        cols = slice(base + c * f, base + (c + 1) * f)
        lanes = slice(half * f, (half + 1) * f)
        u = lambda lo, hi: u_s[c % 2, lo:hi, lanes]
        first = jnp.concatenate([u(prev_row, prev_row + 1), u(tm - SUBLANE, tm - 1)], axis=0)
        last = jnp.concatenate([u(1, SUBLANE), u(next_row, next_row + 1)], axis=0)
        before = jnp.concatenate([first, u(0, tm - SUBLANE)], axis=0)
        after = jnp.concatenate([u(SUBLANE, tm), last], axis=0)
        return (cb_ref[:, cols] + before * cw_ref[0:1, cols] + u(0, tm) * cw_ref[1:2, cols]
                + after * cw_ref[2:3, cols])

    up(0)
    for c in range(nchunks):
        if c + 1 < nchunks:
            up(c + 1)
        glu_s[:, c * f:(c + 1) * f] = (_silu(conv(c, 0, 0)) * conv(c, 1, D_FF)).astype(BF16)
    _store_lane_blocks(d_s, jnp.dot(glu_s[...], wdn_ref[...], preferred_element_type=F32))
    for s in range(SUBLANE):
        rows = slice(s * nv, (s + 1) * nv)
        x2 = x1_ref[rows, :] + _load_strided_rows(d_s, s, nv, SUBLANE)
        o_ref[rows, :] = _rms(x2, fg_ref[...])


def _ffn(h2, x1, w_up, conv_w, conv_b, w_down, fg, S, tm):
    T = h2.shape[0]
    n_seq_tiles = S // tm
    hb = tm // HALO
    nh = T // HALO
    row = lambda i: (i, 0)
    const = lambda i: (0, 0)
    prev = lambda i: (jnp.maximum(i * hb - 1, 0), 0)
    nxt = lambda i: (jnp.minimum((i + 1) * hb, nh - 1), 0)
    return pl.pallas_call(
        functools.partial(_ffn_kernel, tm=tm, n_seq_tiles=n_seq_tiles),
        grid=(T // tm,),
        in_specs=[
            pl.BlockSpec((HALO, D_MODEL), prev),
            pl.BlockSpec((tm, D_MODEL), row),
            pl.BlockSpec((HALO, D_MODEL), nxt),
            pl.BlockSpec((tm, D_MODEL), row),
            pl.BlockSpec((D_MODEL, 2 * D_FF), const),
            pl.BlockSpec((FFN_CONV, 2 * D_FF), const),
            pl.BlockSpec((1, 2 * D_FF), const),
            pl.BlockSpec((D_FF, D_MODEL), const),
            pl.BlockSpec((1, D_MODEL), const),
        ],
        out_specs=pl.BlockSpec((tm, D_MODEL), row),
        out_shape=jax.ShapeDtypeStruct((T, D_MODEL), F32),
        scratch_shapes=[pltpu.VMEM((tm + 2 * HALO, D_MODEL), BF16),
                        pltpu.VMEM((2, tm + 2 * HALO, 2 * FFN_COLS), F32),
                        pltpu.VMEM((tm, D_FF), BF16),
                        pltpu.VMEM((D_MODEL // LANE, tm, LANE), F32)],
        compiler_params=pltpu.CompilerParams(
            dimension_semantics=("arbitrary",), vmem_limit_bytes=VMEM_LIMIT),
        name="ffn",
    )(h2, h2, h2, x1, w_up, conv_w, conv_b, w_down, fg)


def _layer(x2d, B, S, norm1_g, w_in, attn_sink, attn_out_g, ssd_conv_w, ssd_conv_b,
           dt_bias_f, dt_bias_b, a_log_f, a_log_b, ssd_d, ssd_norm_g, w_out, norm2_g,
           w_up, ffn_conv_w, ffn_conv_b, w_down, out_g, tables, tm, ts, tq):
    row = lambda v: v.reshape(1, -1).astype(F32)
    w_qkvz = w_in[:, :QKVZ_COLS].astype(BF16)
    w_xbc = w_in[:, QKVZ_COLS:QKVZ_COLS + XBC_COLS].astype(BF16)
    w_dt = jnp.pad(w_in[:, QKVZ_COLS + XBC_COLS:], ((0, 0), (0, LANE - DT_COLS))).astype(BF16)
    q, k, vT, z, xc, dtT = _in_proj(x2d, row(norm1_g), w_qkvz, w_xbc, w_dt,
                                    ssd_conv_w.astype(F32), row(ssd_conv_b), tables, S, tm)

    sink_rows = jnp.repeat(attn_sink.astype(F32), BLOCK).reshape(KV_HEADS, GQA_GROUP * BLOCK)
    attn = _attention(q, k, vT, sink_rows, row(attn_out_g), B, S, tq)

    bias = jnp.concatenate([dt_bias_f, dt_bias_b]).astype(F32)
    a = -jnp.exp(jnp.concatenate([a_log_f, a_log_b]).astype(F32))
    head_of_col = jnp.arange(SSM_INNER) // SSM_HEAD_DIM
    ys = None
    for reverse in (False, True):
        d0 = SSM_HEADS if reverse else 0
        term_row = jnp.arange(SPLIT_TERMS * DT_COLS) % DT_COLS
        e = (term_row[:, None] == (head_of_col + d0)[None, :]).astype(BF16)
        extra = (z, ys, row(ssd_norm_g)) if reverse else (row(jnp.repeat(ssd_d.astype(F32), SSM_HEAD_DIM)),)
        ys = _ssd_pass(xc, dtT, bias.reshape(-1, 1), a.reshape(-1, 1), e, extra, B, S, ts, reverse)

    w_o = w_out.astype(BF16)
    x1, h2 = _out_proj(x2d, attn, ys, w_o[:ATTN_WIDTH], w_o[ATTN_WIDTH:], row(norm2_g), tm)
    return _ffn(h2, x1, w_up.astype(BF16), ffn_conv_w.astype(F32), row(ffn_conv_b),
                w_down.astype(BF16), out_g, S, tm)


def kernel(x, norm1_g, w_in, attn_sink, attn_out_g, ssd_conv_w, ssd_conv_b, ssd_dt_bias_fwd,
           ssd_dt_bias_bwd, ssd_a_log_fwd, ssd_a_log_bwd, ssd_d, ssd_norm_g, w_out, norm2_g, w_up,
           ffn_conv_w, ffn_conv_b, w_down, final_norm_g, *, tm=512, ts=1024, tq=1024):
    B, S, _ = x.shape
    depth = w_in.shape[0]
    assert depth == 1, "the fused final RMSNorm assumes a single layer"
    tables = _rope_tables(S)
    x2d = x.reshape(B * S, D_MODEL)
    out = _layer(x2d, B, S, norm1_g[0], w_in[0], attn_sink[0], attn_out_g[0], ssd_conv_w[0], ssd_conv_b[0],
                 ssd_dt_bias_fwd[0], ssd_dt_bias_bwd[0], ssd_a_log_fwd[0], ssd_a_log_bwd[0], ssd_d[0],
                 ssd_norm_g[0], w_out[0], norm2_g[0], w_up[0], ffn_conv_w[0], ffn_conv_b[0], w_down[0],
                 final_norm_g.reshape(1, -1).astype(F32), tables, tm, ts, tq)
    return out.reshape(B, S, D_MODEL)
```

```python
import functools
import math

import jax
import jax.numpy as jnp
from jax import lax
from jax.experimental import pallas as pl
from jax.experimental.pallas import tpu as pltpu

D_MODEL = 1024
HEAD_DIM = 64
ATTN_WIDTH = 512
ATTN_HEADS = 8
KV_HEADS = 2
GQA_GROUP = 4
ROT_DIM = 16
ROPE_THETA = 500000.0
WINDOW = 128
BLOCK = 128
SSM_INNER = 512
SSM_HEAD_DIM = 64
SSM_HEADS = 8
SSM_GROUPS = 2
SSM_HPG = 4
D_STATE = 128
SSM_CONV = 5
CHUNK = 128
D_FF = 2816
FFN_CONV = 3
EPS = 1e-5
NEG = -1e30

KV_COLS = KV_HEADS * HEAD_DIM
XBC_COLS = SSM_INNER + 2 * SSM_GROUPS * D_STATE
QKVZ_COLS = ATTN_WIDTH + 2 * KV_COLS + SSM_INNER
DT_COLS = 2 * SSM_HEADS
SPLIT_TERMS = 3
GROUP_W = SSM_HPG * SSM_HEAD_DIM

LANE = 128
SUBLANE = 8
BF16_ROWS = 16
HALO = BF16_ROWS
assert WINDOW == BLOCK, "the banded attention masks assume one key block of reach on each side"
CONV_COLS = 256
VMEM_LIMIT = 56 * 1024 * 1024

F32 = jnp.float32
BF16 = jnp.bfloat16


def _rms(x, g):
    return x * lax.rsqrt(jnp.mean(x * x, axis=-1, keepdims=True) + EPS) * g


def _silu(x):
    return x * (1.0 / (1.0 + jnp.exp(-x)))


def _softplus(x):
    return jnp.maximum(x, 0.0) + jnp.log1p(jnp.exp(-jnp.abs(x)))


def _store_lane_blocks(ref, val):
    for j in range(ref.shape[0]):
        ref[j] = val[:, j * LANE:(j + 1) * LANE]


def _load_strided_rows(ref, start, size, stride):
    return jnp.concatenate([ref[j, pl.ds(start, size, stride=stride), :] for j in range(ref.shape[0])], axis=1)


def _inproj_kernel(xp_ref, x_ref, xn_ref, g_ref, w_ref, wx_ref, wdt_ref, cw_ref, cb_ref,
                   cos_ref, sa_ref, sb_ref,
                   perm_ref, q_ref, k_ref, vT_ref, z_ref, xc_ref, dtT_ref, hext_s, xbc_s, proj_s, xcp_s,
                   *, tm, n_seq_tiles):
    tile = pl.program_id(0) % n_seq_tiles
    nv = tm // SUBLANE
    norm = lambda r: _rms(r[...], g_ref[...]).astype(BF16)
    hb = norm(x_ref)
    hext_s[0:tm, :] = jnp.dot(perm_ref[...], hb, preferred_element_type=F32).astype(BF16)
    before = jnp.where(tile > 0, norm(xp_ref), jnp.zeros((HALO, D_MODEL), BF16))
    after = jnp.where(tile < n_seq_tiles - 1, norm(xn_ref), jnp.zeros((HALO, D_MODEL), BF16))
    halo_row = lax.broadcasted_iota(jnp.int32, (HALO, D_MODEL), 0)
    hext_s[tm:, :] = jnp.where(halo_row >= HALO // 2, before, after)

    f = CONV_COLS
    nchunks = XBC_COLS // f

    def xbc_proj(c):
        xbc_s[c % 2] = jnp.dot(hext_s[...], wx_ref[:, c * f:(c + 1) * f], preferred_element_type=F32)

    def conv(c):
        cols = slice(c * f, (c + 1) * f)
        u = lambda lo, hi: xbc_s[c % 2, lo:hi, :]
        tok = lambda t: u(tm + t % HALO, tm + t % HALO + 1)
        cat = lambda *parts: jnp.concatenate(parts, axis=0)
        wrap_m1 = cat(tok(-1), u(tm - SUBLANE, tm - 1))
        wrap_m2 = cat(tok(-2), u(tm - 2 * SUBLANE, tm - SUBLANE - 1))
        wrap_p1 = cat(u(1, SUBLANE), tok(0))
        wrap_p2 = cat(u(SUBLANE + 1, 2 * SUBLANE), tok(1))
        taps = [cat(wrap_m2, wrap_m1, u(0, tm - 2 * SUBLANE)),
                cat(wrap_m1, u(0, tm - SUBLANE)),
                u(0, tm),
                cat(u(SUBLANE, tm), wrap_p1),
                cat(u(2 * SUBLANE, tm), wrap_p1, wrap_p2)]
        acc = cb_ref[:, cols]
        for kk in range(SSM_CONV):
            acc = acc + taps[kk] * cw_ref[kk:kk + 1, cols]
        act = _silu(acc)
        for j in range(f // LANE):
            xcp_s[c * (f // LANE) + j] = act[:, j * LANE:(j + 1) * LANE]

    xbc_proj(0)
    proj_s[:, :QKVZ_COLS] = jnp.dot(hb, w_ref[...], preferred_element_type=F32)
    proj_s[:, QKVZ_COLS:] = jnp.dot(hb, wdt_ref[...], preferred_element_type=F32)
    for c in range(nchunks - 1):
        xbc_proj(c + 1)
        conv(c)
    conv(nchunks - 1)
    for s in range(SUBLANE):
        xc_ref[s * nv:(s + 1) * nv, :] = _load_strided_rows(xcp_s, s, nv, SUBLANE).astype(BF16)
    proj = proj_s[:, :QKVZ_COLS]
    dt = proj_s[:, QKVZ_COLS:]
    c, sa, sb = cos_ref[...], sa_ref[...], sb_ref[...]

    def rope(t):
        n = t.shape[1]
        reps = n // LANE
        cc = jnp.concatenate([c] * reps, axis=1) if reps > 1 else c
        aa = jnp.concatenate([sa] * reps, axis=1) if reps > 1 else sa
        bb = jnp.concatenate([sb] * reps, axis=1) if reps > 1 else sb
        half = ROT_DIM // 2
        return t * cc + pltpu.roll(t, n - half, 1) * aa + pltpu.roll(t, half, 1) * bb

    o = 0
    q = rope(proj[:, o:o + ATTN_WIDTH]) * (1.0 / math.sqrt(HEAD_DIM))
    o += ATTN_WIDTH
    k = rope(proj[:, o:o + KV_COLS])
    o += KV_COLS
    v = proj[:, o:o + KV_COLS]
    o += KV_COLS
    q_ref[...] = q.astype(BF16)
    k_ref[...] = k.astype(BF16)
    vT_ref[...] = v.T.astype(BF16)
    z_ref[...] = proj[:, o:o + SSM_INNER].astype(BF16)
    dtT_ref[...] = dt.T[:DT_COLS, :]


def _rope_tables(S):
    half = ROT_DIM // 2
    pos = jnp.arange(S, dtype=F32)
    inv = ROPE_THETA ** (-jnp.arange(0, ROT_DIM, 2, dtype=F32) / ROT_DIM)
    ang = pos[:, None] * inv[None, :]
    cos, sin = jnp.cos(ang), jnp.sin(ang)
    rest = HEAD_DIM - ROT_DIM
    ones, zeros, zh = jnp.ones((S, rest), F32), jnp.zeros((S, rest), F32), jnp.zeros((S, half), F32)
    c = jnp.concatenate([cos, cos, ones], axis=1)
    sa = jnp.concatenate([-sin, zh, zeros], axis=1)
    sb = jnp.concatenate([zh, sin, zeros], axis=1)
    rep = LANE // HEAD_DIM
    return tuple(jnp.tile(t, (1, rep)) for t in (c, sa, sb))


def _in_proj(x2d, g1, w_qkvz, w_xbc, w_dt, conv_w, conv_b, tables, S, tm):
    T = x2d.shape[0]
    n_seq_tiles = S // tm
    hb = tm // HALO
    nh = T // HALO
    row = lambda i: (i, 0)
    col = lambda i: (0, i)
    const = lambda i: (0, 0)
    prev = lambda i: (jnp.maximum(i * hb - 1, 0), 0)
    nxt = lambda i: (jnp.minimum((i + 1) * hb, nh - 1), 0)
    tab = lambda i: (i % n_seq_tiles, 0)
    return pl.pallas_call(
        functools.partial(_inproj_kernel, tm=tm, n_seq_tiles=n_seq_tiles),
        grid=(T // tm,),
        in_specs=[
            pl.BlockSpec((HALO, D_MODEL), prev),
            pl.BlockSpec((tm, D_MODEL), row),
            pl.BlockSpec((HALO, D_MODEL), nxt),
            pl.BlockSpec((1, D_MODEL), const),
            pl.BlockSpec((D_MODEL, QKVZ_COLS), const),
            pl.BlockSpec((D_MODEL, XBC_COLS), const),
            pl.BlockSpec((D_MODEL, LANE), const),
            pl.BlockSpec((SSM_CONV, XBC_COLS), const),
            pl.BlockSpec((1, XBC_COLS), const),
            pl.BlockSpec((tm, LANE), tab),
            pl.BlockSpec((tm, LANE), tab),
            pl.BlockSpec((tm, LANE), tab),
            pl.BlockSpec((tm, tm), const),
        ],
        out_specs=[
            pl.BlockSpec((tm, ATTN_WIDTH), row),
            pl.BlockSpec((tm, KV_COLS), row),
            pl.BlockSpec((KV_COLS, tm), col),
            pl.BlockSpec((tm, SSM_INNER), row),
            pl.BlockSpec((tm, XBC_COLS), row),
            pl.BlockSpec((DT_COLS, tm), col),
        ],
        out_shape=[
            jax.ShapeDtypeStruct((T, ATTN_WIDTH), BF16),
            jax.ShapeDtypeStruct((T, KV_COLS), BF16),
            jax.ShapeDtypeStruct((KV_COLS, T), BF16),
            jax.ShapeDtypeStruct((T, SSM_INNER), BF16),
            jax.ShapeDtypeStruct((T, XBC_COLS), BF16),
            jax.ShapeDtypeStruct((DT_COLS, T), F32),
        ],
        scratch_shapes=[pltpu.VMEM((tm + HALO, D_MODEL), BF16),
                        pltpu.VMEM((2, tm + HALO, CONV_COLS), F32),
                        pltpu.VMEM((tm, QKVZ_COLS + LANE), F32),
                        pltpu.VMEM((XBC_COLS // LANE, tm, LANE), F32)],
        compiler_params=pltpu.CompilerParams(
            dimension_semantics=("arbitrary",), vmem_limit_bytes=VMEM_LIMIT),
        name="in_proj",
    )(x2d, x2d, x2d, g1, w_qkvz, w_xbc, w_dt, conv_w, conv_b, *tables, _interleave_matrix(tm).astype(BF16))


def _attn_kernel(sink_ref, q_ref, kp_ref, km_ref, kn_ref, vp_ref, vm_ref, vn_ref, g_ref, o_ref,
                 k_s, vT_s, *, tq):
    i = pl.program_id(1)
    n = pl.num_programs(1)
    k_s[0:BLOCK, :] = kp_ref[...]
    k_s[BLOCK:BLOCK + tq, :] = km_ref[...]
    k_s[BLOCK + tq:, :] = kn_ref[...]
    vT_s[:, 0:BLOCK] = vp_ref[...]
    vT_s[:, BLOCK:BLOCK + tq] = vm_ref[...]
    vT_s[:, BLOCK + tq:] = vn_ref[...]

    band = 3 * BLOCK
    nq = GQA_GROUP * BLOCK
    key = lax.broadcasted_iota(jnp.int32, (BLOCK, nq), 0)
    qry = lax.broadcasted_iota(jnp.int32, (BLOCK, nq), 1) % BLOCK
    no_prev = jnp.where(i > 0, 0, BLOCK)
    no_next = jnp.where(i < n - 1, 0, BLOCK)
    ones = jnp.ones((BF16_ROWS, band), BF16)

    nsub = tq // BLOCK
    for j in range(nsub):
        prev_ok = key >= (qry + no_prev if j == 0 else qry)
        next_ok = key <= (qry - no_next if j == nsub - 1 else qry)
        qj = q_ref[j * BLOCK:(j + 1) * BLOCK, :]
        kj = k_s[j * BLOCK:j * BLOCK + band, :]
        vTj = vT_s[:, j * BLOCK:j * BLOCK + band]
        outs = []
        for kh in range(KV_HEADS):
            heads = [kh * GQA_GROUP + g for g in range(GQA_GROUP)]
            kk = kj[:, kh * HEAD_DIM:(kh + 1) * HEAD_DIM]
            qs = jnp.concatenate([qj[:, h * HEAD_DIM:(h + 1) * HEAD_DIM] for h in heads], axis=0)
            sT = lax.dot_general(kk, qs, (((1,), (1,)), ((), ())), preferred_element_type=F32)
            s_blocks = [jnp.where(prev_ok, sT[:BLOCK], NEG), sT[BLOCK:2 * BLOCK],
                        jnp.where(next_ok, sT[2 * BLOCK:], NEG)]
            sink = sink_ref[kh:kh + 1, :]
            s_max = jnp.maximum(jnp.maximum(s_blocks[0], s_blocks[1]), s_blocks[2])
            m = jnp.maximum(jnp.max(s_max, axis=0, keepdims=True), sink)
            p = jnp.concatenate([jnp.exp(sb - m) for sb in s_blocks], axis=0).astype(BF16)
            v_ext = jnp.concatenate([vTj[kh * HEAD_DIM:(kh + 1) * HEAD_DIM, :], ones], axis=0)
            o_ext = jnp.dot(v_ext, p, preferred_element_type=F32)
            denom = o_ext[HEAD_DIM:HEAD_DIM + 1, :] + jnp.exp(sink - m)
            oT = o_ext[:HEAD_DIM, :] / denom
            outs += [oT[:, g * BLOCK:(g + 1) * BLOCK] for g in range(GQA_GROUP)]
        oT_all = jnp.concatenate(outs, axis=0)
        inv = lax.rsqrt(jnp.mean(oT_all * oT_all, axis=0, keepdims=True) + EPS)
        o_ref[j * BLOCK:(j + 1) * BLOCK, :] = ((oT_all * inv).T * g_ref[...]).astype(BF16)


def _attention(q, k, vT, sink_rows, g, B, S, tq):
    T = q.shape[0]
    nq = S // tq
    sub = tq // BLOCK
    nblk = S // BLOCK
    main = lambda b, i: (b * nq + i, 0)
    prev = lambda b, i: (b * nblk + jnp.maximum(i * sub - 1, 0), 0)
    nxt = lambda b, i: (b * nblk + jnp.minimum((i + 1) * sub, nblk - 1), 0)
    swap = lambda f: (lambda b, i: f(b, i)[::-1])
    const = lambda b, i: (0, 0)
    return pl.pallas_call(
        functools.partial(_attn_kernel, tq=tq),
        grid=(B, nq),
        in_specs=[
            pl.BlockSpec((KV_HEADS, GQA_GROUP * BLOCK), const),
            pl.BlockSpec((tq, ATTN_WIDTH), main),
            pl.BlockSpec((BLOCK, KV_COLS), prev),
            pl.BlockSpec((tq, KV_COLS), main),
            pl.BlockSpec((BLOCK, KV_COLS), nxt),
            pl.BlockSpec((KV_COLS, BLOCK), swap(prev)),
            pl.BlockSpec((KV_COLS, tq), swap(main)),
            pl.BlockSpec((KV_COLS, BLOCK), swap(nxt)),
            pl.BlockSpec((1, ATTN_WIDTH), const),
        ],
        out_specs=pl.BlockSpec((tq, ATTN_WIDTH), main),
        out_shape=jax.ShapeDtypeStruct((T, ATTN_WIDTH), BF16),
        scratch_shapes=[pltpu.VMEM((tq + 2 * BLOCK, KV_COLS), BF16),
                        pltpu.VMEM((KV_COLS, tq + 2 * BLOCK), BF16)],
        compiler_params=pltpu.CompilerParams(
            dimension_semantics=("arbitrary", "arbitrary"), vmem_limit_bytes=VMEM_LIMIT),
        name="attention",
    )(sink_rows, q, k, k, k, vT, vT, vT, g)


def _split_terms(v):
    hi = v.astype(BF16).astype(F32)
    r1 = v - hi
    mid = r1.astype(BF16).astype(F32)
    lo = (r1 - mid).astype(BF16).astype(F32)
    return jnp.concatenate([hi, mid, lo], axis=0)


def _ssd_kernel(*refs, reverse, ts, tm):
    if reverse:
        (xc_ref, dtT_ref, biasT_ref, aT_ref, e_ref, z_ref, yf_ref, ng_ref,
         x_ref, attn_ref, wa_ref, ws_ref, g2_ref, perm_ref, x1_ref, hout_ref, h_s, o_ref) = refs
    else:
        (xc_ref, dtT_ref, biasT_ref, aT_ref, e_ref, dskip_ref, o_ref, h_s) = refs
    t = pl.program_id(1)
    d0 = SSM_HEADS if reverse else 0

    @pl.when(t == 0)
    def _():
        h_s[...] = jnp.zeros_like(h_s)

    li = lax.broadcasted_iota(jnp.int32, (CHUNK, CHUNK), 0)
    si = lax.broadcasted_iota(jnp.int32, (CHUNK, CHUNK), 1)
    causal = (li <= si) if reverse else (li >= si)
    scan_op = jnp.where((li >= si) if reverse else (li <= si), 1.0, 0.0).astype(BF16)
    last = 0 if reverse else CHUNK - 1
    nchunk = ts // CHUNK
    nterm = SPLIT_TERMS * DT_COLS

    def chunks_of(vT):
        return [vT[:, c * CHUNK:(c + 1) * CHUNK] for c in range(nchunk)]

    def sum_terms(m):
        return m[0:DT_COLS] + m[DT_COLS:2 * DT_COLS] + m[2 * DT_COLS:nterm]

    dtT_all = _softplus(dtT_ref[...] + biasT_ref[...])
    a_terms = jnp.concatenate([_split_terms(v) for v in chunks_of(dtT_all * aT_ref[...])], axis=0)
    cs = jnp.dot(a_terms.astype(BF16), scan_op, preferred_element_type=F32)
    dtT_c = chunks_of(dtT_all)
    acsT_c = [sum_terms(cs[c * nterm:(c + 1) * nterm]) for c in range(nchunk)]
    log2e = math.log2(math.e)
    tgt_c = [(v * log2e).T for v in acsT_c]
    src_c = [acsT_c[c] * log2e - jnp.log2(dtT_c[c]) for c in range(nchunk)]
    decayT_c = [dtT_c[c] * jnp.exp(acsT_c[c][:, last:last + 1] - acsT_c[c]) for c in range(nchunk)]

    def expand(vT_c):
        terms = jnp.concatenate([_split_terms(v).T for v in vT_c], axis=0)
        return jnp.dot(terms.astype(BF16), e_ref[...], preferred_element_type=F32)

    exp_acs_all = expand([jnp.exp(v) for v in acsT_c])
    decay_all = expand(decayT_c)

    order = range(nchunk - 1, -1, -1) if reverse else range(nchunk)
    for c in order:
        r0 = c * CHUNK
        xc = xc_ref[r0:r0 + CHUNK, :]
        x_b = xc[:, :SSM_INNER]
        xs = x_b.astype(F32)
        bm = xc[:, SSM_INNER:SSM_INNER + SSM_GROUPS * D_STATE]
        cm = xc[:, SSM_INNER + SSM_GROUPS * D_STATE:]

        tgt, src = tgt_c[c], src_c[c]
        exp_acs = exp_acs_all[r0:r0 + CHUNK, :]
        x_decay = (xs * decay_all[r0:r0 + CHUNK, :]).astype(BF16)
        chunk_decay = exp_acs[last:last + 1, :]

        ys = []
        for g in range(SSM_GROUPS):
            bg = bm[:, g * D_STATE:(g + 1) * D_STATE]
            cg = cm[:, g * D_STATE:(g + 1) * D_STATE]
            cb = lax.dot_general(cg, bg, (((1,), (1,)), ((), ())), preferred_element_type=F32)
            gs = slice(g * GROUP_W, (g + 1) * GROUP_W)
            bgT = bg.astype(F32).T.astype(BF16)
            st = jnp.dot(bgT, x_decay[:, gs], preferred_element_type=F32)
            h_prev = h_s[g]
            y_off = jnp.dot(cg, h_prev.astype(BF16), preferred_element_type=F32)
            h_s[g] = h_prev * chunk_decay[:, gs] + st
            y_diag = []
            for rr in range(SSM_HPG):
                r = g * SSM_HPG + rr
                seg = tgt[:, d0 + r:d0 + r + 1] - src[d0 + r:d0 + r + 1, :]
                lmat = jnp.exp2(jnp.where(causal, seg, -jnp.inf))
                mm = (cb * lmat).astype(BF16)
                y_diag.append(jnp.dot(mm, x_b[:, r * SSM_HEAD_DIM:(r + 1) * SSM_HEAD_DIM],
                                      preferred_element_type=F32))
            ys.append(jnp.concatenate(y_diag, axis=1) + y_off * exp_acs[:, gs])
        y = jnp.concatenate(ys, axis=1)

        if not reverse:
            o_ref[r0:r0 + CHUNK, :] = y + xs * dskip_ref[...]
        else:
            y = (y + yf_ref[r0:r0 + CHUNK, :]) * _silu(z_ref[r0:r0 + CHUNK, :].astype(F32))
            parts = []
            for g in range(SSM_GROUPS):
                yg = y[:, g * GROUP_W:(g + 1) * GROUP_W]
                parts.append(yg * lax.rsqrt(jnp.mean(yg * yg, axis=-1, keepdims=True) + EPS))
            o_ref[r0:r0 + CHUNK, :] = (jnp.concatenate(parts, axis=1) * ng_ref[...]).astype(BF16)

    if reverse:
        for k in range(ts // tm):
            rows = slice(k * tm, (k + 1) * tm)
            x1 = (x_ref[rows, :]
                  + jnp.dot(attn_ref[rows, :], wa_ref[...], preferred_element_type=F32)
                  + jnp.dot(o_ref[rows, :], ws_ref[...], preferred_element_type=F32))
            x1_ref[rows, :] = x1
            h = _rms(x1, g2_ref[...]).astype(BF16)
            hout_ref[rows, :] = jnp.dot(perm_ref[...], h, preferred_element_type=F32).astype(BF16)


def _ssd_pass(xc, dtT, biasT, aT, e, extra, B, S, ts, tm, reverse):
    T = xc.shape[0]
    nt = S // ts
    seq = (lambda t: nt - 1 - t) if reverse else (lambda t: t)
    main = lambda b, t: (b * nt + seq(t), 0)
    const = lambda b, t: (0, 0)
    state = pltpu.VMEM((SSM_GROUPS, D_STATE, GROUP_W), F32)
    in_specs = [
        pl.BlockSpec((ts, XBC_COLS), main),
        pl.BlockSpec((DT_COLS, ts), lambda b, t: (0, b * nt + seq(t))),
        pl.BlockSpec((DT_COLS, 1), const),
        pl.BlockSpec((DT_COLS, 1), const),
        pl.BlockSpec((SPLIT_TERMS * DT_COLS, SSM_INNER), const),
    ]
    args = [xc, dtT, biasT, aT, e]
    if reverse:
        z, yf, ng, x2d, attn, w_attn, w_ssd, g2 = extra
        in_specs += [pl.BlockSpec((ts, SSM_INNER), main), pl.BlockSpec((ts, SSM_INNER), main),
                     pl.BlockSpec((1, SSM_INNER), const),
                     pl.BlockSpec((ts, D_MODEL), main), pl.BlockSpec((ts, ATTN_WIDTH), main),
                     pl.BlockSpec((ATTN_WIDTH, D_MODEL), const), pl.BlockSpec((SSM_INNER, D_MODEL), const),
                     pl.BlockSpec((1, D_MODEL), const), pl.BlockSpec((tm, tm), const)]
        args += [z, yf, ng, x2d, attn, w_attn, w_ssd, g2, _interleave_matrix(tm).astype(BF16)]
        out_specs = [pl.BlockSpec((ts, D_MODEL), main), pl.BlockSpec((ts, D_MODEL), main)]
        out_shape = [jax.ShapeDtypeStruct((T, D_MODEL), F32), jax.ShapeDtypeStruct((T, D_MODEL), BF16)]
        scratch = [state, pltpu.VMEM((ts, SSM_INNER), BF16)]
    else:
        (dskip,) = extra
        in_specs += [pl.BlockSpec((1, SSM_INNER), const)]
        args += [dskip]
        out_specs = pl.BlockSpec((ts, SSM_INNER), main)
        out_shape = jax.ShapeDtypeStruct((T, SSM_INNER), F32)
        scratch = [state]
    return pl.pallas_call(
        functools.partial(_ssd_kernel, reverse=reverse, ts=ts, tm=tm),
        grid=(B, nt),
        in_specs=in_specs,
        out_specs=out_specs,
        out_shape=out_shape,
        scratch_shapes=scratch,
        compiler_params=pltpu.CompilerParams(
            dimension_semantics=("arbitrary", "arbitrary"), vmem_limit_bytes=VMEM_LIMIT),
        name="ssd_bwd" if reverse else "ssd_fwd",
    )(*args)


def _interleave_matrix(tm):
    nv = tm // SUBLANE
    t = jnp.arange(tm)
    return ((t % nv) * SUBLANE + t // nv)[None, :] == jnp.arange(tm)[:, None]


FFN_COLS = 256


def _ffn_kernel(hp_ref, hm_ref, hn_ref, x1_ref, wup_ref, cw_ref, cb_ref, wdn_ref, fg_ref, o_ref,
                hext_s, u_s, glu_s, d_s, *, tm, n_seq_tiles):
    tile = pl.program_id(0) % n_seq_tiles
    nv = tm // SUBLANE
    prev_row = tm + HALO - 1
    next_row = tm
    hext_s[0:tm, :] = hm_ref[...]
    before = jnp.where(tile > 0, hp_ref[...], jnp.zeros_like(hp_ref))
    after = jnp.where(tile < n_seq_tiles - 1, hn_ref[...], jnp.zeros_like(hn_ref))
    halo_row = lax.broadcasted_iota(jnp.int32, (HALO, D_MODEL), 0)
    hext_s[tm:, :] = jnp.where(halo_row >= HALO // 2, before, after)
    f = FFN_COLS
    nchunks = D_FF // f

    def up(c):
        hext = hext_s[...]
        for half, base in enumerate((0, D_FF)):
            u_s[c % 2, :, half * f:(half + 1) * f] = jnp.dot(
                hext, wup_ref[:, base + c * f:base + (c + 1) * f], preferred_element_type=F32)

    def conv(c, half, base):
        cols = slice(base + c * f, base + (c + 1) * f)
        lanes = slice(half * f, (half + 1) * f)
        u = lambda lo, hi: u_s[c % 2, lo:hi, lanes]
        first = jnp.concatenate([u(prev_row, prev_row + 1), u(tm - SUBLANE, tm - 1)], axis=0)
        last = jnp.concatenate([u(1, SUBLANE), u(next_row, next_row + 1)], axis=0)
        before = jnp.concatenate([first, u(0, tm - SUBLANE)], axis=0)
        after = jnp.concatenate([u(SUBLANE, tm), last], axis=0)
        return (cb_ref[:, cols] + before * cw_ref[0:1, cols] + u(0, tm) * cw_ref[1:2, cols]
                + after * cw_ref[2:3, cols])

    up(0)
    for c in range(nchunks):
        if c + 1 < nchunks:
            up(c + 1)
        glu_s[:, c * f:(c + 1) * f] = (_silu(conv(c, 0, 0)) * conv(c, 1, D_FF)).astype(BF16)
    _store_lane_blocks(d_s, jnp.dot(glu_s[...], wdn_ref[...], preferred_element_type=F32))
    for s in range(SUBLANE):
        rows = slice(s * nv, (s + 1) * nv)
        x2 = x1_ref[rows, :] + _load_strided_rows(d_s, s, nv, SUBLANE)
        o_ref[rows, :] = _rms(x2, fg_ref[...])


def _ffn(h2, x1, w_up, conv_w, conv_b, w_down, fg, S, tm):
    T = h2.shape[0]
    n_seq_tiles = S // tm
    hb = tm // HALO
    nh = T // HALO
    row = lambda i: (i, 0)
    const = lambda i: (0, 0)
    prev = lambda i: (jnp.maximum(i * hb - 1, 0), 0)
    nxt = lambda i: (jnp.minimum((i + 1) * hb, nh - 1), 0)
    return pl.pallas_call(
        functools.partial(_ffn_kernel, tm=tm, n_seq_tiles=n_seq_tiles),
        grid=(T // tm,),
        in_specs=[
            pl.BlockSpec((HALO, D_MODEL), prev),
            pl.BlockSpec((tm, D_MODEL), row),
            pl.BlockSpec((HALO, D_MODEL), nxt),
            pl.BlockSpec((tm, D_MODEL), row),
            pl.BlockSpec((D_MODEL, 2 * D_FF), const),
            pl.BlockSpec((FFN_CONV, 2 * D_FF), const),
            pl.BlockSpec((1, 2 * D_FF), const),
            pl.BlockSpec((D_FF, D_MODEL), const),
            pl.BlockSpec((1, D_MODEL), const),
        ],
        out_specs=pl.BlockSpec((tm, D_MODEL), row),
        out_shape=jax.ShapeDtypeStruct((T, D_MODEL), F32),
        scratch_shapes=[pltpu.VMEM((tm + HALO, D_MODEL), BF16),
                        pltpu.VMEM((2, tm + HALO, 2 * FFN_COLS), F32),
                        pltpu.VMEM((tm, D_FF), BF16),
                        pltpu.VMEM((D_MODEL // LANE, tm, LANE), F32)],
        compiler_params=pltpu.CompilerParams(
            dimension_semantics=("arbitrary",), vmem_limit_bytes=VMEM_LIMIT),
        name="ffn",
    )(h2, h2, h2, x1, w_up, conv_w, conv_b, w_down, fg)


def _layer(x2d, B, S, norm1_g, w_in, attn_sink, attn_out_g, ssd_conv_w, ssd_conv_b,
           dt_bias_f, dt_bias_b, a_log_f, a_log_b, ssd_d, ssd_norm_g, w_out, norm2_g,
           w_up, ffn_conv_w, ffn_conv_b, w_down, out_g, tables, tm, ts, tq):
    row = lambda v: v.reshape(1, -1).astype(F32)
    w_qkvz = w_in[:, :QKVZ_COLS].astype(BF16)
    w_xbc = w_in[:, QKVZ_COLS:QKVZ_COLS + XBC_COLS].astype(BF16)
    w_dt = jnp.pad(w_in[:, QKVZ_COLS + XBC_COLS:], ((0, 0), (0, LANE - DT_COLS))).astype(BF16)
    q, k, vT, z, xc, dtT = _in_proj(x2d, row(norm1_g), w_qkvz, w_xbc, w_dt,
                                    ssd_conv_w.astype(F32), row(ssd_conv_b), tables, S, tm)

    sink_rows = jnp.repeat(attn_sink.astype(F32), BLOCK).reshape(KV_HEADS, GQA_GROUP * BLOCK)
    attn = _attention(q, k, vT, sink_rows, row(attn_out_g), B, S, tq)

    bias = jnp.concatenate([dt_bias_f, dt_bias_b]).astype(F32)
    a = -jnp.exp(jnp.concatenate([a_log_f, a_log_b]).astype(F32))
    head_of_col = jnp.arange(SSM_INNER) // SSM_HEAD_DIM
    w_o = w_out.astype(BF16)
    out = None
    for reverse in (False, True):
        d0 = SSM_HEADS if reverse else 0
        term_row = jnp.arange(SPLIT_TERMS * DT_COLS) % DT_COLS
        e = (term_row[:, None] == (head_of_col + d0)[None, :]).astype(BF16)
        if reverse:
            extra = (z, out, row(ssd_norm_g), x2d, attn, w_o[:ATTN_WIDTH], w_o[ATTN_WIDTH:], row(norm2_g))
        else:
            extra = (row(jnp.repeat(ssd_d.astype(F32), SSM_HEAD_DIM)),)
        out = _ssd_pass(xc, dtT, bias.reshape(-1, 1), a.reshape(-1, 1), e, extra, B, S, ts, tm, reverse)
    x1, h2 = out
    return _ffn(h2, x1, w_up.astype(BF16), ffn_conv_w.astype(F32), row(ffn_conv_b),
                w_down.astype(BF16), out_g, S, tm)


def kernel(x, norm1_g, w_in, attn_sink, attn_out_g, ssd_conv_w, ssd_conv_b, ssd_dt_bias_fwd,
           ssd_dt_bias_bwd, ssd_a_log_fwd, ssd_a_log_bwd, ssd_d, ssd_norm_g, w_out, norm2_g, w_up,
           ffn_conv_w, ffn_conv_b, w_down, final_norm_g, *, tm=512, ts=1024, tq=1024):
    B, S, _ = x.shape
    depth = w_in.shape[0]
    assert depth == 1, "the fused final RMSNorm assumes a single layer"
    tables = _rope_tables(S)
    x2d = x.reshape(B * S, D_MODEL)
    out = _layer(x2d, B, S, norm1_g[0], w_in[0], attn_sink[0], attn_out_g[0], ssd_conv_w[0], ssd_conv_b[0],
                 ssd_dt_bias_fwd[0], ssd_dt_bias_bwd[0], ssd_a_log_fwd[0], ssd_a_log_bwd[0], ssd_d[0],
                 ssd_norm_g[0], w_out[0], norm2_g[0], w_up[0], ffn_conv_w[0], ffn_conv_b[0], w_down[0],
                 final_norm_g.reshape(1, -1).astype(F32), tables, tm, ts, tq)
    return out.reshape(B, S, D_MODEL)
```

```python
import functools
import math

import jax
import jax.numpy as jnp
from jax import lax
from jax.experimental import pallas as pl
from jax.experimental.pallas import tpu as pltpu

D_MODEL = 1024
HEAD_DIM = 64
ATTN_WIDTH = 512
ATTN_HEADS = 8
KV_HEADS = 2
GQA_GROUP = 4
ROT_DIM = 16
ROPE_THETA = 500000.0
WINDOW = 128
BLOCK = 128
SSM_INNER = 512
SSM_HEAD_DIM = 64
SSM_HEADS = 8
SSM_GROUPS = 2
SSM_HPG = 4
D_STATE = 128
SSM_CONV = 5
CHUNK = 128
D_FF = 2816
FFN_CONV = 3
EPS = 1e-5
NEG = -1e30
LOG2E = math.log2(math.e)

KV_COLS = KV_HEADS * HEAD_DIM
XBC_COLS = SSM_INNER + 2 * SSM_GROUPS * D_STATE
QKVZ_COLS = ATTN_WIDTH + 2 * KV_COLS + SSM_INNER
DT_COLS = 2 * SSM_HEADS
SPLIT_TERMS = 3
GROUP_W = SSM_HPG * SSM_HEAD_DIM

LANE = 128
SUBLANE = 8
BF16_ROWS = 16
HALO = BF16_ROWS
assert WINDOW == BLOCK, "the banded attention masks assume one key block of reach on each side"
CONV_COLS = 256
VMEM_LIMIT = 56 * 1024 * 1024

F32 = jnp.float32
BF16 = jnp.bfloat16


def _rms(x, g):
    return x * lax.rsqrt(jnp.mean(x * x, axis=-1, keepdims=True) + EPS) * g


def _silu(x):
    return x * (1.0 / (1.0 + jnp.exp(-x)))


def _softplus(x):
    return jnp.maximum(x, 0.0) + jnp.log1p(jnp.exp(-jnp.abs(x)))


def _interleave_tokens(x):
    tm, cols = x.shape
    return jnp.swapaxes(x.reshape(SUBLANE, tm // SUBLANE, cols), 0, 1).reshape(tm, cols)


def _store_lane_blocks(ref, val):
    for j in range(ref.shape[0]):
        ref[j] = val[:, j * LANE:(j + 1) * LANE]


def _load_strided_rows(ref, start, size, stride):
    return jnp.concatenate([ref[j, pl.ds(start, size, stride=stride), :] for j in range(ref.shape[0])], axis=1)


def _inproj_kernel(xp_ref, x_ref, xn_ref, g_ref, w_ref, wx_ref, wdt_ref, cw_ref, cb_ref,
                   cos_ref, sa_ref, sb_ref,
                   q_ref, k_ref, vT_ref, z_ref, xc_ref, dtT_ref, hext_s, xbc_s, proj_s, xcp_s,
                   *, tm, n_seq_tiles):
    tile = pl.program_id(0) % n_seq_tiles
    nv = tm // SUBLANE
    norm = lambda r: _rms(r[...], g_ref[...]).astype(BF16)
    hf = _rms(x_ref[...], g_ref[...])
    hb = hf.astype(BF16)
    hext_s[0:tm, :] = _interleave_tokens(hf).astype(BF16)
    before = jnp.where(tile > 0, norm(xp_ref), jnp.zeros((HALO, D_MODEL), BF16))
    after = jnp.where(tile < n_seq_tiles - 1, norm(xn_ref), jnp.zeros((HALO, D_MODEL), BF16))
    halo_row = lax.broadcasted_iota(jnp.int32, (HALO, D_MODEL), 0)
    hext_s[tm:, :] = jnp.where(halo_row >= HALO // 2, before, after)

    f = CONV_COLS
    nchunks = XBC_COLS // f

    def xbc_proj(c):
        xbc_s[c % 2] = jnp.dot(hext_s[...], wx_ref[:, c * f:(c + 1) * f], preferred_element_type=F32)

    def conv(c):
        cols = slice(c * f, (c + 1) * f)
        u = lambda lo, hi: xbc_s[c % 2, lo:hi, :]
        tok = lambda t: u(tm + t % HALO, tm + t % HALO + 1)
        cat = lambda *parts: jnp.concatenate(parts, axis=0)
        wrap_m1 = cat(tok(-1), u(tm - SUBLANE, tm - 1))
        wrap_m2 = cat(tok(-2), u(tm - 2 * SUBLANE, tm - SUBLANE - 1))
        wrap_p1 = cat(u(1, SUBLANE), tok(0))
        wrap_p2 = cat(u(SUBLANE + 1, 2 * SUBLANE), tok(1))
        taps = [cat(wrap_m2, wrap_m1, u(0, tm - 2 * SUBLANE)),
                cat(wrap_m1, u(0, tm - SUBLANE)),
                u(0, tm),
                cat(u(SUBLANE, tm), wrap_p1),
                cat(u(2 * SUBLANE, tm), wrap_p1, wrap_p2)]
        acc = cb_ref[:, cols]
        for kk in range(SSM_CONV):
            acc = acc + taps[kk] * cw_ref[kk:kk + 1, cols]
        act = _silu(acc)
        for j in range(f // LANE):
            xcp_s[c * (f // LANE) + j] = act[:, j * LANE:(j + 1) * LANE]

    xbc_proj(0)
    proj_s[:, :QKVZ_COLS] = jnp.dot(hb, w_ref[...], preferred_element_type=F32)
    proj_s[:, QKVZ_COLS:] = jnp.dot(hb, wdt_ref[...], preferred_element_type=F32)
    for c in range(nchunks - 1):
        xbc_proj(c + 1)
        conv(c)
    conv(nchunks - 1)
    for s in range(SUBLANE):
        xc_ref[s * nv:(s + 1) * nv, :] = _load_strided_rows(xcp_s, s, nv, SUBLANE).astype(BF16)
    proj = proj_s[:, :QKVZ_COLS]
    dt = proj_s[:, QKVZ_COLS:]
    c, sa, sb = cos_ref[...], sa_ref[...], sb_ref[...]

    def rope(t):
        n = t.shape[1]
        reps = n // LANE
        cc = jnp.concatenate([c] * reps, axis=1) if reps > 1 else c
        aa = jnp.concatenate([sa] * reps, axis=1) if reps > 1 else sa
        bb = jnp.concatenate([sb] * reps, axis=1) if reps > 1 else sb
        half = ROT_DIM // 2
        return t * cc + pltpu.roll(t, n - half, 1) * aa + pltpu.roll(t, half, 1) * bb

    o = 0
    q = rope(proj[:, o:o + ATTN_WIDTH]) * (LOG2E / math.sqrt(HEAD_DIM))
    o += ATTN_WIDTH
    k = rope(proj[:, o:o + KV_COLS])
    o += KV_COLS
    v = proj[:, o:o + KV_COLS]
    o += KV_COLS
    q_ref[...] = q.astype(BF16)
    k_ref[...] = k.astype(BF16)
    vT_ref[...] = v.T.astype(BF16)
    z_ref[...] = proj[:, o:o + SSM_INNER].astype(BF16)
    dtT_ref[...] = dt.T[:DT_COLS, :]


def _rope_tables(S):
    half = ROT_DIM // 2
    pos = jnp.arange(S, dtype=F32)
    inv = ROPE_THETA ** (-jnp.arange(0, ROT_DIM, 2, dtype=F32) / ROT_DIM)
    ang = pos[:, None] * inv[None, :]
    cos, sin = jnp.cos(ang), jnp.sin(ang)
    rest = HEAD_DIM - ROT_DIM
    ones, zeros, zh = jnp.ones((S, rest), F32), jnp.zeros((S, rest), F32), jnp.zeros((S, half), F32)
    c = jnp.concatenate([cos, cos, ones], axis=1)
    sa = jnp.concatenate([-sin, zh, zeros], axis=1)
    sb = jnp.concatenate([zh, sin, zeros], axis=1)
    rep = LANE // HEAD_DIM
    return tuple(jnp.tile(t, (1, rep)) for t in (c, sa, sb))


def _in_proj(x2d, g1, w_qkvz, w_xbc, w_dt, conv_w, conv_b, tables, S, tm):
    T = x2d.shape[0]
    n_seq_tiles = S // tm
    hb = tm // HALO
    nh = T // HALO
    row = lambda i: (i, 0)
    col = lambda i: (0, i)
    const = lambda i: (0, 0)
    prev = lambda i: (jnp.maximum(i * hb - 1, 0), 0)
    nxt = lambda i: (jnp.minimum((i + 1) * hb, nh - 1), 0)
    tab = lambda i: (i % n_seq_tiles, 0)
    return pl.pallas_call(
        functools.partial(_inproj_kernel, tm=tm, n_seq_tiles=n_seq_tiles),
        grid=(T // tm,),
        in_specs=[
            pl.BlockSpec((HALO, D_MODEL), prev),
            pl.BlockSpec((tm, D_MODEL), row),
            pl.BlockSpec((HALO, D_MODEL), nxt),
            pl.BlockSpec((1, D_MODEL), const),
            pl.BlockSpec((D_MODEL, QKVZ_COLS), const),
            pl.BlockSpec((D_MODEL, XBC_COLS), const),
            pl.BlockSpec((D_MODEL, LANE), const),
            pl.BlockSpec((SSM_CONV, XBC_COLS), const),
            pl.BlockSpec((1, XBC_COLS), const),
            pl.BlockSpec((tm, LANE), tab),
            pl.BlockSpec((tm, LANE), tab),
            pl.BlockSpec((tm, LANE), tab),
        ],
        out_specs=[
            pl.BlockSpec((tm, ATTN_WIDTH), row),
            pl.BlockSpec((tm, KV_COLS), row),
            pl.BlockSpec((KV_COLS, tm), col),
            pl.BlockSpec((tm, SSM_INNER), row),
            pl.BlockSpec((tm, XBC_COLS), row),
            pl.BlockSpec((DT_COLS, tm), col),
        ],
        out_shape=[
            jax.ShapeDtypeStruct((T, ATTN_WIDTH), BF16),
            jax.ShapeDtypeStruct((T, KV_COLS), BF16),
            jax.ShapeDtypeStruct((KV_COLS, T), BF16),
            jax.ShapeDtypeStruct((T, SSM_INNER), BF16),
            jax.ShapeDtypeStruct((T, XBC_COLS), BF16),
            jax.ShapeDtypeStruct((DT_COLS, T), F32),
        ],
        scratch_shapes=[pltpu.VMEM((tm + HALO, D_MODEL), BF16),
                        pltpu.VMEM((2, tm + HALO, CONV_COLS), F32),
                        pltpu.VMEM((tm, QKVZ_COLS + LANE), F32),
                        pltpu.VMEM((XBC_COLS // LANE, tm, LANE), F32)],
        compiler_params=pltpu.CompilerParams(
            dimension_semantics=("arbitrary",), vmem_limit_bytes=VMEM_LIMIT),
        name="in_proj",
    )(x2d, x2d, x2d, g1, w_qkvz, w_xbc, w_dt, conv_w, conv_b, *tables)


def _attn_kernel(sink_ref, q_ref, kp_ref, km_ref, kn_ref, vp_ref, vm_ref, vn_ref, g_ref, o_ref,
                 k_s, vT_s, sT_s, *, tq):
    i = pl.program_id(1)
    n = pl.num_programs(1)
    k_s[0:BLOCK, :] = kp_ref[...]
    k_s[BLOCK:BLOCK + tq, :] = km_ref[...]
    k_s[BLOCK + tq:, :] = kn_ref[...]
    vT_s[:, 0:BLOCK] = vp_ref[...]
    vT_s[:, BLOCK:BLOCK + tq] = vm_ref[...]
    vT_s[:, BLOCK + tq:] = vn_ref[...]

    band = 3 * BLOCK
    nq = GQA_GROUP * BLOCK
    key = lax.broadcasted_iota(jnp.int32, (BLOCK, nq), 0)
    qry = lax.broadcasted_iota(jnp.int32, (BLOCK, nq), 1) % BLOCK
    no_prev = jnp.where(i > 0, 0, BLOCK)
    no_next = jnp.where(i < n - 1, 0, BLOCK)
    cap = lambda ok: jnp.where(ok, jnp.inf, NEG).astype(F32)
    cap_prev, cap_prev_edge = cap(key >= qry), cap(key >= qry + no_prev)
    cap_next, cap_next_edge = cap(key <= qry), cap(key <= qry - no_next)
    ones = jnp.ones((BF16_ROWS, band), BF16)

    nsub = tq // BLOCK
    units = [(j, kh) for j in range(nsub) for kh in range(KV_HEADS)]

    def scores(u):
        j, kh = units[u]
        qj = q_ref[j * BLOCK:(j + 1) * BLOCK, :]
        kk = k_s[j * BLOCK:j * BLOCK + band, kh * HEAD_DIM:(kh + 1) * HEAD_DIM]
        qs = jnp.concatenate([qj[:, (kh * GQA_GROUP + g) * HEAD_DIM:(kh * GQA_GROUP + g + 1) * HEAD_DIM]
                              for g in range(GQA_GROUP)], axis=0)
        sT_s[u % 2] = lax.dot_general(kk, qs, (((1,), (1,)), ((), ())), preferred_element_type=F32)

    scores(0)
    for j in range(nsub):
        prev_cap = cap_prev_edge if j == 0 else cap_prev
        next_cap = cap_next_edge if j == nsub - 1 else cap_next
        vTj = vT_s[:, j * BLOCK:j * BLOCK + band]
        outs = []
        for kh in range(KV_HEADS):
            u = j * KV_HEADS + kh
            if u + 1 < len(units):
                scores(u + 1)
            s_blocks = [jnp.minimum(sT_s[u % 2, :BLOCK, :], prev_cap), sT_s[u % 2, BLOCK:2 * BLOCK, :],
                        jnp.minimum(sT_s[u % 2, 2 * BLOCK:, :], next_cap)]
            sink = sink_ref[kh:kh + 1, :]
            s_max = jnp.maximum(jnp.maximum(s_blocks[0], s_blocks[1]), s_blocks[2])
            m = jnp.maximum(jnp.max(s_max, axis=0, keepdims=True), sink)
            p = jnp.concatenate([jnp.exp2(sb - m) for sb in s_blocks], axis=0).astype(BF16)
            v_ext = jnp.concatenate([vTj[kh * HEAD_DIM:(kh + 1) * HEAD_DIM, :], ones], axis=0)
            o_ext = jnp.dot(v_ext, p, preferred_element_type=F32)
            denom = o_ext[HEAD_DIM:HEAD_DIM + 1, :] + jnp.exp2(sink - m)
            oT = o_ext[:HEAD_DIM, :] / denom
            outs += [oT[:, g * BLOCK:(g + 1) * BLOCK] for g in range(GQA_GROUP)]
        oT_all = jnp.concatenate(outs, axis=0)
        inv = lax.rsqrt(jnp.mean(oT_all * oT_all, axis=0, keepdims=True) + EPS)
        o_ref[j * BLOCK:(j + 1) * BLOCK, :] = ((oT_all * inv).T * g_ref[...]).astype(BF16)


def _attention(q, k, vT, sink_rows, g, B, S, tq):
    T = q.shape[0]
    nq = S // tq
    sub = tq // BLOCK
    nblk = S // BLOCK
    main = lambda b, i: (b * nq + i, 0)
    prev = lambda b, i: (b * nblk + jnp.maximum(i * sub - 1, 0), 0)
    nxt = lambda b, i: (b * nblk + jnp.minimum((i + 1) * sub, nblk - 1), 0)
    swap = lambda f: (lambda b, i: f(b, i)[::-1])
    const = lambda b, i: (0, 0)
    return pl.pallas_call(
        functools.partial(_attn_kernel, tq=tq),
        grid=(B, nq),
        in_specs=[
            pl.BlockSpec((KV_HEADS, GQA_GROUP * BLOCK), const),
            pl.BlockSpec((tq, ATTN_WIDTH), main),
            pl.BlockSpec((BLOCK, KV_COLS), prev),
            pl.BlockSpec((tq, KV_COLS), main),
            pl.BlockSpec((BLOCK, KV_COLS), nxt),
            pl.BlockSpec((KV_COLS, BLOCK), swap(prev)),
            pl.BlockSpec((KV_COLS, tq), swap(main)),
            pl.BlockSpec((KV_COLS, BLOCK), swap(nxt)),
            pl.BlockSpec((1, ATTN_WIDTH), const),
        ],
        out_specs=pl.BlockSpec((tq, ATTN_WIDTH), main),
        out_shape=jax.ShapeDtypeStruct((T, ATTN_WIDTH), BF16),
        scratch_shapes=[pltpu.VMEM((tq + 2 * BLOCK, KV_COLS), BF16),
                        pltpu.VMEM((KV_COLS, tq + 2 * BLOCK), BF16),
                        pltpu.VMEM((2, 3 * BLOCK, GQA_GROUP * BLOCK), F32)],
        compiler_params=pltpu.CompilerParams(
            dimension_semantics=("arbitrary", "arbitrary"), vmem_limit_bytes=VMEM_LIMIT),
        name="attention",
    )(sink_rows, q, k, k, k, vT, vT, vT, g)


def _split_terms(v):
    hi = v.astype(BF16).astype(F32)
    r1 = v - hi
    mid = r1.astype(BF16).astype(F32)
    lo = (r1 - mid).astype(BF16).astype(F32)
    return jnp.concatenate([hi, mid, lo], axis=0)


def _ssd_kernel(*refs, reverse, ts, tm):
    if reverse:
        (xc_ref, dtT_ref, biasT_ref, aT_ref, e_ref, z_ref, yf_ref, ng_ref,
         x_ref, attn_ref, wa_ref, ws_ref, g2_ref, perm_ref, x1_ref, hout_ref, h_s, o_ref) = refs
    else:
        (xc_ref, dtT_ref, biasT_ref, aT_ref, e_ref, dskip_ref, o_ref, h_s) = refs
    t = pl.program_id(1)
    d0 = SSM_HEADS if reverse else 0

    @pl.when(t == 0)
    def _():
        h_s[...] = jnp.zeros_like(h_s)

    li = lax.broadcasted_iota(jnp.int32, (CHUNK, CHUNK), 0)
    si = lax.broadcasted_iota(jnp.int32, (CHUNK, CHUNK), 1)
    causal = (li <= si) if reverse else (li >= si)
    scan_op = jnp.where((li >= si) if reverse else (li <= si), 1.0, 0.0).astype(BF16)
    last = 0 if reverse else CHUNK - 1
    nchunk = ts // CHUNK
    nterm = SPLIT_TERMS * DT_COLS

    def chunks_of(vT):
        return [vT[:, c * CHUNK:(c + 1) * CHUNK] for c in range(nchunk)]

    def sum_terms(m):
        return m[0:DT_COLS] + m[DT_COLS:2 * DT_COLS] + m[2 * DT_COLS:nterm]

    dtT_all = _softplus(dtT_ref[...] + biasT_ref[...])
    a_terms = jnp.concatenate([_split_terms(v) for v in chunks_of(dtT_all * aT_ref[...])], axis=0)
    cs = jnp.dot(a_terms.astype(BF16), scan_op, preferred_element_type=F32)
    dtT_c = chunks_of(dtT_all)
    acsT_c = [sum_terms(cs[c * nterm:(c + 1) * nterm]) for c in range(nchunk)]
    tgt_c = [(v * LOG2E).T for v in acsT_c]
    src_c = [acsT_c[c] * LOG2E - jnp.log2(dtT_c[c]) for c in range(nchunk)]
    decayT_c = [dtT_c[c] * jnp.exp(acsT_c[c][:, last:last + 1] - acsT_c[c]) for c in range(nchunk)]

    def expand(vT_c):
        terms = jnp.concatenate([_split_terms(v).T for v in vT_c], axis=0)
        return jnp.dot(terms.astype(BF16), e_ref[...], preferred_element_type=F32)

    exp_acs_all = expand([jnp.exp(v) for v in acsT_c])
    decay_all = expand(decayT_c)

    def project_block(k):
        rows = slice(k * tm, (k + 1) * tm)
        x1 = (x_ref[rows, :]
              + jnp.dot(attn_ref[rows, :], wa_ref[...], preferred_element_type=F32)
              + jnp.dot(o_ref[rows, :], ws_ref[...], preferred_element_type=F32))
        x1_ref[rows, :] = x1
        h = _rms(x1, g2_ref[...]).astype(BF16)
        hout_ref[rows, :] = jnp.dot(perm_ref[...], h, preferred_element_type=F32).astype(BF16)

    order = range(nchunk - 1, -1, -1) if reverse else range(nchunk)
    for c in order:
        r0 = c * CHUNK
        xc = xc_ref[r0:r0 + CHUNK, :]
        x_b = xc[:, :SSM_INNER]
        xs = x_b.astype(F32)
        bm = xc[:, SSM_INNER:SSM_INNER + SSM_GROUPS * D_STATE]
        cm = xc[:, SSM_INNER + SSM_GROUPS * D_STATE:]

        tgt, src = tgt_c[c], src_c[c]
        exp_acs = exp_acs_all[r0:r0 + CHUNK, :]
        x_decay = (xs * decay_all[r0:r0 + CHUNK, :]).astype(BF16)
        chunk_decay = exp_acs[last:last + 1, :]

        ys = []
        for g in range(SSM_GROUPS):
            bg = bm[:, g * D_STATE:(g + 1) * D_STATE]
            cg = cm[:, g * D_STATE:(g + 1) * D_STATE]
            cb = lax.dot_general(cg, bg, (((1,), (1,)), ((), ())), preferred_element_type=F32)
            gs = slice(g * GROUP_W, (g + 1) * GROUP_W)
            bgT = bg.astype(F32).T.astype(BF16)
            st = jnp.dot(bgT, x_decay[:, gs], preferred_element_type=F32)
            h_prev = h_s[g]
            y_off = jnp.dot(cg, h_prev.astype(BF16), preferred_element_type=F32)
            h_s[g] = h_prev * chunk_decay[:, gs] + st
            y_diag = []
            for rr in range(SSM_HPG):
                r = g * SSM_HPG + rr
                seg = tgt[:, d0 + r:d0 + r + 1] - src[d0 + r:d0 + r + 1, :]
                lmat = jnp.exp2(jnp.where(causal, seg, -jnp.inf))
                mm = (cb * lmat).astype(BF16)
                y_diag.append(jnp.dot(mm, x_b[:, r * SSM_HEAD_DIM:(r + 1) * SSM_HEAD_DIM],
                                      preferred_element_type=F32))
            ys.append(jnp.concatenate(y_diag, axis=1) + y_off * exp_acs[:, gs])
        y = jnp.concatenate(ys, axis=1)

        if not reverse:
            o_ref[r0:r0 + CHUNK, :] = y + xs * dskip_ref[...]
        else:
            y = (y + yf_ref[r0:r0 + CHUNK, :]) * _silu(z_ref[r0:r0 + CHUNK, :].astype(F32))
            parts = []
            for g in range(SSM_GROUPS):
                yg = y[:, g * GROUP_W:(g + 1) * GROUP_W]
                parts.append(yg * lax.rsqrt(jnp.mean(yg * yg, axis=-1, keepdims=True) + EPS))
            o_ref[r0:r0 + CHUNK, :] = (jnp.concatenate(parts, axis=1) * ng_ref[...]).astype(BF16)

            if r0 % tm == 0:
                project_block(r0 // tm)


def _ssd_pass(xc, dtT, biasT, aT, e, extra, B, S, ts, tm, reverse):
    T = xc.shape[0]
    nt = S // ts
    seq = (lambda t: nt - 1 - t) if reverse else (lambda t: t)
    main = lambda b, t: (b * nt + seq(t), 0)
    const = lambda b, t: (0, 0)
    state = pltpu.VMEM((SSM_GROUPS, D_STATE, GROUP_W), F32)
    in_specs = [
        pl.BlockSpec((ts, XBC_COLS), main),
        pl.BlockSpec((DT_COLS, ts), lambda b, t: (0, b * nt + seq(t))),
        pl.BlockSpec((DT_COLS, 1), const),
        pl.BlockSpec((DT_COLS, 1), const),
        pl.BlockSpec((SPLIT_TERMS * DT_COLS, SSM_INNER), const),
    ]
    args = [xc, dtT, biasT, aT, e]
    if reverse:
        z, yf, ng, x2d, attn, w_attn, w_ssd, g2 = extra
        in_specs += [pl.BlockSpec((ts, SSM_INNER), main), pl.BlockSpec((ts, SSM_INNER), main),
                     pl.BlockSpec((1, SSM_INNER), const),
                     pl.BlockSpec((ts, D_MODEL), main), pl.BlockSpec((ts, ATTN_WIDTH), main),
                     pl.BlockSpec((ATTN_WIDTH, D_MODEL), const), pl.BlockSpec((SSM_INNER, D_MODEL), const),
                     pl.BlockSpec((1, D_MODEL), const), pl.BlockSpec((tm, tm), const)]
        perm = _interleave_tokens(jnp.eye(tm, dtype=F32)).astype(BF16)
        args += [z, yf, ng, x2d, attn, w_attn, w_ssd, g2, perm]
        out_specs = [pl.BlockSpec((ts, D_MODEL), main), pl.BlockSpec((ts, D_MODEL), main)]
        out_shape = [jax.ShapeDtypeStruct((T, D_MODEL), F32), jax.ShapeDtypeStruct((T, D_MODEL), BF16)]
        scratch = [state, pltpu.VMEM((ts, SSM_INNER), BF16)]
    else:
        (dskip,) = extra
        in_specs += [pl.BlockSpec((1, SSM_INNER), const)]
        args += [dskip]
        out_specs = pl.BlockSpec((ts, SSM_INNER), main)
        out_shape = jax.ShapeDtypeStruct((T, SSM_INNER), F32)
        scratch = [state]
    return pl.pallas_call(
        functools.partial(_ssd_kernel, reverse=reverse, ts=ts, tm=tm),
        grid=(B, nt),
        in_specs=in_specs,
        out_specs=out_specs,
        out_shape=out_shape,
        scratch_shapes=scratch,
        compiler_params=pltpu.CompilerParams(
            dimension_semantics=("arbitrary", "arbitrary"), vmem_limit_bytes=VMEM_LIMIT),
        name="ssd_bwd" if reverse else "ssd_fwd",
    )(*args)


FFN_COLS = 256


def _ffn_kernel(hp_ref, hm_ref, hn_ref, x1_ref, wup_ref, cw_ref, cb_ref, wdn_ref, fg_ref, o_ref,
                hext_s, u_s, glu_s, d_s, *, tm, n_seq_tiles):
    tile = pl.program_id(0) % n_seq_tiles
    nv = tm // SUBLANE
    prev_row = tm + HALO - 1
    next_row = tm
    hext_s[0:tm, :] = hm_ref[...]
    before = jnp.where(tile > 0, hp_ref[...], jnp.zeros_like(hp_ref))
    after = jnp.where(tile < n_seq_tiles - 1, hn_ref[...], jnp.zeros_like(hn_ref))
    halo_row = lax.broadcasted_iota(jnp.int32, (HALO, D_MODEL), 0)
    hext_s[tm:, :] = jnp.where(halo_row >= HALO // 2, before, after)
    f = FFN_COLS
    nchunks = D_FF // f

    def up(c):
        hext = hext_s[...]
        for half, base in enumerate((0, D_FF)):
            u_s[c % 2, :, half * f:(half + 1) * f] = jnp.dot(
                hext, wup_ref[:, base + c * f:base + (c + 1) * f], preferred_element_type=F32)

    def conv(c, half, base):
        cols = slice(base + c * f, base + (c + 1) * f)
        lanes = slice(half * f, (half + 1) * f)
        u = lambda lo, hi: u_s[c % 2, lo:hi, lanes]
        first = jnp.concatenate([u(prev_row, prev_row + 1), u(tm - SUBLANE, tm - 1)], axis=0)
        last = jnp.concatenate([u(1, SUBLANE), u(next_row, next_row + 1)], axis=0)
        before = jnp.concatenate([first, u(0, tm - SUBLANE)], axis=0)
        after = jnp.concatenate([u(SUBLANE, tm), last], axis=0)
        return (cb_ref[:, cols] + before * cw_ref[0:1, cols] + u(0, tm) * cw_ref[1:2, cols]
                + after * cw_ref[2:3, cols])

    up(0)
    for c in range(nchunks):
        if c + 1 < nchunks:
            up(c + 1)
        glu_s[:, c * f:(c + 1) * f] = (_silu(conv(c, 0, 0)) * conv(c, 1, D_FF)).astype(BF16)
    _store_lane_blocks(d_s, jnp.dot(glu_s[...], wdn_ref[...], preferred_element_type=F32))
    for s in range(SUBLANE):
        rows = slice(s * nv, (s + 1) * nv)
        x2 = x1_ref[rows, :] + _load_strided_rows(d_s, s, nv, SUBLANE)
        o_ref[rows, :] = _rms(x2, fg_ref[...])


def _ffn(h2, x1, w_up, conv_w, conv_b, w_down, fg, S, tm):
    T = h2.shape[0]
    n_seq_tiles = S // tm
    hb = tm // HALO
    nh = T // HALO
    row = lambda i: (i, 0)
    const = lambda i: (0, 0)
    prev = lambda i: (jnp.maximum(i * hb - 1, 0), 0)
    nxt = lambda i: (jnp.minimum((i + 1) * hb, nh - 1), 0)
    return pl.pallas_call(
        functools.partial(_ffn_kernel, tm=tm, n_seq_tiles=n_seq_tiles),
        grid=(T // tm,),
        in_specs=[
            pl.BlockSpec((HALO, D_MODEL), prev),
            pl.BlockSpec((tm, D_MODEL), row),
            pl.BlockSpec((HALO, D_MODEL), nxt),
            pl.BlockSpec((tm, D_MODEL), row),
            pl.BlockSpec((D_MODEL, 2 * D_FF), const),
            pl.BlockSpec((FFN_CONV, 2 * D_FF), const),
            pl.BlockSpec((1, 2 * D_FF), const),
            pl.BlockSpec((D_FF, D_MODEL), const),
            pl.BlockSpec((1, D_MODEL), const),
        ],
        out_specs=pl.BlockSpec((tm, D_MODEL), row),
        out_shape=jax.ShapeDtypeStruct((T, D_MODEL), F32),
        scratch_shapes=[pltpu.VMEM((tm + HALO, D_MODEL), BF16),
                        pltpu.VMEM((2, tm + HALO, 2 * FFN_COLS), F32),
                        pltpu.VMEM((tm, D_FF), BF16),
                        pltpu.VMEM((D_MODEL // LANE, tm, LANE), F32)],
        compiler_params=pltpu.CompilerParams(
            dimension_semantics=("arbitrary",), vmem_limit_bytes=VMEM_LIMIT),
        name="ffn",
    )(h2, h2, h2, x1, w_up, conv_w, conv_b, w_down, fg)


def _layer(x2d, B, S, norm1_g, w_in, attn_sink, attn_out_g, ssd_conv_w, ssd_conv_b,
           dt_bias_f, dt_bias_b, a_log_f, a_log_b, ssd_d, ssd_norm_g, w_out, norm2_g,
           w_up, ffn_conv_w, ffn_conv_b, w_down, out_g, tables, tm, ts, tq):
    row = lambda v: v.reshape(1, -1).astype(F32)
    w_qkvz = w_in[:, :QKVZ_COLS].astype(BF16)
    w_xbc = w_in[:, QKVZ_COLS:QKVZ_COLS + XBC_COLS].astype(BF16)
    w_dt = jnp.pad(w_in[:, QKVZ_COLS + XBC_COLS:], ((0, 0), (0, LANE - DT_COLS))).astype(BF16)
    q, k, vT, z, xc, dtT = _in_proj(x2d, row(norm1_g), w_qkvz, w_xbc, w_dt,
                                    ssd_conv_w.astype(F32), row(ssd_conv_b), tables, S, tm)

    sink_rows = jnp.repeat(attn_sink.astype(F32) * LOG2E, BLOCK).reshape(KV_HEADS, GQA_GROUP * BLOCK)
    attn = _attention(q, k, vT, sink_rows, row(attn_out_g), B, S, tq)

    bias = jnp.concatenate([dt_bias_f, dt_bias_b]).astype(F32)
    a = -jnp.exp(jnp.concatenate([a_log_f, a_log_b]).astype(F32))
    head_of_col = jnp.arange(SSM_INNER) // SSM_HEAD_DIM
    w_o = w_out.astype(BF16)
    out = None
    for reverse in (False, True):
        d0 = SSM_HEADS if reverse else 0
        term_row = jnp.arange(SPLIT_TERMS * DT_COLS) % DT_COLS
        e = (term_row[:, None] == (head_of_col + d0)[None, :]).astype(BF16)
        if reverse:
            extra = (z, out, row(ssd_norm_g), x2d, attn, w_o[:ATTN_WIDTH], w_o[ATTN_WIDTH:], row(norm2_g))
        else:
            extra = (row(jnp.repeat(ssd_d.astype(F32), SSM_HEAD_DIM)),)
        out = _ssd_pass(xc, dtT, bias.reshape(-1, 1), a.reshape(-1, 1), e, extra, B, S, ts, tm, reverse)
    x1, h2 = out
    return _ffn(h2, x1, w_up.astype(BF16), ffn_conv_w.astype(F32), row(ffn_conv_b),
                w_down.astype(BF16), out_g, S, tm)


def kernel(x, norm1_g, w_in, attn_sink, attn_out_g, ssd_conv_w, ssd_conv_b, ssd_dt_bias_fwd,
           ssd_dt_bias_bwd, ssd_a_log_fwd, ssd_a_log_bwd, ssd_d, ssd_norm_g, w_out, norm2_g, w_up,
           ffn_conv_w, ffn_conv_b, w_down, final_norm_g, *, tm=512, ts=1024, tq=1024):
    B, S, _ = x.shape
    depth = w_in.shape[0]
    assert depth == 1, "the fused final RMSNorm assumes a single layer"
    tables = _rope_tables(S)
    x2d = x.reshape(B * S, D_MODEL)
    out = _layer(x2d, B, S, norm1_g[0], w_in[0], attn_sink[0], attn_out_g[0], ssd_conv_w[0], ssd_conv_b[0],
                 ssd_dt_bias_fwd[0], ssd_dt_bias_bwd[0], ssd_a_log_fwd[0], ssd_a_log_bwd[0], ssd_d[0],
                 ssd_norm_g[0], w_out[0], norm2_g[0], w_up[0], ffn_conv_w[0], ffn_conv_b[0], w_down[0],
                 final_norm_g.reshape(1, -1).astype(F32), tables, tm, ts, tq)
    return out.reshape(B, S, D_MODEL)
```

```python
import functools
import math

import jax
import jax.numpy as jnp
from jax import lax
from jax.experimental import pallas as pl
from jax.experimental.pallas import tpu as pltpu

D_MODEL = 1024
HEAD_DIM = 64
ATTN_WIDTH = 512
ATTN_HEADS = 8
KV_HEADS = 2
GQA_GROUP = 4
ROT_DIM = 16
ROPE_THETA = 500000.0
WINDOW = 128
BLOCK = 128
SSM_INNER = 512
SSM_HEAD_DIM = 64
SSM_HEADS = 8
SSM_GROUPS = 2
SSM_HPG = 4
D_STATE = 128
SSM_CONV = 5
CHUNK = 128
D_FF = 2816
FFN_CONV = 3
EPS = 1e-5
NEG = -1e30
LOG2E = math.log2(math.e)

KV_COLS = KV_HEADS * HEAD_DIM
XBC_COLS = SSM_INNER + 2 * SSM_GROUPS * D_STATE
QKVZ_COLS = ATTN_WIDTH + 2 * KV_COLS + SSM_INNER
DT_COLS = 2 * SSM_HEADS
SPLIT_TERMS = 3
GROUP_W = SSM_HPG * SSM_HEAD_DIM

LANE = 128
SUBLANE = 8
BF16_ROWS = 16
HALO = BF16_ROWS
assert WINDOW == BLOCK, "the banded attention masks assume one key block of reach on each side"
CONV_COLS = 256
SSD_ROWS = 1
VMEM_LIMIT = 56 * 1024 * 1024

F32 = jnp.float32
BF16 = jnp.bfloat16


def _rms(x, g):
    return x * lax.rsqrt(jnp.mean(x * x, axis=-1, keepdims=True) + EPS) * g


def _silu(x):
    return x * (1.0 / (1.0 + jnp.exp(-x)))


def _softplus(x):
    return jnp.maximum(x, 0.0) + jnp.log1p(jnp.exp(-jnp.abs(x)))


def _interleave_tokens(x):
    tm, cols = x.shape
    return jnp.swapaxes(x.reshape(SUBLANE, tm // SUBLANE, cols), 0, 1).reshape(tm, cols)


def _store_lane_blocks(ref, val):
    for j in range(ref.shape[0]):
        ref[j] = val[:, j * LANE:(j + 1) * LANE]


def _load_strided_rows(ref, start, size, stride):
    return jnp.concatenate([ref[j, pl.ds(start, size, stride=stride), :] for j in range(ref.shape[0])], axis=1)


def _inproj_kernel(xp_ref, x_ref, xn_ref, g_ref, w_ref, wx_ref, wdt_ref, cw_ref, cb_ref,
                   cos_ref, sa_ref, sb_ref,
                   q_ref, k_ref, vT_ref, z_ref, xc_ref, dtT_ref, hext_s, xbc_s, proj_s, xcp_s,
                   *, tm, n_seq_tiles):
    tile = pl.program_id(0) % n_seq_tiles
    nv = tm // SUBLANE
    norm = lambda r: _rms(r[...], g_ref[...]).astype(BF16)
    hf = _rms(x_ref[...], g_ref[...])
    hb = hf.astype(BF16)
    hext_s[0:tm, :] = _interleave_tokens(hf).astype(BF16)
    before = jnp.where(tile > 0, norm(xp_ref), jnp.zeros((HALO, D_MODEL), BF16))
    after = jnp.where(tile < n_seq_tiles - 1, norm(xn_ref), jnp.zeros((HALO, D_MODEL), BF16))
    halo_row = lax.broadcasted_iota(jnp.int32, (HALO, D_MODEL), 0)
    hext_s[tm:, :] = jnp.where(halo_row >= HALO // 2, before, after)

    f = CONV_COLS
    nchunks = XBC_COLS // f

    def xbc_proj(c):
        xbc_s[c % 2] = jnp.dot(hext_s[...], wx_ref[:, c * f:(c + 1) * f], preferred_element_type=F32)

    def conv(c):
        cols = slice(c * f, (c + 1) * f)
        u = lambda lo, hi: xbc_s[c % 2, lo:hi, :]
        tok = lambda t: u(tm + t % HALO, tm + t % HALO + 1)
        cat = lambda *parts: jnp.concatenate(parts, axis=0)
        wrap_m1 = cat(tok(-1), u(tm - SUBLANE, tm - 1))
        wrap_m2 = cat(tok(-2), u(tm - 2 * SUBLANE, tm - SUBLANE - 1))
        wrap_p1 = cat(u(1, SUBLANE), tok(0))
        wrap_p2 = cat(u(SUBLANE + 1, 2 * SUBLANE), tok(1))
        taps = [cat(wrap_m2, wrap_m1, u(0, tm - 2 * SUBLANE)),
                cat(wrap_m1, u(0, tm - SUBLANE)),
                u(0, tm),
                cat(u(SUBLANE, tm), wrap_p1),
                cat(u(2 * SUBLANE, tm), wrap_p1, wrap_p2)]
        acc = cb_ref[:, cols]
        for kk in range(SSM_CONV):
            acc = acc + taps[kk] * cw_ref[kk:kk + 1, cols]
        act = _silu(acc)
        for j in range(f // LANE):
            xcp_s[c * (f // LANE) + j] = act[:, j * LANE:(j + 1) * LANE]

    cos, sa, sb = cos_ref[...], sa_ref[...], sb_ref[...]

    def rope(t):
        n = t.shape[1]
        reps = n // LANE
        cc = jnp.concatenate([cos] * reps, axis=1) if reps > 1 else cos
        aa = jnp.concatenate([sa] * reps, axis=1) if reps > 1 else sa
        bb = jnp.concatenate([sb] * reps, axis=1) if reps > 1 else sb
        half = ROT_DIM // 2
        return t * cc + pltpu.roll(t, n - half, 1) * aa + pltpu.roll(t, half, 1) * bb

    assert nchunks == 4
    qk = ATTN_WIDTH + KV_COLS
    xbc_proj(0)
    proj_s[:, :qk] = jnp.dot(hb, w_ref[:, :qk], preferred_element_type=F32)
    xbc_proj(1)
    conv(0)
    q_ref[...] = (rope(proj_s[:, :ATTN_WIDTH]) * (LOG2E / math.sqrt(HEAD_DIM))).astype(BF16)
    k_ref[...] = rope(proj_s[:, ATTN_WIDTH:qk]).astype(BF16)
    xbc_proj(2)
    conv(1)
    proj_s[:, qk:QKVZ_COLS] = jnp.dot(hb, w_ref[:, qk:], preferred_element_type=F32)
    proj_s[:, QKVZ_COLS:] = jnp.dot(hb, wdt_ref[...], preferred_element_type=F32)
    xbc_proj(3)
    conv(2)
    vT_ref[...] = proj_s[:, qk:qk + KV_COLS].T.astype(BF16)
    z_ref[...] = proj_s[:, qk + KV_COLS:QKVZ_COLS].astype(BF16)
    dtT_ref[...] = proj_s[:, QKVZ_COLS:].T[:DT_COLS, :]
    conv(3)
    for s in range(SUBLANE):
        xc_ref[s * nv:(s + 1) * nv, :] = _load_strided_rows(xcp_s, s, nv, SUBLANE).astype(BF16)


def _rope_tables(S):
    half = ROT_DIM // 2
    pos = jnp.arange(S, dtype=F32)
    inv = ROPE_THETA ** (-jnp.arange(0, ROT_DIM, 2, dtype=F32) / ROT_DIM)
    ang = pos[:, None] * inv[None, :]
    cos, sin = jnp.cos(ang), jnp.sin(ang)
    rest = HEAD_DIM - ROT_DIM
    ones, zeros, zh = jnp.ones((S, rest), F32), jnp.zeros((S, rest), F32), jnp.zeros((S, half), F32)
    c = jnp.concatenate([cos, cos, ones], axis=1)
    sa = jnp.concatenate([-sin, zh, zeros], axis=1)
    sb = jnp.concatenate([zh, sin, zeros], axis=1)
    rep = LANE // HEAD_DIM
    return tuple(jnp.tile(t, (1, rep)) for t in (c, sa, sb))


def _in_proj(x2d, g1, w_qkvz, w_xbc, w_dt, conv_w, conv_b, tables, S, tm):
    T = x2d.shape[0]
    n_seq_tiles = S // tm
    hb = tm // HALO
    nh = T // HALO
    row = lambda i: (i, 0)
    col = lambda i: (0, i)
    const = lambda i: (0, 0)
    prev = lambda i: (jnp.maximum(i * hb - 1, 0), 0)
    nxt = lambda i: (jnp.minimum((i + 1) * hb, nh - 1), 0)
    tab = lambda i: (i % n_seq_tiles, 0)
    return pl.pallas_call(
        functools.partial(_inproj_kernel, tm=tm, n_seq_tiles=n_seq_tiles),
        grid=(T // tm,),
        in_specs=[
            pl.BlockSpec((HALO, D_MODEL), prev),
            pl.BlockSpec((tm, D_MODEL), row),
            pl.BlockSpec((HALO, D_MODEL), nxt),
            pl.BlockSpec((1, D_MODEL), const),
            pl.BlockSpec((D_MODEL, QKVZ_COLS), const),
            pl.BlockSpec((D_MODEL, XBC_COLS), const),
            pl.BlockSpec((D_MODEL, LANE), const),
            pl.BlockSpec((SSM_CONV, XBC_COLS), const),
            pl.BlockSpec((1, XBC_COLS), const),
            pl.BlockSpec((tm, LANE), tab),
            pl.BlockSpec((tm, LANE), tab),
            pl.BlockSpec((tm, LANE), tab),
        ],
        out_specs=[
            pl.BlockSpec((tm, ATTN_WIDTH), row),
            pl.BlockSpec((tm, KV_COLS), row),
            pl.BlockSpec((KV_COLS, tm), col),
            pl.BlockSpec((tm, SSM_INNER), row),
            pl.BlockSpec((tm, XBC_COLS), row),
            pl.BlockSpec((None, DT_COLS, tm), lambda i: (i // n_seq_tiles, 0, i % n_seq_tiles)),
        ],
        out_shape=[
            jax.ShapeDtypeStruct((T, ATTN_WIDTH), BF16),
            jax.ShapeDtypeStruct((T, KV_COLS), BF16),
            jax.ShapeDtypeStruct((KV_COLS, T), BF16),
            jax.ShapeDtypeStruct((T, SSM_INNER), BF16),
            jax.ShapeDtypeStruct((T, XBC_COLS), BF16),
            jax.ShapeDtypeStruct((T // S, DT_COLS, S), F32),
        ],
        scratch_shapes=[pltpu.VMEM((tm + HALO, D_MODEL), BF16),
                        pltpu.VMEM((2, tm + HALO, CONV_COLS), F32),
                        pltpu.VMEM((tm, QKVZ_COLS + LANE), F32),
                        pltpu.VMEM((XBC_COLS // LANE, tm, LANE), F32)],
        compiler_params=pltpu.CompilerParams(
            dimension_semantics=("arbitrary",), vmem_limit_bytes=VMEM_LIMIT),
        name="in_proj",
    )(x2d, x2d, x2d, g1, w_qkvz, w_xbc, w_dt, conv_w, conv_b, *tables)


def _attn_kernel(sink_ref, q_ref, kp_ref, km_ref, kn_ref, vp_ref, vm_ref, vn_ref, g_ref, o_ref,
                 k_s, vT_s, sT_s, *, tq):
    i = pl.program_id(1)
    n = pl.num_programs(1)
    k_s[0:BLOCK, :] = kp_ref[...]
    k_s[BLOCK:BLOCK + tq, :] = km_ref[...]
    k_s[BLOCK + tq:, :] = kn_ref[...]
    vT_s[:, 0:BLOCK] = vp_ref[...]
    vT_s[:, BLOCK:BLOCK + tq] = vm_ref[...]
    vT_s[:, BLOCK + tq:] = vn_ref[...]

    band = 3 * BLOCK
    nq = GQA_GROUP * BLOCK
    key = lax.broadcasted_iota(jnp.int32, (BLOCK, nq), 0)
    qry = lax.broadcasted_iota(jnp.int32, (BLOCK, nq), 1) % BLOCK
    no_prev = jnp.where(i > 0, 0, BLOCK)
    no_next = jnp.where(i < n - 1, 0, BLOCK)
    cap = lambda ok: jnp.where(ok, jnp.inf, NEG).astype(F32)
    cap_prev, cap_prev_edge = cap(key >= qry), cap(key >= qry + no_prev)
    cap_next, cap_next_edge = cap(key <= qry), cap(key <= qry - no_next)
    ones = jnp.ones((BF16_ROWS, band), BF16)

    nsub = tq // BLOCK
    units = [(j, kh) for j in range(nsub) for kh in range(KV_HEADS)]

    def scores(u):
        j, kh = units[u]
        qj = q_ref[j * BLOCK:(j + 1) * BLOCK, :]
        kk = k_s[j * BLOCK:j * BLOCK + band, kh * HEAD_DIM:(kh + 1) * HEAD_DIM]
        qs = jnp.concatenate([qj[:, (kh * GQA_GROUP + g) * HEAD_DIM:(kh * GQA_GROUP + g + 1) * HEAD_DIM]
                              for g in range(GQA_GROUP)], axis=0)
        sT_s[u % 2] = lax.dot_general(kk, qs, (((1,), (1,)), ((), ())), preferred_element_type=F32)

    scores(0)
    for j in range(nsub):
        prev_cap = cap_prev_edge if j == 0 else cap_prev
        next_cap = cap_next_edge if j == nsub - 1 else cap_next
        vTj = vT_s[:, j * BLOCK:j * BLOCK + band]
        outs = []
        for kh in range(KV_HEADS):
            u = j * KV_HEADS + kh
            if u + 1 < len(units):
                scores(u + 1)
            s_blocks = [jnp.minimum(sT_s[u % 2, :BLOCK, :], prev_cap), sT_s[u % 2, BLOCK:2 * BLOCK, :],
                        jnp.minimum(sT_s[u % 2, 2 * BLOCK:, :], next_cap)]
            sink = sink_ref[kh:kh + 1, :]
            s_max = jnp.maximum(jnp.maximum(s_blocks[0], s_blocks[1]), s_blocks[2])
            m = jnp.maximum(jnp.max(s_max, axis=0, keepdims=True), sink)
            p = jnp.concatenate([jnp.exp2(sb - m) for sb in s_blocks], axis=0).astype(BF16)
            v_ext = jnp.concatenate([vTj[kh * HEAD_DIM:(kh + 1) * HEAD_DIM, :], ones], axis=0)
            o_ext = jnp.dot(v_ext, p, preferred_element_type=F32)
            denom = o_ext[HEAD_DIM:HEAD_DIM + 1, :] + jnp.exp2(sink - m)
            oT = o_ext[:HEAD_DIM, :] / denom
            outs += [oT[:, g * BLOCK:(g + 1) * BLOCK] for g in range(GQA_GROUP)]
        oT_all = jnp.concatenate(outs, axis=0)
        inv = lax.rsqrt(jnp.mean(oT_all * oT_all, axis=0, keepdims=True) + EPS)
        o_ref[j * BLOCK:(j + 1) * BLOCK, :] = ((oT_all * inv).T * g_ref[...]).astype(BF16)


def _attention(q, k, vT, sink_rows, g, B, S, tq):
    T = q.shape[0]
    nq = S // tq
    sub = tq // BLOCK
    nblk = S // BLOCK
    main = lambda b, i: (b * nq + i, 0)
    prev = lambda b, i: (b * nblk + jnp.maximum(i * sub - 1, 0), 0)
    nxt = lambda b, i: (b * nblk + jnp.minimum((i + 1) * sub, nblk - 1), 0)
    swap = lambda f: (lambda b, i: f(b, i)[::-1])
    const = lambda b, i: (0, 0)
    return pl.pallas_call(
        functools.partial(_attn_kernel, tq=tq),
        grid=(B, nq),
        in_specs=[
            pl.BlockSpec((KV_HEADS, GQA_GROUP * BLOCK), const),
            pl.BlockSpec((tq, ATTN_WIDTH), main),
            pl.BlockSpec((BLOCK, KV_COLS), prev),
            pl.BlockSpec((tq, KV_COLS), main),
            pl.BlockSpec((BLOCK, KV_COLS), nxt),
            pl.BlockSpec((KV_COLS, BLOCK), swap(prev)),
            pl.BlockSpec((KV_COLS, tq), swap(main)),
            pl.BlockSpec((KV_COLS, BLOCK), swap(nxt)),
            pl.BlockSpec((1, ATTN_WIDTH), const),
        ],
        out_specs=pl.BlockSpec((tq, ATTN_WIDTH), main),
        out_shape=jax.ShapeDtypeStruct((T, ATTN_WIDTH), BF16),
        scratch_shapes=[pltpu.VMEM((tq + 2 * BLOCK, KV_COLS), BF16),
                        pltpu.VMEM((KV_COLS, tq + 2 * BLOCK), BF16),
                        pltpu.VMEM((2, 3 * BLOCK, GQA_GROUP * BLOCK), F32)],
        compiler_params=pltpu.CompilerParams(
            dimension_semantics=("arbitrary", "arbitrary"), vmem_limit_bytes=VMEM_LIMIT),
        name="attention",
    )(sink_rows, q, k, k, k, vT, vT, vT, g)


def _split_terms(v):
    hi = v.astype(BF16).astype(F32)
    r1 = v - hi
    mid = r1.astype(BF16).astype(F32)
    lo = (r1 - mid).astype(BF16).astype(F32)
    return jnp.concatenate([hi, mid, lo], axis=0)


def _ssd_kernel(*refs, reverse, ts, tm, nb):
    if reverse:
        (xc_all, dtT_all_ref, biasT_ref, aT_ref, e_ref, z_all, yf_all, ng_ref,
         x_all, attn_all, wa_ref, ws_ref, g2_ref, perm_ref, x1_all, hout_all, h_all, o_all) = refs
    else:
        (xc_all, dtT_all_ref, biasT_ref, aT_ref, e_ref, dskip_ref, o_all, h_all) = refs
    t = pl.program_id(1)
    d0 = SSM_HEADS if reverse else 0

    @pl.when(t == 0)
    def _():
        h_all[...] = jnp.zeros_like(h_all)

    li = lax.broadcasted_iota(jnp.int32, (CHUNK, CHUNK), 0)
    si = lax.broadcasted_iota(jnp.int32, (CHUNK, CHUNK), 1)
    causal = (li <= si) if reverse else (li >= si)
    scan_op = jnp.where((li >= si) if reverse else (li <= si), 1.0, 0.0).astype(BF16)
    last = 0 if reverse else CHUNK - 1
    nchunk = ts // CHUNK
    nterm = SPLIT_TERMS * DT_COLS

    def chunks_of(vT):
        return [vT[:, c * CHUNK:(c + 1) * CHUNK] for c in range(nchunk)]

    def sum_terms(m):
        return m[0:DT_COLS] + m[DT_COLS:2 * DT_COLS] + m[2 * DT_COLS:nterm]

    def expand(vT_c):
        terms = jnp.concatenate([_split_terms(v).T for v in vT_c], axis=0)
        return jnp.dot(terms.astype(BF16), e_ref[...], preferred_element_type=F32)

    def per_head_terms(dtT_ref):
        dtT_all = _softplus(dtT_ref[...] + biasT_ref[...])
        a_terms = jnp.concatenate([_split_terms(v) for v in chunks_of(dtT_all * aT_ref[...])], axis=0)
        cs = jnp.dot(a_terms.astype(BF16), scan_op, preferred_element_type=F32)
        dtT_c = chunks_of(dtT_all)
        acsT_c = [sum_terms(cs[c * nterm:(c + 1) * nterm]) for c in range(nchunk)]
        tgt_c = [(v * LOG2E).T for v in acsT_c]
        src_c = [acsT_c[c] * LOG2E - jnp.log2(dtT_c[c]) for c in range(nchunk)]
        decayT_c = [dtT_c[c] * jnp.exp(acsT_c[c][:, last:last + 1] - acsT_c[c]) for c in range(nchunk)]
        exp_acs_all = expand([jnp.exp(v) for v in acsT_c])
        decay_all = expand(decayT_c)
        return tgt_c, src_c, exp_acs_all, decay_all

    terms = [per_head_terms(dtT_all_ref.at[bi]) for bi in range(nb)]

    def project_block(bi, k):
        x_ref, attn_ref, x1_ref, hout_ref, o_ref = (r.at[bi] for r in (x_all, attn_all, x1_all, hout_all, o_all))
        rows = slice(k * tm, (k + 1) * tm)
        x1 = (x_ref[rows, :]
              + jnp.dot(attn_ref[rows, :], wa_ref[...], preferred_element_type=F32)
              + jnp.dot(o_ref[rows, :], ws_ref[...], preferred_element_type=F32))
        x1_ref[rows, :] = x1
        h = _rms(x1, g2_ref[...]).astype(BF16)
        hout_ref[rows, :] = jnp.dot(perm_ref[...], h, preferred_element_type=F32).astype(BF16)

    def scan_chunk(bi, c):
        xc_ref, o_ref, h_s = xc_all.at[bi], o_all.at[bi], h_all.at[bi]
        tgt_c, src_c, exp_acs_all, decay_all = terms[bi]
        r0 = c * CHUNK
        xc = xc_ref[r0:r0 + CHUNK, :]
        x_b = xc[:, :SSM_INNER]
        xs = x_b.astype(F32)
        bm = xc[:, SSM_INNER:SSM_INNER + SSM_GROUPS * D_STATE]
        cm = xc[:, SSM_INNER + SSM_GROUPS * D_STATE:]

        tgt, src = tgt_c[c], src_c[c]
        exp_acs = exp_acs_all[r0:r0 + CHUNK, :]
        x_decay = (xs * decay_all[r0:r0 + CHUNK, :]).astype(BF16)
        chunk_decay = exp_acs[last:last + 1, :]

        ys = []
        for g in range(SSM_GROUPS):
            bg = bm[:, g * D_STATE:(g + 1) * D_STATE]
            cg = cm[:, g * D_STATE:(g + 1) * D_STATE]
            cb = lax.dot_general(cg, bg, (((1,), (1,)), ((), ())), preferred_element_type=F32)
            gs = slice(g * GROUP_W, (g + 1) * GROUP_W)
            bgT = bg.astype(F32).T.astype(BF16)
            st = jnp.dot(bgT, x_decay[:, gs], preferred_element_type=F32)
            h_prev = h_s[g]
            y_off = jnp.dot(cg, h_prev.astype(BF16), preferred_element_type=F32)
            h_s[g] = h_prev * chunk_decay[:, gs] + st
            y_diag = []
            for rr in range(SSM_HPG):
                r = g * SSM_HPG + rr
                seg = tgt[:, d0 + r:d0 + r + 1] - src[d0 + r:d0 + r + 1, :]
                lmat = jnp.exp2(jnp.where(causal, seg, -jnp.inf))
                mm = (cb * lmat).astype(BF16)
                y_diag.append(jnp.dot(mm, x_b[:, r * SSM_HEAD_DIM:(r + 1) * SSM_HEAD_DIM],
                                      preferred_element_type=F32))
            ys.append(jnp.concatenate(y_diag, axis=1) + y_off * exp_acs[:, gs])
        y = jnp.concatenate(ys, axis=1)

        if not reverse:
            o_ref[r0:r0 + CHUNK, :] = y + xs * dskip_ref[...]
        else:
            yf_ref, z_ref = yf_all.at[bi], z_all.at[bi]
            y = (y + yf_ref[r0:r0 + CHUNK, :]) * _silu(z_ref[r0:r0 + CHUNK, :].astype(F32))
            parts = []
            for g in range(SSM_GROUPS):
                yg = y[:, g * GROUP_W:(g + 1) * GROUP_W]
                parts.append(yg * lax.rsqrt(jnp.mean(yg * yg, axis=-1, keepdims=True) + EPS))
            o_ref[r0:r0 + CHUNK, :] = (jnp.concatenate(parts, axis=1) * ng_ref[...]).astype(BF16)

            if r0 % tm == 0:
                project_block(bi, r0 // tm)

    for c in (range(nchunk - 1, -1, -1) if reverse else range(nchunk)):
        for bi in range(nb):
            scan_chunk(bi, c)


def _ssd_pass(xc, dtT, biasT, aT, e, extra, B, S, ts, tm, nb, reverse):
    nt = S // ts
    seq = (lambda t: nt - 1 - t) if reverse else (lambda t: t)
    main = lambda b, t: (b, seq(t), 0)
    const = lambda b, t: (0, 0)
    tok = lambda cols: pl.BlockSpec((nb, ts, cols), main)
    rows3 = lambda a: a.reshape(B, S, a.shape[-1])
    state = pltpu.VMEM((nb, SSM_GROUPS, D_STATE, GROUP_W), F32)
    in_specs = [
        tok(XBC_COLS),
        pl.BlockSpec((nb, DT_COLS, ts), lambda b, t: (b, 0, seq(t))),
        pl.BlockSpec((DT_COLS, 1), const),
        pl.BlockSpec((DT_COLS, 1), const),
        pl.BlockSpec((SPLIT_TERMS * DT_COLS, SSM_INNER), const),
    ]
    args = [rows3(xc), dtT, biasT, aT, e]
    if reverse:
        z, yf, ng, x2d, attn, w_attn, w_ssd, g2 = extra
        in_specs += [tok(SSM_INNER), tok(SSM_INNER), pl.BlockSpec((1, SSM_INNER), const),
                     tok(D_MODEL), tok(ATTN_WIDTH),
                     pl.BlockSpec((ATTN_WIDTH, D_MODEL), const), pl.BlockSpec((SSM_INNER, D_MODEL), const),
                     pl.BlockSpec((1, D_MODEL), const), pl.BlockSpec((tm, tm), const)]
        perm = _interleave_tokens(jnp.eye(tm, dtype=F32)).astype(BF16)
        args += [rows3(z), yf, ng, rows3(x2d), rows3(attn), w_attn, w_ssd, g2, perm]
        out_specs = [tok(D_MODEL), tok(D_MODEL)]
        out_shape = [jax.ShapeDtypeStruct((B, S, D_MODEL), F32), jax.ShapeDtypeStruct((B, S, D_MODEL), BF16)]
        scratch = [state, pltpu.VMEM((nb, ts, SSM_INNER), BF16)]
    else:
        (dskip,) = extra
        in_specs += [pl.BlockSpec((1, SSM_INNER), const)]
        args += [dskip]
        out_specs = tok(SSM_INNER)
        out_shape = jax.ShapeDtypeStruct((B, S, SSM_INNER), F32)
        scratch = [state]
    out = pl.pallas_call(
        functools.partial(_ssd_kernel, reverse=reverse, ts=ts, tm=tm, nb=nb),
        grid=(B // nb, nt),
        in_specs=in_specs,
        out_specs=out_specs,
        out_shape=out_shape,
        scratch_shapes=scratch,
        compiler_params=pltpu.CompilerParams(
            dimension_semantics=("arbitrary", "arbitrary"), vmem_limit_bytes=VMEM_LIMIT),
        name="ssd_bwd" if reverse else "ssd_fwd",
    )(*args)
    return [o.reshape(B * S, D_MODEL) for o in out] if reverse else out


FFN_COLS = 256


def _ffn_kernel(hp_ref, hm_ref, hn_ref, x1_ref, wup_ref, cw_ref, cb_ref, wdn_ref, fg_ref, o_ref,
                hext_s, u_s, glu_s, d_s, *, tm, n_seq_tiles):
    tile = pl.program_id(0) % n_seq_tiles
    nv = tm // SUBLANE
    prev_row = tm + HALO - 1
    next_row = tm
    hext_s[0:tm, :] = hm_ref[...]
    before = jnp.where(tile > 0, hp_ref[...], jnp.zeros_like(hp_ref))
    after = jnp.where(tile < n_seq_tiles - 1, hn_ref[...], jnp.zeros_like(hn_ref))
    halo_row = lax.broadcasted_iota(jnp.int32, (HALO, D_MODEL), 0)
    hext_s[tm:, :] = jnp.where(halo_row >= HALO // 2, before, after)
    f = FFN_COLS
    nchunks = D_FF // f

    def up(c):
        hext = hext_s[...]
        for half, base in enumerate((0, D_FF)):
            u_s[c % 2, :, half * f:(half + 1) * f] = jnp.dot(
                hext, wup_ref[:, base + c * f:base + (c + 1) * f], preferred_element_type=F32)

    def conv(c, half, base):
        cols = slice(base + c * f, base + (c + 1) * f)
        lanes = slice(half * f, (half + 1) * f)
        u = lambda lo, hi: u_s[c % 2, lo:hi, lanes]
        first = jnp.concatenate([u(prev_row, prev_row + 1), u(tm - SUBLANE, tm - 1)], axis=0)
        last = jnp.concatenate([u(1, SUBLANE), u(next_row, next_row + 1)], axis=0)
        before = jnp.concatenate([first, u(0, tm - SUBLANE)], axis=0)
        after = jnp.concatenate([u(SUBLANE, tm), last], axis=0)
        return (cb_ref[:, cols] + before * cw_ref[0:1, cols] + u(0, tm) * cw_ref[1:2, cols]
                + after * cw_ref[2:3, cols])

    up(0)
    for c in range(nchunks):
        if c + 1 < nchunks:
            up(c + 1)
        glu_s[:, c * f:(c + 1) * f] = (_silu(conv(c, 0, 0)) * conv(c, 1, D_FF)).astype(BF16)
    _store_lane_blocks(d_s, jnp.dot(glu_s[...], wdn_ref[...], preferred_element_type=F32))
    for s in range(SUBLANE):
        rows = slice(s * nv, (s + 1) * nv)
        x2 = x1_ref[rows, :] + _load_strided_rows(d_s, s, nv, SUBLANE)
        o_ref[rows, :] = _rms(x2, fg_ref[...])


def _ffn(h2, x1, w_up, conv_w, conv_b, w_down, fg, S, tm):
    T = h2.shape[0]
    n_seq_tiles = S // tm
    hb = tm // HALO
    nh = T // HALO
    row = lambda i: (i, 0)
    const = lambda i: (0, 0)
    prev = lambda i: (jnp.maximum(i * hb - 1, 0), 0)
    nxt = lambda i: (jnp.minimum((i + 1) * hb, nh - 1), 0)
    return pl.pallas_call(
        functools.partial(_ffn_kernel, tm=tm, n_seq_tiles=n_seq_tiles),
        grid=(T // tm,),
        in_specs=[
            pl.BlockSpec((HALO, D_MODEL), prev),
            pl.BlockSpec((tm, D_MODEL), row),
            pl.BlockSpec((HALO, D_MODEL), nxt),
            pl.BlockSpec((tm, D_MODEL), row),
            pl.BlockSpec((D_MODEL, 2 * D_FF), const),
            pl.BlockSpec((FFN_CONV, 2 * D_FF), const),
            pl.BlockSpec((1, 2 * D_FF), const),
            pl.BlockSpec((D_FF, D_MODEL), const),
            pl.BlockSpec((1, D_MODEL), const),
        ],
        out_specs=pl.BlockSpec((tm, D_MODEL), row),
        out_shape=jax.ShapeDtypeStruct((T, D_MODEL), F32),
        scratch_shapes=[pltpu.VMEM((tm + HALO, D_MODEL), BF16),
                        pltpu.VMEM((2, tm + HALO, 2 * FFN_COLS), F32),
                        pltpu.VMEM((tm, D_FF), BF16),
                        pltpu.VMEM((D_MODEL // LANE, tm, LANE), F32)],
        compiler_params=pltpu.CompilerParams(
            dimension_semantics=("arbitrary",), vmem_limit_bytes=VMEM_LIMIT),
        name="ffn",
    )(h2, h2, h2, x1, w_up, conv_w, conv_b, w_down, fg)


def _layer(x2d, B, S, norm1_g, w_in, attn_sink, attn_out_g, ssd_conv_w, ssd_conv_b,
           dt_bias_f, dt_bias_b, a_log_f, a_log_b, ssd_d, ssd_norm_g, w_out, norm2_g,
           w_up, ffn_conv_w, ffn_conv_b, w_down, out_g, tables, tm, ts, tq):
    row = lambda v: v.reshape(1, -1).astype(F32)
    w_qkvz = w_in[:, :QKVZ_COLS].astype(BF16)
    w_xbc = w_in[:, QKVZ_COLS:QKVZ_COLS + XBC_COLS].astype(BF16)
    w_dt = jnp.pad(w_in[:, QKVZ_COLS + XBC_COLS:], ((0, 0), (0, LANE - DT_COLS))).astype(BF16)
    q, k, vT, z, xc, dtT = _in_proj(x2d, row(norm1_g), w_qkvz, w_xbc, w_dt,
                                    ssd_conv_w.astype(F32), row(ssd_conv_b), tables, S, tm)

    sink_rows = jnp.repeat(attn_sink.astype(F32) * LOG2E, BLOCK).reshape(KV_HEADS, GQA_GROUP * BLOCK)
    attn = _attention(q, k, vT, sink_rows, row(attn_out_g), B, S, tq)

    bias = jnp.concatenate([dt_bias_f, dt_bias_b]).astype(F32)
    a = -jnp.exp(jnp.concatenate([a_log_f, a_log_b]).astype(F32))
    head_of_col = jnp.arange(SSM_INNER) // SSM_HEAD_DIM
    w_o = w_out.astype(BF16)
    out = None
    for reverse in (False, True):
        d0 = SSM_HEADS if reverse else 0
        term_row = jnp.arange(SPLIT_TERMS * DT_COLS) % DT_COLS
        e = (term_row[:, None] == (head_of_col + d0)[None, :]).astype(BF16)
        if reverse:
            extra = (z, out, row(ssd_norm_g), x2d, attn, w_o[:ATTN_WIDTH], w_o[ATTN_WIDTH:], row(norm2_g))
        else:
            extra = (row(jnp.repeat(ssd_d.astype(F32), SSM_HEAD_DIM)),)
        out = _ssd_pass(xc, dtT, bias.reshape(-1, 1), a.reshape(-1, 1), e, extra, B, S, ts, tm,
                        SSD_ROWS, reverse)
    x1, h2 = out
    return _ffn(h2, x1, w_up.astype(BF16), ffn_conv_w.astype(F32), row(ffn_conv_b),
                w_down.astype(BF16), out_g, S, tm)


def kernel(x, norm1_g, w_in, attn_sink, attn_out_g, ssd_conv_w, ssd_conv_b, ssd_dt_bias_fwd,
           ssd_dt_bias_bwd, ssd_a_log_fwd, ssd_a_log_bwd, ssd_d, ssd_norm_g, w_out, norm2_g, w_up,
           ffn_conv_w, ffn_conv_b, w_down, final_norm_g, *, tm=512, ts=1024, tq=1024):
    B, S, _ = x.shape
    depth = w_in.shape[0]
    assert depth == 1, "the fused final RMSNorm assumes a single layer"
    tables = _rope_tables(S)
    x2d = x.reshape(B * S, D_MODEL)
    out = _layer(x2d, B, S, norm1_g[0], w_in[0], attn_sink[0], attn_out_g[0], ssd_conv_w[0], ssd_conv_b[0],
                 ssd_dt_bias_fwd[0], ssd_dt_bias_bwd[0], ssd_a_log_fwd[0], ssd_a_log_bwd[0], ssd_d[0],
                 ssd_norm_g[0], w_out[0], norm2_g[0], w_up[0], ffn_conv_w[0], ffn_conv_b[0], w_down[0],
                 final_norm_g.reshape(1, -1).astype(F32), tables, tm, ts, tq)
    return out.reshape(B, S, D_MODEL)
```

```python
import functools
import math

import jax
import jax.numpy as jnp
from jax import lax
from jax.experimental import pallas as pl
from jax.experimental.pallas import tpu as pltpu

D_MODEL = 1024
HEAD_DIM = 64
ATTN_WIDTH = 512
ATTN_HEADS = 8
KV_HEADS = 2
GQA_GROUP = 4
ROT_DIM = 16
ROPE_THETA = 500000.0
WINDOW = 128
BLOCK = 128
SSM_INNER = 512
SSM_HEAD_DIM = 64
SSM_HEADS = 8
SSM_GROUPS = 2
SSM_HPG = 4
D_STATE = 128
SSM_CONV = 5
CHUNK = 128
D_FF = 2816
FFN_CONV = 3
EPS = 1e-5
NEG = -1e30
LOG2E = math.log2(math.e)

KV_COLS = KV_HEADS * HEAD_DIM
XBC_COLS = SSM_INNER + 2 * SSM_GROUPS * D_STATE
QKVZ_COLS = ATTN_WIDTH + 2 * KV_COLS + SSM_INNER
DT_COLS = 2 * SSM_HEADS
SPLIT_TERMS = 3
GROUP_W = SSM_HPG * SSM_HEAD_DIM

LANE = 128
SUBLANE = 8
BF16_ROWS = 16
HALO = BF16_ROWS
assert WINDOW == BLOCK, "the banded attention masks assume one key block of reach on each side"
CONV_COLS = 256
SSD_ROWS = 1
VMEM_LIMIT = 56 * 1024 * 1024

F32 = jnp.float32
BF16 = jnp.bfloat16


def _rms(x, g):
    return x * lax.rsqrt(jnp.mean(x * x, axis=-1, keepdims=True) + EPS) * g


def _silu(x):
    return x * (1.0 / (1.0 + jnp.exp(-x)))


def _softplus(x):
    return jnp.maximum(x, 0.0) + jnp.log1p(jnp.exp(-jnp.abs(x)))


def _interleave_tokens(x):
    tm, cols = x.shape
    return jnp.swapaxes(x.reshape(SUBLANE, tm // SUBLANE, cols), 0, 1).reshape(tm, cols)


def _store_lane_blocks(ref, val):
    for j in range(ref.shape[0]):
        ref[j] = val[:, j * LANE:(j + 1) * LANE]


def _load_strided_rows(ref, start, size, stride):
    return jnp.concatenate([ref[j, pl.ds(start, size, stride=stride), :] for j in range(ref.shape[0])], axis=1)


def _inproj_kernel(xp_ref, x_ref, xn_ref, g_ref, w_ref, wx_ref, wdt_ref, cw_ref, cb_ref,
                   cos_ref, sa_ref, sb_ref,
                   q_ref, k_ref, vT_ref, z_ref, xc_ref, dtT_ref, hext_s, xbc_s, proj_s, xcp_s,
                   *, tm, n_seq_tiles):
    tile = pl.program_id(0) % n_seq_tiles
    nv = tm // SUBLANE
    norm = lambda r: _rms(r[...], g_ref[...]).astype(BF16)
    hf = _rms(x_ref[...], g_ref[...])
    hb = hf.astype(BF16)
    hext_s[0:tm, :] = _interleave_tokens(hf).astype(BF16)
    before = jnp.where(tile > 0, norm(xp_ref), jnp.zeros((HALO, D_MODEL), BF16))
    after = jnp.where(tile < n_seq_tiles - 1, norm(xn_ref), jnp.zeros((HALO, D_MODEL), BF16))
    halo_row = lax.broadcasted_iota(jnp.int32, (HALO, D_MODEL), 0)
    hext_s[tm:, :] = jnp.where(halo_row >= HALO // 2, before, after)

    f = CONV_COLS
    nchunks = XBC_COLS // f

    def xbc_proj(c):
        xbc_s[c % 2] = jnp.dot(hext_s[...], wx_ref[:, c * f:(c + 1) * f], preferred_element_type=F32)

    def conv(c):
        cols = slice(c * f, (c + 1) * f)
        u = lambda lo, hi: xbc_s[c % 2, lo:hi, :]
        tok = lambda t: u(tm + t % HALO, tm + t % HALO + 1)
        cat = lambda *parts: jnp.concatenate(parts, axis=0)
        wrap_m1 = cat(tok(-1), u(tm - SUBLANE, tm - 1))
        wrap_m2 = cat(tok(-2), u(tm - 2 * SUBLANE, tm - SUBLANE - 1))
        wrap_p1 = cat(u(1, SUBLANE), tok(0))
        wrap_p2 = cat(u(SUBLANE + 1, 2 * SUBLANE), tok(1))
        taps = [cat(wrap_m2, wrap_m1, u(0, tm - 2 * SUBLANE)),
                cat(wrap_m1, u(0, tm - SUBLANE)),
                u(0, tm),
                cat(u(SUBLANE, tm), wrap_p1),
                cat(u(2 * SUBLANE, tm), wrap_p1, wrap_p2)]
        acc = cb_ref[:, cols]
        for kk in range(SSM_CONV):
            acc = acc + taps[kk] * cw_ref[kk:kk + 1, cols]
        act = _silu(acc)
        for j in range(f // LANE):
            xcp_s[c * (f // LANE) + j] = act[:, j * LANE:(j + 1) * LANE]

    cos, sa, sb = cos_ref[...], sa_ref[...], sb_ref[...]

    def rope(t):
        n = t.shape[1]
        reps = n // LANE
        cc = jnp.concatenate([cos] * reps, axis=1) if reps > 1 else cos
        aa = jnp.concatenate([sa] * reps, axis=1) if reps > 1 else sa
        bb = jnp.concatenate([sb] * reps, axis=1) if reps > 1 else sb
        half = ROT_DIM // 2
        return t * cc + pltpu.roll(t, n - half, 1) * aa + pltpu.roll(t, half, 1) * bb

    assert nchunks == 4
    qk = ATTN_WIDTH + KV_COLS
    xbc_proj(0)
    proj_s[:, :qk] = jnp.dot(hb, w_ref[:, :qk], preferred_element_type=F32)
    xbc_proj(1)
    conv(0)
    q_ref[...] = (rope(proj_s[:, :ATTN_WIDTH]) * (LOG2E / math.sqrt(HEAD_DIM))).astype(BF16)
    k_ref[...] = rope(proj_s[:, ATTN_WIDTH:qk]).astype(BF16)
    xbc_proj(2)
    conv(1)
    proj_s[:, qk:QKVZ_COLS] = jnp.dot(hb, w_ref[:, qk:], preferred_element_type=F32)
    proj_s[:, QKVZ_COLS:] = jnp.dot(hb, wdt_ref[...], preferred_element_type=F32)
    xbc_proj(3)
    conv(2)
    vT_ref[...] = proj_s[:, qk:qk + KV_COLS].T.astype(BF16)
    z_ref[...] = proj_s[:, qk + KV_COLS:QKVZ_COLS].astype(BF16)
    dtT_ref[...] = proj_s[:, QKVZ_COLS:].T[:DT_COLS, :]
    conv(3)
    for s in range(SUBLANE):
        xc_ref[s * nv:(s + 1) * nv, :] = _load_strided_rows(xcp_s, s, nv, SUBLANE).astype(BF16)


def _rope_tables(S):
    half = ROT_DIM // 2
    pos = jnp.arange(S, dtype=F32)
    inv = ROPE_THETA ** (-jnp.arange(0, ROT_DIM, 2, dtype=F32) / ROT_DIM)
    ang = pos[:, None] * inv[None, :]
    cos, sin = jnp.cos(ang), jnp.sin(ang)
    rest = HEAD_DIM - ROT_DIM
    ones, zeros, zh = jnp.ones((S, rest), F32), jnp.zeros((S, rest), F32), jnp.zeros((S, half), F32)
    c = jnp.concatenate([cos, cos, ones], axis=1)
    sa = jnp.concatenate([-sin, zh, zeros], axis=1)
    sb = jnp.concatenate([zh, sin, zeros], axis=1)
    rep = LANE // HEAD_DIM
    return tuple(jnp.tile(t, (1, rep)) for t in (c, sa, sb))


def _in_proj(x2d, g1, w_qkvz, w_xbc, w_dt, conv_w, conv_b, tables, S, tm):
    T = x2d.shape[0]
    n_seq_tiles = S // tm
    hb = tm // HALO
    nh = T // HALO
    row = lambda i: (i, 0)
    col = lambda i: (0, i)
    const = lambda i: (0, 0)
    prev = lambda i: (jnp.maximum(i * hb - 1, 0), 0)
    nxt = lambda i: (jnp.minimum((i + 1) * hb, nh - 1), 0)
    tab = lambda i: (i % n_seq_tiles, 0)
    return pl.pallas_call(
        functools.partial(_inproj_kernel, tm=tm, n_seq_tiles=n_seq_tiles),
        grid=(T // tm,),
        in_specs=[
            pl.BlockSpec((HALO, D_MODEL), prev),
            pl.BlockSpec((tm, D_MODEL), row),
            pl.BlockSpec((HALO, D_MODEL), nxt),
            pl.BlockSpec((1, D_MODEL), const),
            pl.BlockSpec((D_MODEL, QKVZ_COLS), const),
            pl.BlockSpec((D_MODEL, XBC_COLS), const),
            pl.BlockSpec((D_MODEL, LANE), const),
            pl.BlockSpec((SSM_CONV, XBC_COLS), const),
            pl.BlockSpec((1, XBC_COLS), const),
            pl.BlockSpec((tm, LANE), tab),
            pl.BlockSpec((tm, LANE), tab),
            pl.BlockSpec((tm, LANE), tab),
        ],
        out_specs=[
            pl.BlockSpec((tm, ATTN_WIDTH), row),
            pl.BlockSpec((tm, KV_COLS), row),
            pl.BlockSpec((KV_COLS, tm), col),
            pl.BlockSpec((tm, SSM_INNER), row),
            pl.BlockSpec((tm, XBC_COLS), row),
            pl.BlockSpec((None, DT_COLS, tm), lambda i: (i // n_seq_tiles, 0, i % n_seq_tiles)),
        ],
        out_shape=[
            jax.ShapeDtypeStruct((T, ATTN_WIDTH), BF16),
            jax.ShapeDtypeStruct((T, KV_COLS), BF16),
            jax.ShapeDtypeStruct((KV_COLS, T), BF16),
            jax.ShapeDtypeStruct((T, SSM_INNER), BF16),
            jax.ShapeDtypeStruct((T, XBC_COLS), BF16),
            jax.ShapeDtypeStruct((T // S, DT_COLS, S), F32),
        ],
        scratch_shapes=[pltpu.VMEM((tm + HALO, D_MODEL), BF16),
                        pltpu.VMEM((2, tm + HALO, CONV_COLS), F32),
                        pltpu.VMEM((tm, QKVZ_COLS + LANE), F32),
                        pltpu.VMEM((XBC_COLS // LANE, tm, LANE), F32)],
        compiler_params=pltpu.CompilerParams(
            dimension_semantics=("arbitrary",), vmem_limit_bytes=VMEM_LIMIT),
        name="in_proj",
    )(x2d, x2d, x2d, g1, w_qkvz, w_xbc, w_dt, conv_w, conv_b, *tables)


def _attn_kernel(sink_ref, q_ref, kp_ref, km_ref, kn_ref, vp_ref, vm_ref, vn_ref, g_ref, o_ref,
                 k_s, vT_s, sT_s, *, tq):
    i = pl.program_id(1)
    n = pl.num_programs(1)
    k_s[0:BLOCK, :] = kp_ref[...]
    k_s[BLOCK:BLOCK + tq, :] = km_ref[...]
    k_s[BLOCK + tq:, :] = kn_ref[...]
    vT_s[:, 0:BLOCK] = vp_ref[...]
    vT_s[:, BLOCK:BLOCK + tq] = vm_ref[...]
    vT_s[:, BLOCK + tq:] = vn_ref[...]

    band = 3 * BLOCK
    nq = GQA_GROUP * BLOCK
    key = lax.broadcasted_iota(jnp.int32, (BLOCK, nq), 0)
    qry = lax.broadcasted_iota(jnp.int32, (BLOCK, nq), 1) % BLOCK
    no_prev = jnp.where(i > 0, 0, BLOCK)
    no_next = jnp.where(i < n - 1, 0, BLOCK)
    cap = lambda ok: jnp.where(ok, jnp.inf, NEG).astype(F32)
    cap_prev, cap_prev_edge = cap(key >= qry), cap(key >= qry + no_prev)
    cap_next, cap_next_edge = cap(key <= qry), cap(key <= qry - no_next)
    ones = jnp.ones((BF16_ROWS, band), BF16)

    nsub = tq // BLOCK
    units = [(j, kh) for j in range(nsub) for kh in range(KV_HEADS)]

    def scores(u):
        j, kh = units[u]
        qj = q_ref[j * BLOCK:(j + 1) * BLOCK, :]
        kk = k_s[j * BLOCK:j * BLOCK + band, kh * HEAD_DIM:(kh + 1) * HEAD_DIM]
        qs = jnp.concatenate([qj[:, (kh * GQA_GROUP + g) * HEAD_DIM:(kh * GQA_GROUP + g + 1) * HEAD_DIM]
                              for g in range(GQA_GROUP)], axis=0)
        sT_s[u % 2] = lax.dot_general(kk, qs, (((1,), (1,)), ((), ())), preferred_element_type=F32)

    scores(0)
    for j in range(nsub):
        prev_cap = cap_prev_edge if j == 0 else cap_prev
        next_cap = cap_next_edge if j == nsub - 1 else cap_next
        vTj = vT_s[:, j * BLOCK:j * BLOCK + band]
        outs = []
        for kh in range(KV_HEADS):
            u = j * KV_HEADS + kh
            if u + 1 < len(units):
                scores(u + 1)
            s_blocks = [jnp.minimum(sT_s[u % 2, :BLOCK, :], prev_cap), sT_s[u % 2, BLOCK:2 * BLOCK, :],
                        jnp.minimum(sT_s[u % 2, 2 * BLOCK:, :], next_cap)]
            sink = sink_ref[kh:kh + 1, :]
            s_max = jnp.maximum(jnp.maximum(s_blocks[0], s_blocks[1]), s_blocks[2])
            m = jnp.maximum(jnp.max(s_max, axis=0, keepdims=True), sink)
            p = jnp.concatenate([jnp.exp2(sb - m) for sb in s_blocks], axis=0).astype(BF16)
            v_ext = jnp.concatenate([vTj[kh * HEAD_DIM:(kh + 1) * HEAD_DIM, :], ones], axis=0)
            o_ext = jnp.dot(v_ext, p, preferred_element_type=F32)
            denom = o_ext[HEAD_DIM:HEAD_DIM + 1, :] + jnp.exp2(sink - m)
            oT = o_ext[:HEAD_DIM, :] / denom
            outs += [oT[:, g * BLOCK:(g + 1) * BLOCK] for g in range(GQA_GROUP)]
        oT_all = jnp.concatenate(outs, axis=0)
        inv = lax.rsqrt(jnp.mean(oT_all * oT_all, axis=0, keepdims=True) + EPS)
        o_ref[j * BLOCK:(j + 1) * BLOCK, :] = ((oT_all * inv).T * g_ref[...]).astype(BF16)


def _attention(q, k, vT, sink_rows, g, B, S, tq):
    T = q.shape[0]
    nq = S // tq
    sub = tq // BLOCK
    nblk = S // BLOCK
    main = lambda b, i: (b * nq + i, 0)
    prev = lambda b, i: (b * nblk + jnp.maximum(i * sub - 1, 0), 0)
    nxt = lambda b, i: (b * nblk + jnp.minimum((i + 1) * sub, nblk - 1), 0)
    swap = lambda f: (lambda b, i: f(b, i)[::-1])
    const = lambda b, i: (0, 0)
    return pl.pallas_call(
        functools.partial(_attn_kernel, tq=tq),
        grid=(B, nq),
        in_specs=[
            pl.BlockSpec((KV_HEADS, GQA_GROUP * BLOCK), const),
            pl.BlockSpec((tq, ATTN_WIDTH), main),
            pl.BlockSpec((BLOCK, KV_COLS), prev),
            pl.BlockSpec((tq, KV_COLS), main),
            pl.BlockSpec((BLOCK, KV_COLS), nxt),
            pl.BlockSpec((KV_COLS, BLOCK), swap(prev)),
            pl.BlockSpec((KV_COLS, tq), swap(main)),
            pl.BlockSpec((KV_COLS, BLOCK), swap(nxt)),
            pl.BlockSpec((1, ATTN_WIDTH), const),
        ],
        out_specs=pl.BlockSpec((tq, ATTN_WIDTH), main),
        out_shape=jax.ShapeDtypeStruct((T, ATTN_WIDTH), BF16),
        scratch_shapes=[pltpu.VMEM((tq + 2 * BLOCK, KV_COLS), BF16),
                        pltpu.VMEM((KV_COLS, tq + 2 * BLOCK), BF16),
                        pltpu.VMEM((2, 3 * BLOCK, GQA_GROUP * BLOCK), F32)],
        compiler_params=pltpu.CompilerParams(
            dimension_semantics=("arbitrary", "arbitrary"), vmem_limit_bytes=VMEM_LIMIT),
        name="attention",
    )(sink_rows, q, k, k, k, vT, vT, vT, g)


def _split_terms(v):
    hi = v.astype(BF16).astype(F32)
    r1 = v - hi
    mid = r1.astype(BF16).astype(F32)
    lo = (r1 - mid).astype(BF16).astype(F32)
    return jnp.concatenate([hi, mid, lo], axis=0)


def _ssd_kernel(*refs, reverse, ts, tm, nb):
    if reverse:
        (xc_all, dtT_all_ref, biasT_ref, aT_ref, e_ref, z_all, yf_all, ng_ref,
         x_all, attn_all, wa_ref, ws_ref, g2_ref, perm_ref, x1_all, hout_all, h_all, o_all) = refs
    else:
        (xc_all, dtT_all_ref, biasT_ref, aT_ref, e_ref, dskip_ref, o_all, h_all) = refs
    t = pl.program_id(1)
    d0 = SSM_HEADS if reverse else 0

    @pl.when(t == 0)
    def _():
        h_all[...] = jnp.zeros_like(h_all)

    li = lax.broadcasted_iota(jnp.int32, (CHUNK, CHUNK), 0)
    si = lax.broadcasted_iota(jnp.int32, (CHUNK, CHUNK), 1)
    causal = (li <= si) if reverse else (li >= si)
    scan_op = jnp.where((li >= si) if reverse else (li <= si), 1.0, 0.0).astype(BF16)
    last = 0 if reverse else CHUNK - 1
    head_block = (lax.broadcasted_iota(jnp.int32, (SSM_HPG * CHUNK, GROUP_W), 0) // CHUNK
                  == lax.broadcasted_iota(jnp.int32, (SSM_HPG * CHUNK, GROUP_W), 1) // SSM_HEAD_DIM)
    nchunk = ts // CHUNK
    nterm = SPLIT_TERMS * DT_COLS

    def chunks_of(vT):
        return [vT[:, c * CHUNK:(c + 1) * CHUNK] for c in range(nchunk)]

    def sum_terms(m):
        return m[0:DT_COLS] + m[DT_COLS:2 * DT_COLS] + m[2 * DT_COLS:nterm]

    def expand(vT_c):
        terms = jnp.concatenate([_split_terms(v).T for v in vT_c], axis=0)
        return jnp.dot(terms.astype(BF16), e_ref[...], preferred_element_type=F32)

    def per_head_terms(dtT_ref):
        dtT_all = _softplus(dtT_ref[...] + biasT_ref[...])
        a_terms = jnp.concatenate([_split_terms(v) for v in chunks_of(dtT_all * aT_ref[...])], axis=0)
        cs = jnp.dot(a_terms.astype(BF16), scan_op, preferred_element_type=F32)
        dtT_c = chunks_of(dtT_all)
        acsT_c = [sum_terms(cs[c * nterm:(c + 1) * nterm]) for c in range(nchunk)]
        tgt_c = [(v * LOG2E).T for v in acsT_c]
        src_c = [acsT_c[c] * LOG2E - jnp.log2(dtT_c[c]) for c in range(nchunk)]
        decayT_c = [dtT_c[c] * jnp.exp(acsT_c[c][:, last:last + 1] - acsT_c[c]) for c in range(nchunk)]
        exp_acs_all = expand([jnp.exp(v) for v in acsT_c])
        decay_all = expand(decayT_c)
        return tgt_c, src_c, exp_acs_all, decay_all

    terms = [per_head_terms(dtT_all_ref.at[bi]) for bi in range(nb)]

    def project_block(bi, k):
        x_ref, attn_ref, x1_ref, hout_ref, o_ref = (r.at[bi] for r in (x_all, attn_all, x1_all, hout_all, o_all))
        rows = slice(k * tm, (k + 1) * tm)
        x1 = (x_ref[rows, :]
              + jnp.dot(attn_ref[rows, :], wa_ref[...], preferred_element_type=F32)
              + jnp.dot(o_ref[rows, :], ws_ref[...], preferred_element_type=F32))
        x1_ref[rows, :] = x1
        h = _rms(x1, g2_ref[...]).astype(BF16)
        hout_ref[rows, :] = jnp.dot(perm_ref[...], h, preferred_element_type=F32).astype(BF16)

    def scan_chunk(bi, c):
        xc_ref, o_ref, h_s = xc_all.at[bi], o_all.at[bi], h_all.at[bi]
        tgt_c, src_c, exp_acs_all, decay_all = terms[bi]
        r0 = c * CHUNK
        xc = xc_ref[r0:r0 + CHUNK, :]
        x_b = xc[:, :SSM_INNER]
        xs = x_b.astype(F32)
        bm = xc[:, SSM_INNER:SSM_INNER + SSM_GROUPS * D_STATE]
        cm = xc[:, SSM_INNER + SSM_GROUPS * D_STATE:]

        tgt, src = tgt_c[c], src_c[c]
        exp_acs = exp_acs_all[r0:r0 + CHUNK, :]
        x_decay = (xs * decay_all[r0:r0 + CHUNK, :]).astype(BF16)
        chunk_decay = exp_acs[last:last + 1, :]

        ys = []
        for g in range(SSM_GROUPS):
            bg = bm[:, g * D_STATE:(g + 1) * D_STATE]
            cg = cm[:, g * D_STATE:(g + 1) * D_STATE]
            cb = lax.dot_general(cg, bg, (((1,), (1,)), ((), ())), preferred_element_type=F32)
            gs = slice(g * GROUP_W, (g + 1) * GROUP_W)
            bgT = bg.astype(F32).T.astype(BF16)
            st = jnp.dot(bgT, x_decay[:, gs], preferred_element_type=F32)
            h_prev = h_s[g]
            y_off = jnp.dot(cg, h_prev.astype(BF16), preferred_element_type=F32)
            h_s[g] = h_prev * chunk_decay[:, gs] + st
            weights = []
            for rr in range(SSM_HPG):
                r = g * SSM_HPG + rr
                seg = tgt[:, d0 + r:d0 + r + 1] - src[d0 + r:d0 + r + 1, :]
                lmat = jnp.exp2(jnp.where(causal, seg, -jnp.inf))
                weights.append((cb * lmat).astype(BF16))
            x_heads = jnp.where(head_block, jnp.concatenate([x_b[:, gs]] * SSM_HPG, axis=0), 0)
            y_diag = jnp.dot(jnp.concatenate(weights, axis=1), x_heads, preferred_element_type=F32)
            ys.append(y_diag + y_off * exp_acs[:, gs])
        y = jnp.concatenate(ys, axis=1)

        if not reverse:
            o_ref[r0:r0 + CHUNK, :] = y + xs * dskip_ref[...]
        else:
            yf_ref, z_ref = yf_all.at[bi], z_all.at[bi]
            y = (y + yf_ref[r0:r0 + CHUNK, :]) * _silu(z_ref[r0:r0 + CHUNK, :].astype(F32))
            parts = []
            for g in range(SSM_GROUPS):
                yg = y[:, g * GROUP_W:(g + 1) * GROUP_W]
                parts.append(yg * lax.rsqrt(jnp.mean(yg * yg, axis=-1, keepdims=True) + EPS))
            o_ref[r0:r0 + CHUNK, :] = (jnp.concatenate(parts, axis=1) * ng_ref[...]).astype(BF16)

            if r0 % tm == 0:
                project_block(bi, r0 // tm)

    for c in (range(nchunk - 1, -1, -1) if reverse else range(nchunk)):
        for bi in range(nb):
            scan_chunk(bi, c)


def _ssd_pass(xc, dtT, biasT, aT, e, extra, B, S, ts, tm, nb, reverse):
    nt = S // ts
    seq = (lambda t: nt - 1 - t) if reverse else (lambda t: t)
    main = lambda b, t: (b, seq(t), 0)
    const = lambda b, t: (0, 0)
    tok = lambda cols: pl.BlockSpec((nb, ts, cols), main)
    rows3 = lambda a: a.reshape(B, S, a.shape[-1])
    state = pltpu.VMEM((nb, SSM_GROUPS, D_STATE, GROUP_W), F32)
    in_specs = [
        tok(XBC_COLS),
        pl.BlockSpec((nb, DT_COLS, ts), lambda b, t: (b, 0, seq(t))),
        pl.BlockSpec((DT_COLS, 1), const),
        pl.BlockSpec((DT_COLS, 1), const),
        pl.BlockSpec((SPLIT_TERMS * DT_COLS, SSM_INNER), const),
    ]
    args = [rows3(xc), dtT, biasT, aT, e]
    if reverse:
        z, yf, ng, x2d, attn, w_attn, w_ssd, g2 = extra
        in_specs += [tok(SSM_INNER), tok(SSM_INNER), pl.BlockSpec((1, SSM_INNER), const),
                     tok(D_MODEL), tok(ATTN_WIDTH),
                     pl.BlockSpec((ATTN_WIDTH, D_MODEL), const), pl.BlockSpec((SSM_INNER, D_MODEL), const),
                     pl.BlockSpec((1, D_MODEL), const), pl.BlockSpec((tm, tm), const)]
        perm = _interleave_tokens(jnp.eye(tm, dtype=F32)).astype(BF16)
        args += [rows3(z), yf, ng, rows3(x2d), rows3(attn), w_attn, w_ssd, g2, perm]
        out_specs = [tok(D_MODEL), tok(D_MODEL)]
        out_shape = [jax.ShapeDtypeStruct((B, S, D_MODEL), F32), jax.ShapeDtypeStruct((B, S, D_MODEL), BF16)]
        scratch = [state, pltpu.VMEM((nb, ts, SSM_INNER), BF16)]
    else:
        (dskip,) = extra
        in_specs += [pl.BlockSpec((1, SSM_INNER), const)]
        args += [dskip]
        out_specs = tok(SSM_INNER)
        out_shape = jax.ShapeDtypeStruct((B, S, SSM_INNER), F32)
        scratch = [state]
    out = pl.pallas_call(
        functools.partial(_ssd_kernel, reverse=reverse, ts=ts, tm=tm, nb=nb),
        grid=(B // nb, nt),
        in_specs=in_specs,
        out_specs=out_specs,
        out_shape=out_shape,
        scratch_shapes=scratch,
        compiler_params=pltpu.CompilerParams(
            dimension_semantics=("arbitrary", "arbitrary"), vmem_limit_bytes=VMEM_LIMIT),
        name="ssd_bwd" if reverse else "ssd_fwd",
    )(*args)
    return [o.reshape(B * S, D_MODEL) for o in out] if reverse else out


FFN_COLS = 256


def _ffn_kernel(hp_ref, hm_ref, hn_ref, x1_ref, wup_ref, cw_ref, cb_ref, wdn_ref, fg_ref, o_ref,
                hext_s, u_s, glu_s, d_s, *, tm, n_seq_tiles):
    tile = pl.program_id(0) % n_seq_tiles
    nv = tm // SUBLANE
    prev_row = tm + HALO - 1
    next_row = tm
    hext_s[0:tm, :] = hm_ref[...]
    before = jnp.where(tile > 0, hp_ref[...], jnp.zeros_like(hp_ref))
    after = jnp.where(tile < n_seq_tiles - 1, hn_ref[...], jnp.zeros_like(hn_ref))
    halo_row = lax.broadcasted_iota(jnp.int32, (HALO, D_MODEL), 0)
    hext_s[tm:, :] = jnp.where(halo_row >= HALO // 2, before, after)
    f = FFN_COLS
    nchunks = D_FF // f

    def up(c):
        hext = hext_s[...]
        for half, base in enumerate((0, D_FF)):
            u_s[c % 2, :, half * f:(half + 1) * f] = jnp.dot(
                hext, wup_ref[:, base + c * f:base + (c + 1) * f], preferred_element_type=F32)

    def conv(c, half, base):
        cols = slice(base + c * f, base + (c + 1) * f)
        lanes = slice(half * f, (half + 1) * f)
        u = lambda lo, hi: u_s[c % 2, lo:hi, lanes]
        first = jnp.concatenate([u(prev_row, prev_row + 1), u(tm - SUBLANE, tm - 1)], axis=0)
        last = jnp.concatenate([u(1, SUBLANE), u(next_row, next_row + 1)], axis=0)
        before = jnp.concatenate([first, u(0, tm - SUBLANE)], axis=0)
        after = jnp.concatenate([u(SUBLANE, tm), last], axis=0)
        return (cb_ref[:, cols] + before * cw_ref[0:1, cols] + u(0, tm) * cw_ref[1:2, cols]
                + after * cw_ref[2:3, cols])

    up(0)
    for c in range(nchunks):
        if c + 1 < nchunks:
            up(c + 1)
        glu_s[:, c * f:(c + 1) * f] = (_silu(conv(c, 0, 0)) * conv(c, 1, D_FF)).astype(BF16)
    _store_lane_blocks(d_s, jnp.dot(glu_s[...], wdn_ref[...], preferred_element_type=F32))
    for s in range(SUBLANE):
        rows = slice(s * nv, (s + 1) * nv)
        x2 = x1_ref[rows, :] + _load_strided_rows(d_s, s, nv, SUBLANE)
        o_ref[rows, :] = _rms(x2, fg_ref[...])


def _ffn(h2, x1, w_up, conv_w, conv_b, w_down, fg, S, tm):
    T = h2.shape[0]
    n_seq_tiles = S // tm
    hb = tm // HALO
    nh = T // HALO
    row = lambda i: (i, 0)
    const = lambda i: (0, 0)
    prev = lambda i: (jnp.maximum(i * hb - 1, 0), 0)
    nxt = lambda i: (jnp.minimum((i + 1) * hb, nh - 1), 0)
    return pl.pallas_call(
        functools.partial(_ffn_kernel, tm=tm, n_seq_tiles=n_seq_tiles),
        grid=(T // tm,),
        in_specs=[
            pl.BlockSpec((HALO, D_MODEL), prev),
            pl.BlockSpec((tm, D_MODEL), row),
            pl.BlockSpec((HALO, D_MODEL), nxt),
            pl.BlockSpec((tm, D_MODEL), row),
            pl.BlockSpec((D_MODEL, 2 * D_FF), const),
            pl.BlockSpec((FFN_CONV, 2 * D_FF), const),
            pl.BlockSpec((1, 2 * D_FF), const),
            pl.BlockSpec((D_FF, D_MODEL), const),
            pl.BlockSpec((1, D_MODEL), const),
        ],
        out_specs=pl.BlockSpec((tm, D_MODEL), row),
        out_shape=jax.ShapeDtypeStruct((T, D_MODEL), F32),
        scratch_shapes=[pltpu.VMEM((tm + HALO, D_MODEL), BF16),
                        pltpu.VMEM((2, tm + HALO, 2 * FFN_COLS), F32),
                        pltpu.VMEM((tm, D_FF), BF16),
                        pltpu.VMEM((D_MODEL // LANE, tm, LANE), F32)],
        compiler_params=pltpu.CompilerParams(
            dimension_semantics=("arbitrary",), vmem_limit_bytes=VMEM_LIMIT),
        name="ffn",
    )(h2, h2, h2, x1, w_up, conv_w, conv_b, w_down, fg)


def _layer(x2d, B, S, norm1_g, w_in, attn_sink, attn_out_g, ssd_conv_w, ssd_conv_b,
           dt_bias_f, dt_bias_b, a_log_f, a_log_b, ssd_d, ssd_norm_g, w_out, norm2_g,
           w_up, ffn_conv_w, ffn_conv_b, w_down, out_g, tables, tm, ts, tq):
    row = lambda v: v.reshape(1, -1).astype(F32)
    w_qkvz = w_in[:, :QKVZ_COLS].astype(BF16)
    w_xbc = w_in[:, QKVZ_COLS:QKVZ_COLS + XBC_COLS].astype(BF16)
    w_dt = jnp.pad(w_in[:, QKVZ_COLS + XBC_COLS:], ((0, 0), (0, LANE - DT_COLS))).astype(BF16)
    q, k, vT, z, xc, dtT = _in_proj(x2d, row(norm1_g), w_qkvz, w_xbc, w_dt,
                                    ssd_conv_w.astype(F32), row(ssd_conv_b), tables, S, tm)

    sink_rows = jnp.repeat(attn_sink.astype(F32) * LOG2E, BLOCK).reshape(KV_HEADS, GQA_GROUP * BLOCK)
    attn = _attention(q, k, vT, sink_rows, row(attn_out_g), B, S, tq)

    bias = jnp.concatenate([dt_bias_f, dt_bias_b]).astype(F32)
    a = -jnp.exp(jnp.concatenate([a_log_f, a_log_b]).astype(F32))
    head_of_col = jnp.arange(SSM_INNER) // SSM_HEAD_DIM
    w_o = w_out.astype(BF16)
    out = None
    for reverse in (False, True):
        d0 = SSM_HEADS if reverse else 0
        term_row = jnp.arange(SPLIT_TERMS * DT_COLS) % DT_COLS
        e = (term_row[:, None] == (head_of_col + d0)[None, :]).astype(BF16)
        if reverse:
            extra = (z, out, row(ssd_norm_g), x2d, attn, w_o[:ATTN_WIDTH], w_o[ATTN_WIDTH:], row(norm2_g))
        else:
            extra = (row(jnp.repeat(ssd_d.astype(F32), SSM_HEAD_DIM)),)
        out = _ssd_pass(xc, dtT, bias.reshape(-1, 1), a.reshape(-1, 1), e, extra, B, S, ts, tm,
                        SSD_ROWS, reverse)
    x1, h2 = out
    return _ffn(h2, x1, w_up.astype(BF16), ffn_conv_w.astype(F32), row(ffn_conv_b),
                w_down.astype(BF16), out_g, S, tm)


def kernel(x, norm1_g, w_in, attn_sink, attn_out_g, ssd_conv_w, ssd_conv_b, ssd_dt_bias_fwd,
           ssd_dt_bias_bwd, ssd_a_log_fwd, ssd_a_log_bwd, ssd_d, ssd_norm_g, w_out, norm2_g, w_up,
           ffn_conv_w, ffn_conv_b, w_down, final_norm_g, *, tm=512, ts=1024, tq=1024):
    B, S, _ = x.shape
    depth = w_in.shape[0]
    assert depth == 1, "the fused final RMSNorm assumes a single layer"
    tables = _rope_tables(S)
    x2d = x.reshape(B * S, D_MODEL)
    out = _layer(x2d, B, S, norm1_g[0], w_in[0], attn_sink[0], attn_out_g[0], ssd_conv_w[0], ssd_conv_b[0],
                 ssd_dt_bias_fwd[0], ssd_dt_bias_bwd[0], ssd_a_log_fwd[0], ssd_a_log_bwd[0], ssd_d[0],
                 ssd_norm_g[0], w_out[0], norm2_g[0], w_up[0], ffn_conv_w[0], ffn_conv_b[0], w_down[0],
                 final_norm_g.reshape(1, -1).astype(F32), tables, tm, ts, tq)
    return out.reshape(B, S, D_MODEL)
```

```python
import functools
import math

import jax
import jax.numpy as jnp
from jax import lax
from jax.experimental import pallas as pl
from jax.experimental.pallas import tpu as pltpu

D_MODEL = 1024
HEAD_DIM = 64
ATTN_WIDTH = 512
ATTN_HEADS = 8
KV_HEADS = 2
GQA_GROUP = 4
ROT_DIM = 16
ROPE_THETA = 500000.0
WINDOW = 128
BLOCK = 128
SSM_INNER = 512
SSM_HEAD_DIM = 64
SSM_HEADS = 8
SSM_GROUPS = 2
SSM_HPG = 4
D_STATE = 128
SSM_CONV = 5
CHUNK = 128
D_FF = 2816
FFN_CONV = 3
EPS = 1e-5
NEG = -1e30
LOG2E = math.log2(math.e)

KV_COLS = KV_HEADS * HEAD_DIM
XBC_COLS = SSM_INNER + 2 * SSM_GROUPS * D_STATE
QKVZ_COLS = ATTN_WIDTH + 2 * KV_COLS + SSM_INNER
DT_COLS = 2 * SSM_HEADS
SPLIT_TERMS = 3
GROUP_W = SSM_HPG * SSM_HEAD_DIM

LANE = 128
SUBLANE = 8
BF16_ROWS = 16
HALO = BF16_ROWS
assert WINDOW == BLOCK, "the banded attention masks assume one key block of reach on each side"
CONV_COLS = 256
SSD_ROWS = 1
VMEM_LIMIT = 56 * 1024 * 1024

F32 = jnp.float32
BF16 = jnp.bfloat16


def _rms(x, g):
    return x * lax.rsqrt(jnp.mean(x * x, axis=-1, keepdims=True) + EPS) * g


def _silu(x):
    return x * (1.0 / (1.0 + jnp.exp(-x)))


def _softplus(x):
    return jnp.maximum(x, 0.0) + jnp.log1p(jnp.exp(-jnp.abs(x)))


def _interleave_tokens(x):
    tm, cols = x.shape
    return jnp.swapaxes(x.reshape(SUBLANE, tm // SUBLANE, cols), 0, 1).reshape(tm, cols)


def _store_lane_blocks(ref, val):
    for j in range(ref.shape[0]):
        ref[j] = val[:, j * LANE:(j + 1) * LANE]


def _load_strided_rows(ref, start, size, stride):
    return jnp.concatenate([ref[j, pl.ds(start, size, stride=stride), :] for j in range(ref.shape[0])], axis=1)


def _inproj_kernel(xp_ref, x_ref, xn_ref, g_ref, w_ref, wx_ref, wdt_ref, cw_ref, cb_ref,
                   cos_ref, sa_ref, sb_ref,
                   q_ref, k_ref, vT_ref, z_ref, xc_ref, dtT_ref, hext_s, xbc_s, proj_s, xcp_s,
                   *, tm, n_seq_tiles):
    tile = pl.program_id(0) % n_seq_tiles
    nv = tm // SUBLANE
    norm = lambda r: _rms(r[...], g_ref[...]).astype(BF16)
    hf = _rms(x_ref[...], g_ref[...])
    hb = hf.astype(BF16)
    hext_s[0:tm, :] = _interleave_tokens(hf).astype(BF16)
    before = jnp.where(tile > 0, norm(xp_ref), jnp.zeros((HALO, D_MODEL), BF16))
    after = jnp.where(tile < n_seq_tiles - 1, norm(xn_ref), jnp.zeros((HALO, D_MODEL), BF16))
    halo_row = lax.broadcasted_iota(jnp.int32, (HALO, D_MODEL), 0)
    hext_s[tm:, :] = jnp.where(halo_row >= HALO // 2, before, after)

    f = CONV_COLS
    nchunks = XBC_COLS // f

    def xbc_proj(c):
        xbc_s[c % 2] = jnp.dot(hext_s[...], wx_ref[:, c * f:(c + 1) * f], preferred_element_type=F32)

    def conv(c):
        cols = slice(c * f, (c + 1) * f)
        u = lambda lo, hi: xbc_s[c % 2, lo:hi, :]
        tok = lambda t: u(tm + t % HALO, tm + t % HALO + 1)
        cat = lambda *parts: jnp.concatenate(parts, axis=0)
        wrap_m1 = cat(tok(-1), u(tm - SUBLANE, tm - 1))
        wrap_m2 = cat(tok(-2), u(tm - 2 * SUBLANE, tm - SUBLANE - 1))
        wrap_p1 = cat(u(1, SUBLANE), tok(0))
        wrap_p2 = cat(u(SUBLANE + 1, 2 * SUBLANE), tok(1))
        taps = [cat(wrap_m2, wrap_m1, u(0, tm - 2 * SUBLANE)),
                cat(wrap_m1, u(0, tm - SUBLANE)),
                u(0, tm),
                cat(u(SUBLANE, tm), wrap_p1),
                cat(u(2 * SUBLANE, tm), wrap_p1, wrap_p2)]
        acc = cb_ref[:, cols]
        for kk in range(SSM_CONV):
            acc = acc + taps[kk] * cw_ref[kk:kk + 1, cols]
        act = _silu(acc)
        for j in range(f // LANE):
            xcp_s[c * (f // LANE) + j] = act[:, j * LANE:(j + 1) * LANE]

    cos, sa, sb = cos_ref[...], sa_ref[...], sb_ref[...]

    def rope(t):
        n = t.shape[1]
        reps = n // LANE
        cc = jnp.concatenate([cos] * reps, axis=1) if reps > 1 else cos
        aa = jnp.concatenate([sa] * reps, axis=1) if reps > 1 else sa
        bb = jnp.concatenate([sb] * reps, axis=1) if reps > 1 else sb
        half = ROT_DIM // 2
        return t * cc + pltpu.roll(t, n - half, 1) * aa + pltpu.roll(t, half, 1) * bb

    assert nchunks == 4
    qk = ATTN_WIDTH + KV_COLS
    xbc_proj(0)
    proj_s[:, :qk] = jnp.dot(hb, w_ref[:, :qk], preferred_element_type=F32)
    xbc_proj(1)
    conv(0)
    q_ref[...] = (rope(proj_s[:, :ATTN_WIDTH]) * (LOG2E / math.sqrt(HEAD_DIM))).astype(BF16)
    k_ref[...] = rope(proj_s[:, ATTN_WIDTH:qk]).astype(BF16)
    xbc_proj(2)
    conv(1)
    proj_s[:, qk:QKVZ_COLS] = jnp.dot(hb, w_ref[:, qk:], preferred_element_type=F32)
    proj_s[:, QKVZ_COLS:] = jnp.dot(hb, wdt_ref[...], preferred_element_type=F32)
    xbc_proj(3)
    conv(2)
    vT_ref[...] = proj_s[:, qk:qk + KV_COLS].T.astype(BF16)
    z_ref[...] = proj_s[:, qk + KV_COLS:QKVZ_COLS].astype(BF16)
    dtT_ref[...] = proj_s[:, QKVZ_COLS:].T[:DT_COLS, :]
    conv(3)
    for s in range(SUBLANE):
        xc_ref[s * nv:(s + 1) * nv, :] = _load_strided_rows(xcp_s, s, nv, SUBLANE).astype(BF16)


def _rope_tables(S):
    half = ROT_DIM // 2
    pos = jnp.arange(S, dtype=F32)
    inv = ROPE_THETA ** (-jnp.arange(0, ROT_DIM, 2, dtype=F32) / ROT_DIM)
    ang = pos[:, None] * inv[None, :]
    cos, sin = jnp.cos(ang), jnp.sin(ang)
    rest = HEAD_DIM - ROT_DIM
    ones, zeros, zh = jnp.ones((S, rest), F32), jnp.zeros((S, rest), F32), jnp.zeros((S, half), F32)
    c = jnp.concatenate([cos, cos, ones], axis=1)
    sa = jnp.concatenate([-sin, zh, zeros], axis=1)
    sb = jnp.concatenate([zh, sin, zeros], axis=1)
    rep = LANE // HEAD_DIM
    return tuple(jnp.tile(t, (1, rep)) for t in (c, sa, sb))


def _in_proj(x2d, g1, w_qkvz, w_xbc, w_dt, conv_w, conv_b, tables, S, tm):
    T = x2d.shape[0]
    n_seq_tiles = S // tm
    hb = tm // HALO
    nh = T // HALO
    row = lambda i: (i, 0)
    col = lambda i: (0, i)
    const = lambda i: (0, 0)
    prev = lambda i: (jnp.maximum(i * hb - 1, 0), 0)
    nxt = lambda i: (jnp.minimum((i + 1) * hb, nh - 1), 0)
    tab = lambda i: (i % n_seq_tiles, 0)
    return pl.pallas_call(
        functools.partial(_inproj_kernel, tm=tm, n_seq_tiles=n_seq_tiles),
        grid=(T // tm,),
        in_specs=[
            pl.BlockSpec((HALO, D_MODEL), prev),
            pl.BlockSpec((tm, D_MODEL), row),
            pl.BlockSpec((HALO, D_MODEL), nxt),
            pl.BlockSpec((1, D_MODEL), const),
            pl.BlockSpec((D_MODEL, QKVZ_COLS), const),
            pl.BlockSpec((D_MODEL, XBC_COLS), const),
            pl.BlockSpec((D_MODEL, LANE), const),
            pl.BlockSpec((SSM_CONV, XBC_COLS), const),
            pl.BlockSpec((1, XBC_COLS), const),
            pl.BlockSpec((tm, LANE), tab),
            pl.BlockSpec((tm, LANE), tab),
            pl.BlockSpec((tm, LANE), tab),
        ],
        out_specs=[
            pl.BlockSpec((tm, ATTN_WIDTH), row),
            pl.BlockSpec((tm, KV_COLS), row),
            pl.BlockSpec((KV_COLS, tm), col),
            pl.BlockSpec((tm, SSM_INNER), row),
            pl.BlockSpec((tm, XBC_COLS), row),
            pl.BlockSpec((None, DT_COLS, tm), lambda i: (i // n_seq_tiles, 0, i % n_seq_tiles)),
        ],
        out_shape=[
            jax.ShapeDtypeStruct((T, ATTN_WIDTH), BF16),
            jax.ShapeDtypeStruct((T, KV_COLS), BF16),
            jax.ShapeDtypeStruct((KV_COLS, T), BF16),
            jax.ShapeDtypeStruct((T, SSM_INNER), BF16),
            jax.ShapeDtypeStruct((T, XBC_COLS), BF16),
            jax.ShapeDtypeStruct((T // S, DT_COLS, S), F32),
        ],
        scratch_shapes=[pltpu.VMEM((tm + HALO, D_MODEL), BF16),
                        pltpu.VMEM((2, tm + HALO, CONV_COLS), F32),
                        pltpu.VMEM((tm, QKVZ_COLS + LANE), F32),
                        pltpu.VMEM((XBC_COLS // LANE, tm, LANE), F32)],
        compiler_params=pltpu.CompilerParams(
            dimension_semantics=("arbitrary",), vmem_limit_bytes=VMEM_LIMIT),
        name="in_proj",
    )(x2d, x2d, x2d, g1, w_qkvz, w_xbc, w_dt, conv_w, conv_b, *tables)


def _attn_kernel(sink_ref, q_ref, kp_ref, km_ref, kn_ref, vp_ref, vm_ref, vn_ref, g_ref, o_ref,
                 k_s, vT_s, sT_s, *, tq):
    i = pl.program_id(1)
    n = pl.num_programs(1)
    k_s[0:BLOCK, :] = kp_ref[...]
    k_s[BLOCK:BLOCK + tq, :] = km_ref[...]
    k_s[BLOCK + tq:, :] = kn_ref[...]
    vT_s[:, 0:BLOCK] = vp_ref[...]
    vT_s[:, BLOCK:BLOCK + tq] = vm_ref[...]
    vT_s[:, BLOCK + tq:] = vn_ref[...]

    band = 3 * BLOCK
    nq = GQA_GROUP * BLOCK
    key = lax.broadcasted_iota(jnp.int32, (BLOCK, nq), 0)
    qry = lax.broadcasted_iota(jnp.int32, (BLOCK, nq), 1) % BLOCK
    no_prev = jnp.where(i > 0, 0, BLOCK)
    no_next = jnp.where(i < n - 1, 0, BLOCK)
    cap = lambda ok: jnp.where(ok, jnp.inf, NEG).astype(F32)
    cap_prev, cap_prev_edge = cap(key >= qry), cap(key >= qry + no_prev)
    cap_next, cap_next_edge = cap(key <= qry), cap(key <= qry - no_next)
    ones = jnp.ones((BF16_ROWS, band), BF16)

    nsub = tq // BLOCK
    units = [(j, kh) for j in range(nsub) for kh in range(KV_HEADS)]

    def scores(u):
        j, kh = units[u]
        qj = q_ref[j * BLOCK:(j + 1) * BLOCK, :]
        kk = k_s[j * BLOCK:j * BLOCK + band, kh * HEAD_DIM:(kh + 1) * HEAD_DIM]
        qs = jnp.concatenate([qj[:, (kh * GQA_GROUP + g) * HEAD_DIM:(kh * GQA_GROUP + g + 1) * HEAD_DIM]
                              for g in range(GQA_GROUP)], axis=0)
        sT_s[u % 2] = lax.dot_general(kk, qs, (((1,), (1,)), ((), ())), preferred_element_type=F32)

    scores(0)
    for j in range(nsub):
        prev_cap = cap_prev_edge if j == 0 else cap_prev
        next_cap = cap_next_edge if j == nsub - 1 else cap_next
        vTj = vT_s[:, j * BLOCK:j * BLOCK + band]
        outs = []
        for kh in range(KV_HEADS):
            u = j * KV_HEADS + kh
            if u + 1 < len(units):
                scores(u + 1)
            s_blocks = [jnp.minimum(sT_s[u % 2, :BLOCK, :], prev_cap), sT_s[u % 2, BLOCK:2 * BLOCK, :],
                        jnp.minimum(sT_s[u % 2, 2 * BLOCK:, :], next_cap)]
            sink = sink_ref[kh:kh + 1, :]
            s_max = jnp.maximum(jnp.maximum(s_blocks[0], s_blocks[1]), s_blocks[2])
            m = jnp.maximum(jnp.max(s_max, axis=0, keepdims=True), sink)
            p = jnp.concatenate([jnp.exp2(sb - m) for sb in s_blocks], axis=0).astype(BF16)
            v_ext = jnp.concatenate([vTj[kh * HEAD_DIM:(kh + 1) * HEAD_DIM, :], ones], axis=0)
            o_ext = jnp.dot(v_ext, p, preferred_element_type=F32)
            denom = o_ext[HEAD_DIM:HEAD_DIM + 1, :] + jnp.exp2(sink - m)
            oT = o_ext[:HEAD_DIM, :] / denom
            outs += [oT[:, g * BLOCK:(g + 1) * BLOCK] for g in range(GQA_GROUP)]
        oT_all = jnp.concatenate(outs, axis=0)
        inv = lax.rsqrt(jnp.mean(oT_all * oT_all, axis=0, keepdims=True) + EPS)
        o_ref[j * BLOCK:(j + 1) * BLOCK, :] = ((oT_all * inv).T * g_ref[...]).astype(BF16)


def _attention(q, k, vT, sink_rows, g, B, S, tq):
    T = q.shape[0]
    nq = S // tq
    sub = tq // BLOCK
    nblk = S // BLOCK
    main = lambda b, i: (b * nq + i, 0)
    prev = lambda b, i: (b * nblk + jnp.maximum(i * sub - 1, 0), 0)
    nxt = lambda b, i: (b * nblk + jnp.minimum((i + 1) * sub, nblk - 1), 0)
    swap = lambda f: (lambda b, i: f(b, i)[::-1])
    const = lambda b, i: (0, 0)
    return pl.pallas_call(
        functools.partial(_attn_kernel, tq=tq),
        grid=(B, nq),
        in_specs=[
            pl.BlockSpec((KV_HEADS, GQA_GROUP * BLOCK), const),
            pl.BlockSpec((tq, ATTN_WIDTH), main),
            pl.BlockSpec((BLOCK, KV_COLS), prev),
            pl.BlockSpec((tq, KV_COLS), main),
            pl.BlockSpec((BLOCK, KV_COLS), nxt),
            pl.BlockSpec((KV_COLS, BLOCK), swap(prev)),
            pl.BlockSpec((KV_COLS, tq), swap(main)),
            pl.BlockSpec((KV_COLS, BLOCK), swap(nxt)),
            pl.BlockSpec((1, ATTN_WIDTH), const),
        ],
        out_specs=pl.BlockSpec((tq, ATTN_WIDTH), main),
        out_shape=jax.ShapeDtypeStruct((T, ATTN_WIDTH), BF16),
        scratch_shapes=[pltpu.VMEM((tq + 2 * BLOCK, KV_COLS), BF16),
                        pltpu.VMEM((KV_COLS, tq + 2 * BLOCK), BF16),
                        pltpu.VMEM((2, 3 * BLOCK, GQA_GROUP * BLOCK), F32)],
        compiler_params=pltpu.CompilerParams(
            dimension_semantics=("arbitrary", "arbitrary"), vmem_limit_bytes=VMEM_LIMIT),
        name="attention",
    )(sink_rows, q, k, k, k, vT, vT, vT, g)


def _split_terms(v):
    hi = v.astype(BF16).astype(F32)
    r1 = v - hi
    mid = r1.astype(BF16).astype(F32)
    lo = (r1 - mid).astype(BF16).astype(F32)
    return jnp.concatenate([hi, mid, lo], axis=0)


def _ssd_kernel(*refs, reverse, ts, tm, nb):
    if reverse:
        (xc_all, dtT_all_ref, biasT_ref, aT_ref, e_ref, z_all, yf_all, ng_ref,
         x_all, attn_all, wa_ref, ws_ref, g2_ref, perm_ref, x1_all, hout_all, h_all, o_all) = refs
    else:
        (xc_all, dtT_all_ref, biasT_ref, aT_ref, e_ref, dskip_ref, o_all, h_all) = refs
    t = pl.program_id(1)
    d0 = SSM_HEADS if reverse else 0

    @pl.when(t == 0)
    def _():
        h_all[...] = jnp.zeros_like(h_all)

    li = lax.broadcasted_iota(jnp.int32, (CHUNK, CHUNK), 0)
    si = lax.broadcasted_iota(jnp.int32, (CHUNK, CHUNK), 1)
    causal = (li <= si) if reverse else (li >= si)
    scan_op = jnp.where((li >= si) if reverse else (li <= si), 1.0, 0.0).astype(BF16)
    last = 0 if reverse else CHUNK - 1
    head_block = (lax.broadcasted_iota(jnp.int32, (SSM_HPG * CHUNK, GROUP_W), 0) // CHUNK
                  == lax.broadcasted_iota(jnp.int32, (SSM_HPG * CHUNK, GROUP_W), 1) // SSM_HEAD_DIM)
    nchunk = ts // CHUNK
    nterm = SPLIT_TERMS * DT_COLS

    def chunks_of(vT):
        return [vT[:, c * CHUNK:(c + 1) * CHUNK] for c in range(nchunk)]

    def sum_terms(m):
        return m[0:DT_COLS] + m[DT_COLS:2 * DT_COLS] + m[2 * DT_COLS:nterm]

    def expand(vT_c):
        terms = jnp.concatenate([_split_terms(v).T for v in vT_c], axis=0)
        return jnp.dot(terms.astype(BF16), e_ref[...], preferred_element_type=F32)

    def per_head_terms(dtT_ref):
        dtT_all = _softplus(dtT_ref[...] + biasT_ref[...])
        a_terms = jnp.concatenate([_split_terms(v) for v in chunks_of(dtT_all * aT_ref[...])], axis=0)
        cs = jnp.dot(a_terms.astype(BF16), scan_op, preferred_element_type=F32)
        dtT_c = chunks_of(dtT_all)
        acsT_c = [sum_terms(cs[c * nterm:(c + 1) * nterm]) for c in range(nchunk)]
        tgt_c = [(v * LOG2E).T for v in acsT_c]
        src_c = [acsT_c[c] * LOG2E - jnp.log2(dtT_c[c]) for c in range(nchunk)]
        decayT_c = [dtT_c[c] * jnp.exp(acsT_c[c][:, last:last + 1] - acsT_c[c]) for c in range(nchunk)]
        exp_acs_all = expand([jnp.exp(v) for v in acsT_c])
        decay_all = expand(decayT_c)
        return tgt_c, src_c, exp_acs_all, decay_all

    terms = [per_head_terms(dtT_all_ref.at[bi]) for bi in range(nb)]

    def project_block(bi, k):
        x_ref, attn_ref, x1_ref, hout_ref, o_ref = (r.at[bi] for r in (x_all, attn_all, x1_all, hout_all, o_all))
        rows = slice(k * tm, (k + 1) * tm)
        x1 = (x_ref[rows, :]
              + jnp.dot(attn_ref[rows, :], wa_ref[...], preferred_element_type=F32)
              + jnp.dot(o_ref[rows, :], ws_ref[...], preferred_element_type=F32))
        x1_ref[rows, :] = x1
        h = _rms(x1, g2_ref[...]).astype(BF16)
        hout_ref[rows, :] = jnp.dot(perm_ref[...], h, preferred_element_type=F32).astype(BF16)

    def scan_chunk(bi, c):
        xc_ref, o_ref, h_s = xc_all.at[bi], o_all.at[bi], h_all.at[bi]
        tgt_c, src_c, exp_acs_all, decay_all = terms[bi]
        r0 = c * CHUNK
        xc = xc_ref[r0:r0 + CHUNK, :]
        x_b = xc[:, :SSM_INNER]
        xs = x_b.astype(F32)
        bm = xc[:, SSM_INNER:SSM_INNER + SSM_GROUPS * D_STATE]
        cm = xc[:, SSM_INNER + SSM_GROUPS * D_STATE:]

        tgt, src = tgt_c[c], src_c[c]
        exp_acs = exp_acs_all[r0:r0 + CHUNK, :]
        x_decay = (xs * decay_all[r0:r0 + CHUNK, :]).astype(BF16)
        chunk_decay = exp_acs[last:last + 1, :]

        ys = []
        for g in range(SSM_GROUPS):
            bg = bm[:, g * D_STATE:(g + 1) * D_STATE]
            cg = cm[:, g * D_STATE:(g + 1) * D_STATE]
            cb = lax.dot_general(cg, bg, (((1,), (1,)), ((), ())), preferred_element_type=F32)
            gs = slice(g * GROUP_W, (g + 1) * GROUP_W)
            bgT = bg.astype(F32).T.astype(BF16)
            st = jnp.dot(bgT, x_decay[:, gs], preferred_element_type=F32)
            h_prev = h_s[g]
            y_off = jnp.dot(cg, h_prev.astype(BF16), preferred_element_type=F32)
            h_s[g] = h_prev * chunk_decay[:, gs] + st
            weights = []
            for rr in range(SSM_HPG):
                r = g * SSM_HPG + rr
                seg = tgt[:, d0 + r:d0 + r + 1] - src[d0 + r:d0 + r + 1, :]
                lmat = jnp.exp2(jnp.where(causal, seg, -jnp.inf))
                weights.append((cb * lmat).astype(BF16))
            x_heads = jnp.where(head_block, jnp.concatenate([x_b[:, gs]] * SSM_HPG, axis=0), 0)
            y_diag = jnp.dot(jnp.concatenate(weights, axis=1), x_heads, preferred_element_type=F32)
            ys.append(y_diag + y_off * exp_acs[:, gs])
        y = jnp.concatenate(ys, axis=1)

        if not reverse:
            o_ref[r0:r0 + CHUNK, :] = y + xs * dskip_ref[...]
        else:
            yf_ref, z_ref = yf_all.at[bi], z_all.at[bi]
            y = (y + yf_ref[r0:r0 + CHUNK, :]) * _silu(z_ref[r0:r0 + CHUNK, :].astype(F32))
            parts = []
            for g in range(SSM_GROUPS):
                yg = y[:, g * GROUP_W:(g + 1) * GROUP_W]
                parts.append(yg * lax.rsqrt(jnp.mean(yg * yg, axis=-1, keepdims=True) + EPS))
            o_ref[r0:r0 + CHUNK, :] = (jnp.concatenate(parts, axis=1) * ng_ref[...]).astype(BF16)

            if r0 % tm == 0:
                project_block(bi, r0 // tm)

    for c in (range(nchunk - 1, -1, -1) if reverse else range(nchunk)):
        for bi in range(nb):
            scan_chunk(bi, c)


def _ssd_pass(xc, dtT, biasT, aT, e, extra, B, S, ts, tm, nb, reverse):
    nt = S // ts
    seq = (lambda t: nt - 1 - t) if reverse else (lambda t: t)
    main = lambda b, t: (b, seq(t), 0)
    const = lambda b, t: (0, 0)
    tok = lambda cols: pl.BlockSpec((nb, ts, cols), main)
    rows3 = lambda a: a.reshape(B, S, a.shape[-1])
    state = pltpu.VMEM((nb, SSM_GROUPS, D_STATE, GROUP_W), F32)
    in_specs = [
        tok(XBC_COLS),
        pl.BlockSpec((nb, DT_COLS, ts), lambda b, t: (b, 0, seq(t))),
        pl.BlockSpec((DT_COLS, 1), const),
        pl.BlockSpec((DT_COLS, 1), const),
        pl.BlockSpec((SPLIT_TERMS * DT_COLS, SSM_INNER), const),
    ]
    args = [rows3(xc), dtT, biasT, aT, e]
    if reverse:
        z, yf, ng, x2d, attn, w_attn, w_ssd, g2 = extra
        in_specs += [tok(SSM_INNER), tok(SSM_INNER), pl.BlockSpec((1, SSM_INNER), const),
                     tok(D_MODEL), tok(ATTN_WIDTH),
                     pl.BlockSpec((ATTN_WIDTH, D_MODEL), const), pl.BlockSpec((SSM_INNER, D_MODEL), const),
                     pl.BlockSpec((1, D_MODEL), const), pl.BlockSpec((tm, tm), const)]
        perm = _interleave_tokens(jnp.eye(tm, dtype=F32)).astype(BF16)
        args += [rows3(z), yf, ng, rows3(x2d), rows3(attn), w_attn, w_ssd, g2, perm]
        out_specs = [tok(D_MODEL), tok(D_MODEL)]
        out_shape = [jax.ShapeDtypeStruct((B, S, D_MODEL), F32), jax.ShapeDtypeStruct((B, S, D_MODEL), BF16)]
        scratch = [state, pltpu.VMEM((nb, ts, SSM_INNER), BF16)]
    else:
        (dskip,) = extra
        in_specs += [pl.BlockSpec((1, SSM_INNER), const)]
        args += [dskip]
        out_specs = tok(SSM_INNER)
        out_shape = jax.ShapeDtypeStruct((B, S, SSM_INNER), F32)
        scratch = [state]
    out = pl.pallas_call(
        functools.partial(_ssd_kernel, reverse=reverse, ts=ts, tm=tm, nb=nb),
        grid=(B // nb, nt),
        in_specs=in_specs,
        out_specs=out_specs,
        out_shape=out_shape,
        scratch_shapes=scratch,
        compiler_params=pltpu.CompilerParams(
            dimension_semantics=("arbitrary", "arbitrary"), vmem_limit_bytes=VMEM_LIMIT),
        name="ssd_bwd" if reverse else "ssd_fwd",
    )(*args)
    return [o.reshape(B * S, D_MODEL) for o in out] if reverse else out


FFN_COLS = 256


def _ffn_kernel(hp_ref, hm_ref, hn_ref, x1_ref, wup_ref, cw_ref, cb_ref, wdn_ref, fg_ref, o_ref,
                hext_s, u_s, glu_s, d_s, *, tm, n_seq_tiles):
    tile = pl.program_id(0) % n_seq_tiles
    nv = tm // SUBLANE
    prev_row = tm + HALO - 1
    next_row = tm
    hext_s[0:tm, :] = hm_ref[...]
    before = jnp.where(tile > 0, hp_ref[...], jnp.zeros_like(hp_ref))
    after = jnp.where(tile < n_seq_tiles - 1, hn_ref[...], jnp.zeros_like(hn_ref))
    halo_row = lax.broadcasted_iota(jnp.int32, (HALO, D_MODEL), 0)
    hext_s[tm:, :] = jnp.where(halo_row >= HALO // 2, before, after)
    f = FFN_COLS
    nchunks = D_FF // f

    def up(c):
        hext = hext_s[...]
        for half, base in enumerate((0, D_FF)):
            u_s[c % 2, :, half * f:(half + 1) * f] = jnp.dot(
                hext, wup_ref[:, base + c * f:base + (c + 1) * f], preferred_element_type=F32)

    def conv(c, half, base):
        cols = slice(base + c * f, base + (c + 1) * f)
        lanes = slice(half * f, (half + 1) * f)
        u = lambda lo, hi: u_s[c % 2, lo:hi, lanes]
        first = jnp.concatenate([u(prev_row, prev_row + 1), u(tm - SUBLANE, tm - 1)], axis=0)
        last = jnp.concatenate([u(1, SUBLANE), u(next_row, next_row + 1)], axis=0)
        before = jnp.concatenate([first, u(0, tm - SUBLANE)], axis=0)
        after = jnp.concatenate([u(SUBLANE, tm), last], axis=0)
        return (cb_ref[:, cols] + before * cw_ref[0:1, cols] + u(0, tm) * cw_ref[1:2, cols]
                + after * cw_ref[2:3, cols])

    up(0)
    for c in range(nchunks):
        if c + 1 < nchunks:
            up(c + 1)
        glu_s[:, c * f:(c + 1) * f] = (_silu(conv(c, 0, 0)) * conv(c, 1, D_FF)).astype(BF16)
    _store_lane_blocks(d_s, jnp.dot(glu_s[...], wdn_ref[...], preferred_element_type=F32))
    for s in range(SUBLANE):
        rows = slice(s * nv, (s + 1) * nv)
        x2 = x1_ref[rows, :] + _load_strided_rows(d_s, s, nv, SUBLANE)
        o_ref[rows, :] = _rms(x2, fg_ref[...])


def _ffn(h2, x1, w_up, conv_w, conv_b, w_down, fg, S, tm):
    T = h2.shape[0]
    n_seq_tiles = S // tm
    hb = tm // HALO
    nh = T // HALO
    row = lambda i: (i, 0)
    const = lambda i: (0, 0)
    prev = lambda i: (jnp.maximum(i * hb - 1, 0), 0)
    nxt = lambda i: (jnp.minimum((i + 1) * hb, nh - 1), 0)
    return pl.pallas_call(
        functools.partial(_ffn_kernel, tm=tm, n_seq_tiles=n_seq_tiles),
        grid=(T // tm,),
        in_specs=[
            pl.BlockSpec((HALO, D_MODEL), prev),
            pl.BlockSpec((tm, D_MODEL), row),
            pl.BlockSpec((HALO, D_MODEL), nxt),
            pl.BlockSpec((tm, D_MODEL), row),
            pl.BlockSpec((D_MODEL, 2 * D_FF), const),
            pl.BlockSpec((FFN_CONV, 2 * D_FF), const),
            pl.BlockSpec((1, 2 * D_FF), const),
            pl.BlockSpec((D_FF, D_MODEL), const),
            pl.BlockSpec((1, D_MODEL), const),
        ],
        out_specs=pl.BlockSpec((tm, D_MODEL), row),
        out_shape=jax.ShapeDtypeStruct((T, D_MODEL), F32),
        scratch_shapes=[pltpu.VMEM((tm + HALO, D_MODEL), BF16),
                        pltpu.VMEM((2, tm + HALO, 2 * FFN_COLS), F32),
                        pltpu.VMEM((tm, D_FF), BF16),
                        pltpu.VMEM((D_MODEL // LANE, tm, LANE), F32)],
        compiler_params=pltpu.CompilerParams(
            dimension_semantics=("arbitrary",), vmem_limit_bytes=VMEM_LIMIT),
        name="ffn",
    )(h2, h2, h2, x1, w_up, conv_w, conv_b, w_down, fg)


def _layer(x2d, B, S, norm1_g, w_in, attn_sink, attn_out_g, ssd_conv_w, ssd_conv_b,
           dt_bias_f, dt_bias_b, a_log_f, a_log_b, ssd_d, ssd_norm_g, w_out, norm2_g,
           w_up, ffn_conv_w, ffn_conv_b, w_down, out_g, tables, tm, ts, tq):
    row = lambda v: v.reshape(1, -1).astype(F32)
    w_qkvz = w_in[:, :QKVZ_COLS].astype(BF16)
    w_xbc = w_in[:, QKVZ_COLS:QKVZ_COLS + XBC_COLS].astype(BF16)
    w_dt = jnp.pad(w_in[:, QKVZ_COLS + XBC_COLS:], ((0, 0), (0, LANE - DT_COLS))).astype(BF16)
    q, k, vT, z, xc, dtT = _in_proj(x2d, row(norm1_g), w_qkvz, w_xbc, w_dt,
                                    ssd_conv_w.astype(F32), row(ssd_conv_b), tables, S, tm)

    sink_rows = jnp.repeat(attn_sink.astype(F32) * LOG2E, BLOCK).reshape(KV_HEADS, GQA_GROUP * BLOCK)
    attn = _attention(q, k, vT, sink_rows, row(attn_out_g), B, S, tq)

    bias = jnp.concatenate([dt_bias_f, dt_bias_b]).astype(F32)
    a = -jnp.exp(jnp.concatenate([a_log_f, a_log_b]).astype(F32))
    head_of_col = jnp.arange(SSM_INNER) // SSM_HEAD_DIM
    w_o = w_out.astype(BF16)
    out = None
    for reverse in (False, True):
        d0 = SSM_HEADS if reverse else 0
        term_row = jnp.arange(SPLIT_TERMS * DT_COLS) % DT_COLS
        e = (term_row[:, None] == (head_of_col + d0)[None, :]).astype(BF16)
        if reverse:
            extra = (z, out, row(ssd_norm_g), x2d, attn, w_o[:ATTN_WIDTH], w_o[ATTN_WIDTH:], row(norm2_g))
        else:
            extra = (row(jnp.repeat(ssd_d.astype(F32), SSM_HEAD_DIM)),)
        out = _ssd_pass(xc, dtT, bias.reshape(-1, 1), a.reshape(-1, 1), e, extra, B, S,
                        ts if reverse else 2 * ts, tm, SSD_ROWS, reverse)
    x1, h2 = out
    return _ffn(h2, x1, w_up.astype(BF16), ffn_conv_w.astype(F32), row(ffn_conv_b),
                w_down.astype(BF16), out_g, S, tm)


def kernel(x, norm1_g, w_in, attn_sink, attn_out_g, ssd_conv_w, ssd_conv_b, ssd_dt_bias_fwd,
           ssd_dt_bias_bwd, ssd_a_log_fwd, ssd_a_log_bwd, ssd_d, ssd_norm_g, w_out, norm2_g, w_up,
           ffn_conv_w, ffn_conv_b, w_down, final_norm_g, *, tm=512, ts=1024, tq=2048):
    B, S, _ = x.shape
    depth = w_in.shape[0]
    assert depth == 1, "the fused final RMSNorm assumes a single layer"
    tables = _rope_tables(S)
    x2d = x.reshape(B * S, D_MODEL)
    out = _layer(x2d, B, S, norm1_g[0], w_in[0], attn_sink[0], attn_out_g[0], ssd_conv_w[0], ssd_conv_b[0],
                 ssd_dt_bias_fwd[0], ssd_dt_bias_bwd[0], ssd_a_log_fwd[0], ssd_a_log_bwd[0], ssd_d[0],
                 ssd_norm_g[0], w_out[0], norm2_g[0], w_up[0], ffn_conv_w[0], ffn_conv_b[0], w_down[0],
                 final_norm_g.reshape(1, -1).astype(F32), tables, tm, ts, tq)
    return out.reshape(B, S, D_MODEL)
```

```python
import functools
import math

import jax
import jax.numpy as jnp
from jax import lax
from jax.experimental import pallas as pl
from jax.experimental.pallas import tpu as pltpu

D_MODEL = 1024
HEAD_DIM = 64
ATTN_WIDTH = 512
KV_HEADS = 2
GQA_GROUP = 4
ROT_DIM = 16
ROPE_THETA = 500000.0
WINDOW = 128
BLOCK = 128
SSM_INNER = 512
SSM_HEAD_DIM = 64
SSM_HEADS = 8
SSM_GROUPS = 2
SSM_HPG = 4
D_STATE = 128
SSM_CONV = 5
CHUNK = 128
D_FF = 2816
FFN_CONV = 3
EPS = 1e-5
NEG = -1e30
LOG2E = math.log2(math.e)

KV_COLS = KV_HEADS * HEAD_DIM
XBC_COLS = SSM_INNER + 2 * SSM_GROUPS * D_STATE
QKVZ_COLS = ATTN_WIDTH + 2 * KV_COLS + SSM_INNER
DT_COLS = 2 * SSM_HEADS
SPLIT_TERMS = 3
GROUP_W = SSM_HPG * SSM_HEAD_DIM

LANE = 128
SUBLANE = 8
BF16_ROWS = 16
HALO = BF16_ROWS
assert WINDOW == BLOCK, "the banded attention masks assume one key block of reach on each side"
CONV_COLS = 256
SSD_ROWS = 1
VMEM_LIMIT = 56 * 1024 * 1024

F32 = jnp.float32
BF16 = jnp.bfloat16


def _rms(x, g):
    return x * lax.rsqrt(jnp.mean(x * x, axis=-1, keepdims=True) + EPS) * g


def _silu(x):
    return x * (1.0 / (1.0 + jnp.exp(-x)))


def _softplus(x):
    return jnp.maximum(x, 0.0) + jnp.log1p(jnp.exp(-jnp.abs(x)))


def _interleave_tokens(x):
    tm, cols = x.shape
    return jnp.swapaxes(x.reshape(SUBLANE, tm // SUBLANE, cols), 0, 1).reshape(tm, cols)


def _store_lane_blocks(ref, val):
    for j in range(ref.shape[0]):
        ref[j] = val[:, j * LANE:(j + 1) * LANE]


def _load_strided_rows(ref, start, size, stride):
    return jnp.concatenate([ref[j, pl.ds(start, size, stride=stride), :] for j in range(ref.shape[0])], axis=1)


def _inproj_kernel(xp_ref, x_ref, xn_ref, g_ref, w_ref, wx_ref, cw_ref, cb_ref,
                   cos_ref, sa_ref, sb_ref,
                   q_ref, k_ref, vT_ref, z_ref, xc_ref, dtT_ref, hext_s, xbc_s, proj_s, xcp_s,
                   *, tm, n_seq_tiles):
    tile = pl.program_id(0) % n_seq_tiles
    nv = tm // SUBLANE
    norm = lambda r: _rms(r[...], g_ref[...]).astype(BF16)
    hf = _rms(x_ref[...], g_ref[...])
    hb = hf.astype(BF16)
    hext_s[0:tm, :] = _interleave_tokens(hf).astype(BF16)
    before = jnp.where(tile > 0, norm(xp_ref), jnp.zeros((HALO, D_MODEL), BF16))
    after = jnp.where(tile < n_seq_tiles - 1, norm(xn_ref), jnp.zeros((HALO, D_MODEL), BF16))
    halo_row = lax.broadcasted_iota(jnp.int32, (HALO, D_MODEL), 0)
    hext_s[tm:, :] = jnp.where(halo_row >= HALO // 2, before, after)

    f = CONV_COLS
    nchunks = XBC_COLS // f

    def xbc_proj(c):
        xbc_s[c % 2] = jnp.dot(hext_s[...], wx_ref[:, c * f:(c + 1) * f], preferred_element_type=F32)

    def conv(c):
        cols = slice(c * f, (c + 1) * f)
        u = lambda lo, hi: xbc_s[c % 2, lo:hi, :]
        tok = lambda t: u(tm + t % HALO, tm + t % HALO + 1)
        cat = lambda *parts: jnp.concatenate(parts, axis=0)
        wrap_m1 = cat(tok(-1), u(tm - SUBLANE, tm - 1))
        wrap_m2 = cat(tok(-2), u(tm - 2 * SUBLANE, tm - SUBLANE - 1))
        wrap_p1 = cat(u(1, SUBLANE), tok(0))
        wrap_p2 = cat(u(SUBLANE + 1, 2 * SUBLANE), tok(1))
        taps = [cat(wrap_m2, wrap_m1, u(0, tm - 2 * SUBLANE)),
                cat(wrap_m1, u(0, tm - SUBLANE)),
                u(0, tm),
                cat(u(SUBLANE, tm), wrap_p1),
                cat(u(2 * SUBLANE, tm), wrap_p1, wrap_p2)]
        acc = cb_ref[:, cols]
        for kk in range(SSM_CONV):
            acc = acc + taps[kk] * cw_ref[kk:kk + 1, cols]
        act = _silu(acc)
        for j in range(f // LANE):
            xcp_s[c * (f // LANE) + j] = act[:, j * LANE:(j + 1) * LANE]

    cos, sa, sb = cos_ref[...], sa_ref[...], sb_ref[...]

    def rope(t):
        n = t.shape[1]
        reps = n // LANE
        cc = jnp.concatenate([cos] * reps, axis=1) if reps > 1 else cos
        aa = jnp.concatenate([sa] * reps, axis=1) if reps > 1 else sa
        bb = jnp.concatenate([sb] * reps, axis=1) if reps > 1 else sb
        half = ROT_DIM // 2
        return t * cc + pltpu.roll(t, n - half, 1) * aa + pltpu.roll(t, half, 1) * bb

    assert nchunks == 4
    qk = ATTN_WIDTH + KV_COLS
    xbc_proj(0)
    proj_s[:, :qk] = jnp.dot(hb, w_ref[:, :qk], preferred_element_type=F32)
    xbc_proj(1)
    conv(0)
    q_ref[...] = (rope(proj_s[:, :ATTN_WIDTH]) * (LOG2E / math.sqrt(HEAD_DIM))).astype(BF16)
    k_ref[...] = rope(proj_s[:, ATTN_WIDTH:qk]).astype(BF16)
    xbc_proj(2)
    conv(1)
    proj_s[:, qk:] = jnp.dot(hb, w_ref[:, qk:], preferred_element_type=F32)
    xbc_proj(3)
    conv(2)
    vT_ref[...] = proj_s[:, qk:qk + KV_COLS].T.astype(BF16)
    z_ref[...] = proj_s[:, qk + KV_COLS:QKVZ_COLS].astype(BF16)
    dtT_ref[...] = proj_s[:, QKVZ_COLS:].T[:DT_COLS, :]
    conv(3)
    for s in range(SUBLANE):
        xc_ref[s * nv:(s + 1) * nv, :] = _load_strided_rows(xcp_s, s, nv, SUBLANE).astype(BF16)


def _rope_tables(S):
    half = ROT_DIM // 2
    pos = jnp.arange(S, dtype=F32)
    inv = ROPE_THETA ** (-jnp.arange(0, ROT_DIM, 2, dtype=F32) / ROT_DIM)
    ang = pos[:, None] * inv[None, :]
    cos, sin = jnp.cos(ang), jnp.sin(ang)
    rest = HEAD_DIM - ROT_DIM
    ones, zeros, zh = jnp.ones((S, rest), F32), jnp.zeros((S, rest), F32), jnp.zeros((S, half), F32)
    c = jnp.concatenate([cos, cos, ones], axis=1)
    sa = jnp.concatenate([-sin, zh, zeros], axis=1)
    sb = jnp.concatenate([zh, sin, zeros], axis=1)
    rep = LANE // HEAD_DIM
    return tuple(jnp.tile(t, (1, rep)) for t in (c, sa, sb))


def _in_proj(x2d, g1, w_qkvzd, w_xbc, conv_w, conv_b, tables, S, tm):
    T = x2d.shape[0]
    n_seq_tiles = S // tm
    hb = tm // HALO
    nh = T // HALO
    row = lambda i: (i, 0)
    col = lambda i: (0, i)
    const = lambda i: (0, 0)
    prev = lambda i: (jnp.maximum(i * hb - 1, 0), 0)
    nxt = lambda i: (jnp.minimum((i + 1) * hb, nh - 1), 0)
    tab = lambda i: (i % n_seq_tiles, 0)
    return pl.pallas_call(
        functools.partial(_inproj_kernel, tm=tm, n_seq_tiles=n_seq_tiles),
        grid=(T // tm,),
        in_specs=[
            pl.BlockSpec((HALO, D_MODEL), prev),
            pl.BlockSpec((tm, D_MODEL), row),
            pl.BlockSpec((HALO, D_MODEL), nxt),
            pl.BlockSpec((1, D_MODEL), const),
            pl.BlockSpec((D_MODEL, QKVZ_COLS + LANE), const),
            pl.BlockSpec((D_MODEL, XBC_COLS), const),
            pl.BlockSpec((SSM_CONV, XBC_COLS), const),
            pl.BlockSpec((1, XBC_COLS), const),
            pl.BlockSpec((tm, LANE), tab),
            pl.BlockSpec((tm, LANE), tab),
            pl.BlockSpec((tm, LANE), tab),
        ],
        out_specs=[
            pl.BlockSpec((tm, ATTN_WIDTH), row),
            pl.BlockSpec((tm, KV_COLS), row),
            pl.BlockSpec((KV_COLS, tm), col),
            pl.BlockSpec((tm, SSM_INNER), row),
            pl.BlockSpec((tm, XBC_COLS), row),
            pl.BlockSpec((None, DT_COLS, tm), lambda i: (i // n_seq_tiles, 0, i % n_seq_tiles)),
        ],
        out_shape=[
            jax.ShapeDtypeStruct((T, ATTN_WIDTH), BF16),
            jax.ShapeDtypeStruct((T, KV_COLS), BF16),
            jax.ShapeDtypeStruct((KV_COLS, T), BF16),
            jax.ShapeDtypeStruct((T, SSM_INNER), BF16),
            jax.ShapeDtypeStruct((T, XBC_COLS), BF16),
            jax.ShapeDtypeStruct((T // S, DT_COLS, S), F32),
        ],
        scratch_shapes=[pltpu.VMEM((tm + HALO, D_MODEL), BF16),
                        pltpu.VMEM((2, tm + HALO, CONV_COLS), F32),
                        pltpu.VMEM((tm, QKVZ_COLS + LANE), F32),
                        pltpu.VMEM((XBC_COLS // LANE, tm, LANE), F32)],
        compiler_params=pltpu.CompilerParams(
            dimension_semantics=("arbitrary",), vmem_limit_bytes=VMEM_LIMIT),
        name="in_proj",
    )(x2d, x2d, x2d, g1, w_qkvzd, w_xbc, conv_w, conv_b, *tables)


def _attn_kernel(sink_ref, q_ref, kp_ref, km_ref, kn_ref, vp_ref, vm_ref, vn_ref, g_ref, o_ref,
                 k_s, vT_s, sT_s, *, tq):
    i = pl.program_id(1)
    n = pl.num_programs(1)
    k_s[0:BLOCK, :] = kp_ref[...]
    k_s[BLOCK:BLOCK + tq, :] = km_ref[...]
    k_s[BLOCK + tq:, :] = kn_ref[...]
    vT_s[:, 0:BLOCK] = vp_ref[...]
    vT_s[:, BLOCK:BLOCK + tq] = vm_ref[...]
    vT_s[:, BLOCK + tq:] = vn_ref[...]

    band = 3 * BLOCK
    nq = GQA_GROUP * BLOCK
    key = lax.broadcasted_iota(jnp.int32, (BLOCK, nq), 0)
    qry = lax.broadcasted_iota(jnp.int32, (BLOCK, nq), 1) % BLOCK
    no_prev = jnp.where(i > 0, 0, BLOCK)
    no_next = jnp.where(i < n - 1, 0, BLOCK)
    cap = lambda ok: jnp.where(ok, jnp.inf, NEG).astype(F32)
    cap_prev, cap_prev_edge = cap(key >= qry), cap(key >= qry + no_prev)
    cap_next, cap_next_edge = cap(key <= qry), cap(key <= qry - no_next)
    ones = jnp.ones((BF16_ROWS, band), BF16)

    nsub = tq // BLOCK
    units = [(j, kh) for j in range(nsub) for kh in range(KV_HEADS)]

    def scores(u):
        j, kh = units[u]
        qj = q_ref[j * BLOCK:(j + 1) * BLOCK, :]
        kk = k_s[j * BLOCK:j * BLOCK + band, kh * HEAD_DIM:(kh + 1) * HEAD_DIM]
        qs = jnp.concatenate([qj[:, (kh * GQA_GROUP + g) * HEAD_DIM:(kh * GQA_GROUP + g + 1) * HEAD_DIM]
                              for g in range(GQA_GROUP)], axis=0)
        sT_s[u % 2] = lax.dot_general(kk, qs, (((1,), (1,)), ((), ())), preferred_element_type=F32)

    scores(0)
    for j in range(nsub):
        prev_cap = cap_prev_edge if j == 0 else cap_prev
        next_cap = cap_next_edge if j == nsub - 1 else cap_next
        vTj = vT_s[:, j * BLOCK:j * BLOCK + band]
        outs = []
        for kh in range(KV_HEADS):
            u = j * KV_HEADS + kh
            if u + 1 < len(units):
                scores(u + 1)
            s_blocks = [jnp.minimum(sT_s[u % 2, :BLOCK, :], prev_cap), sT_s[u % 2, BLOCK:2 * BLOCK, :],
                        jnp.minimum(sT_s[u % 2, 2 * BLOCK:, :], next_cap)]
            sink = sink_ref[kh:kh + 1, :]
            s_max = jnp.maximum(jnp.maximum(s_blocks[0], s_blocks[1]), s_blocks[2])
            m = jnp.maximum(jnp.max(s_max, axis=0, keepdims=True), sink)
            p = jnp.concatenate([jnp.exp2(sb - m) for sb in s_blocks], axis=0).astype(BF16)
            v_ext = jnp.concatenate([vTj[kh * HEAD_DIM:(kh + 1) * HEAD_DIM, :], ones], axis=0)
            o_ext = jnp.dot(v_ext, p, preferred_element_type=F32)
            denom = o_ext[HEAD_DIM:HEAD_DIM + 1, :] + jnp.exp2(sink - m)
            oT = o_ext[:HEAD_DIM, :] / denom
            outs += [oT[:, g * BLOCK:(g + 1) * BLOCK] for g in range(GQA_GROUP)]
        oT_all = jnp.concatenate(outs, axis=0)
        inv = lax.rsqrt(jnp.mean(oT_all * oT_all, axis=0, keepdims=True) + EPS)
        o_ref[j * BLOCK:(j + 1) * BLOCK, :] = ((oT_all * inv).T * g_ref[...]).astype(BF16)


def _attention(q, k, vT, sink_rows, g, B, S, tq):
    T = q.shape[0]
    nq = S // tq
    sub = tq // BLOCK
    nblk = S // BLOCK
    main = lambda b, i: (b * nq + i, 0)
    prev = lambda b, i: (b * nblk + jnp.maximum(i * sub - 1, 0), 0)
    nxt = lambda b, i: (b * nblk + jnp.minimum((i + 1) * sub, nblk - 1), 0)
    swap = lambda f: (lambda b, i: f(b, i)[::-1])
    const = lambda b, i: (0, 0)
    return pl.pallas_call(
        functools.partial(_attn_kernel, tq=tq),
        grid=(B, nq),
        in_specs=[
            pl.BlockSpec((KV_HEADS, GQA_GROUP * BLOCK), const),
            pl.BlockSpec((tq, ATTN_WIDTH), main),
            pl.BlockSpec((BLOCK, KV_COLS), prev),
            pl.BlockSpec((tq, KV_COLS), main),
            pl.BlockSpec((BLOCK, KV_COLS), nxt),
            pl.BlockSpec((KV_COLS, BLOCK), swap(prev)),
            pl.BlockSpec((KV_COLS, tq), swap(main)),
            pl.BlockSpec((KV_COLS, BLOCK), swap(nxt)),
            pl.BlockSpec((1, ATTN_WIDTH), const),
        ],
        out_specs=pl.BlockSpec((tq, ATTN_WIDTH), main),
        out_shape=jax.ShapeDtypeStruct((T, ATTN_WIDTH), BF16),
        scratch_shapes=[pltpu.VMEM((tq + 2 * BLOCK, KV_COLS), BF16),
                        pltpu.VMEM((KV_COLS, tq + 2 * BLOCK), BF16),
                        pltpu.VMEM((2, 3 * BLOCK, GQA_GROUP * BLOCK), F32)],
        compiler_params=pltpu.CompilerParams(
            dimension_semantics=("arbitrary", "arbitrary"), vmem_limit_bytes=VMEM_LIMIT),
        name="attention",
    )(sink_rows, q, k, k, k, vT, vT, vT, g)


def _split_terms(v):
    hi = v.astype(BF16).astype(F32)
    r1 = v - hi
    mid = r1.astype(BF16).astype(F32)
    lo = (r1 - mid).astype(BF16).astype(F32)
    return jnp.concatenate([hi, mid, lo], axis=0)


def _ssd_kernel(*refs, reverse, ts, tm, nb):
    if reverse:
        (xc_all, dtT_all_ref, biasT_ref, aT_ref, e_ref, z_all, yf_all, ng_ref,
         x_all, attn_all, wa_ref, ws_ref, g2_ref, perm_ref, x1_all, hout_all, h_all, o_all) = refs
    else:
        (xc_all, dtT_all_ref, biasT_ref, aT_ref, e_ref, dskip_ref, o_all, h_all) = refs
    t = pl.program_id(1)
    d0 = SSM_HEADS if reverse else 0

    @pl.when(t == 0)
    def _():
        h_all[...] = jnp.zeros_like(h_all)

    li = lax.broadcasted_iota(jnp.int32, (CHUNK, CHUNK), 0)
    si = lax.broadcasted_iota(jnp.int32, (CHUNK, CHUNK), 1)
    causal = (li <= si) if reverse else (li >= si)
    scan_op = jnp.where((li >= si) if reverse else (li <= si), 1.0, 0.0).astype(BF16)
    last = 0 if reverse else CHUNK - 1
    head_block = (lax.broadcasted_iota(jnp.int32, (SSM_HPG * CHUNK, GROUP_W), 0) // CHUNK
                  == lax.broadcasted_iota(jnp.int32, (SSM_HPG * CHUNK, GROUP_W), 1) // SSM_HEAD_DIM)
    nchunk = ts // CHUNK
    nterm = SPLIT_TERMS * DT_COLS

    def chunks_of(vT):
        return [vT[:, c * CHUNK:(c + 1) * CHUNK] for c in range(nchunk)]

    def sum_terms(m):
        return m[0:DT_COLS] + m[DT_COLS:2 * DT_COLS] + m[2 * DT_COLS:nterm]

    def expand(vT_c):
        terms = jnp.concatenate([_split_terms(v).T for v in vT_c], axis=0)
        return jnp.dot(terms.astype(BF16), e_ref[...], preferred_element_type=F32)

    def per_head_terms(dtT_ref):
        dtT_all = _softplus(dtT_ref[...] + biasT_ref[...])
        a_terms = jnp.concatenate([_split_terms(v) for v in chunks_of(dtT_all * aT_ref[...])], axis=0)
        cs = jnp.dot(a_terms.astype(BF16), scan_op, preferred_element_type=F32)
        dtT_c = chunks_of(dtT_all)
        acsT_c = [sum_terms(cs[c * nterm:(c + 1) * nterm]) for c in range(nchunk)]
        tgt_c = [(v * LOG2E).T for v in acsT_c]
        src_c = [acsT_c[c] * LOG2E - jnp.log2(dtT_c[c]) for c in range(nchunk)]
        decayT_c = [dtT_c[c] * jnp.exp(acsT_c[c][:, last:last + 1] - acsT_c[c]) for c in range(nchunk)]
        exp_acs_all = expand([jnp.exp(v) for v in acsT_c])
        decay_all = expand(decayT_c)
        return tgt_c, src_c, exp_acs_all, decay_all

    terms = [per_head_terms(dtT_all_ref.at[bi]) for bi in range(nb)]

    def project_block(bi, k):
        x_ref, attn_ref, x1_ref, hout_ref, o_ref = (r.at[bi] for r in (x_all, attn_all, x1_all, hout_all, o_all))
        rows = slice(k * tm, (k + 1) * tm)
        x1 = (x_ref[rows, :]
              + jnp.dot(attn_ref[rows, :], wa_ref[...], preferred_element_type=F32)
              + jnp.dot(o_ref[rows, :], ws_ref[...], preferred_element_type=F32))
        x1_ref[rows, :] = x1
        h = _rms(x1, g2_ref[...]).astype(BF16)
        hout_ref[rows, :] = jnp.dot(perm_ref[...], h, preferred_element_type=F32).astype(BF16)

    def scan_chunk(bi, c):
        xc_ref, o_ref, h_s = xc_all.at[bi], o_all.at[bi], h_all.at[bi]
        tgt_c, src_c, exp_acs_all, decay_all = terms[bi]
        r0 = c * CHUNK
        xc = xc_ref[r0:r0 + CHUNK, :]
        x_b = xc[:, :SSM_INNER]
        xs = x_b.astype(F32)
        bm = xc[:, SSM_INNER:SSM_INNER + SSM_GROUPS * D_STATE]
        cm = xc[:, SSM_INNER + SSM_GROUPS * D_STATE:]

        tgt, src = tgt_c[c], src_c[c]
        exp_acs = exp_acs_all[r0:r0 + CHUNK, :]
        x_decay = (xs * decay_all[r0:r0 + CHUNK, :]).astype(BF16)
        chunk_decay = exp_acs[last:last + 1, :]

        ys = []
        for g in range(SSM_GROUPS):
            bg = bm[:, g * D_STATE:(g + 1) * D_STATE]
            cg = cm[:, g * D_STATE:(g + 1) * D_STATE]
            cb = lax.dot_general(cg, bg, (((1,), (1,)), ((), ())), preferred_element_type=F32)
            gs = slice(g * GROUP_W, (g + 1) * GROUP_W)
            bgT = bg.astype(F32).T.astype(BF16)
            st = jnp.dot(bgT, x_decay[:, gs], preferred_element_type=F32)
            h_prev = h_s[g]
            y_off = jnp.dot(cg, h_prev.astype(BF16), preferred_element_type=F32)
            h_s[g] = h_prev * chunk_decay[:, gs] + st
            weights = []
            for rr in range(SSM_HPG):
                r = g * SSM_HPG + rr
                seg = tgt[:, d0 + r:d0 + r + 1] - src[d0 + r:d0 + r + 1, :]
                lmat = jnp.exp2(jnp.where(causal, seg, -jnp.inf))
                weights.append((cb * lmat).astype(BF16))
            x_heads = jnp.where(head_block, jnp.concatenate([x_b[:, gs]] * SSM_HPG, axis=0), 0)
            y_diag = jnp.dot(jnp.concatenate(weights, axis=1), x_heads, preferred_element_type=F32)
            ys.append(y_diag + y_off * exp_acs[:, gs])
        y = jnp.concatenate(ys, axis=1)

        if not reverse:
            o_ref[r0:r0 + CHUNK, :] = y + xs * dskip_ref[...]
        else:
            yf_ref, z_ref = yf_all.at[bi], z_all.at[bi]
            y = (y + yf_ref[r0:r0 + CHUNK, :]) * _silu(z_ref[r0:r0 + CHUNK, :].astype(F32))
            parts = []
            for g in range(SSM_GROUPS):
                yg = y[:, g * GROUP_W:(g + 1) * GROUP_W]
                parts.append(yg * lax.rsqrt(jnp.mean(yg * yg, axis=-1, keepdims=True) + EPS))
            o_ref[r0:r0 + CHUNK, :] = (jnp.concatenate(parts, axis=1) * ng_ref[...]).astype(BF16)

            if r0 % tm == 0:
                project_block(bi, r0 // tm)

    for c in (range(nchunk - 1, -1, -1) if reverse else range(nchunk)):
        for bi in range(nb):
            scan_chunk(bi, c)


def _ssd_pass(xc, dtT, biasT, aT, e, extra, B, S, ts, tm, nb, reverse):
    nt = S // ts
    seq = (lambda t: nt - 1 - t) if reverse else (lambda t: t)
    main = lambda b, t: (b, seq(t), 0)
    const = lambda b, t: (0, 0)
    tok = lambda cols: pl.BlockSpec((nb, ts, cols), main)
    rows3 = lambda a: a.reshape(B, S, a.shape[-1])
    state = pltpu.VMEM((nb, SSM_GROUPS, D_STATE, GROUP_W), F32)
    in_specs = [
        tok(XBC_COLS),
        pl.BlockSpec((nb, DT_COLS, ts), lambda b, t: (b, 0, seq(t))),
        pl.BlockSpec((DT_COLS, 1), const),
        pl.BlockSpec((DT_COLS, 1), const),
        pl.BlockSpec((SPLIT_TERMS * DT_COLS, SSM_INNER), const),
    ]
    args = [rows3(xc), dtT, biasT, aT, e]
    if reverse:
        z, yf, ng, x2d, attn, w_attn, w_ssd, g2 = extra
        in_specs += [tok(SSM_INNER), tok(SSM_INNER), pl.BlockSpec((1, SSM_INNER), const),
                     tok(D_MODEL), tok(ATTN_WIDTH),
                     pl.BlockSpec((ATTN_WIDTH, D_MODEL), const), pl.BlockSpec((SSM_INNER, D_MODEL), const),
                     pl.BlockSpec((1, D_MODEL), const), pl.BlockSpec((tm, tm), const)]
        perm = _interleave_tokens(jnp.eye(tm, dtype=F32)).astype(BF16)
        args += [rows3(z), yf, ng, rows3(x2d), rows3(attn), w_attn, w_ssd, g2, perm]
        out_specs = [tok(D_MODEL), tok(D_MODEL)]
        out_shape = [jax.ShapeDtypeStruct((B, S, D_MODEL), F32), jax.ShapeDtypeStruct((B, S, D_MODEL), BF16)]
        scratch = [state, pltpu.VMEM((nb, ts, SSM_INNER), BF16)]
    else:
        (dskip,) = extra
        in_specs += [pl.BlockSpec((1, SSM_INNER), const)]
        args += [dskip]
        out_specs = tok(SSM_INNER)
        out_shape = jax.ShapeDtypeStruct((B, S, SSM_INNER), F32)
        scratch = [state]
    out = pl.pallas_call(
        functools.partial(_ssd_kernel, reverse=reverse, ts=ts, tm=tm, nb=nb),
        grid=(B // nb, nt),
        in_specs=in_specs,
        out_specs=out_specs,
        out_shape=out_shape,
        scratch_shapes=scratch,
        compiler_params=pltpu.CompilerParams(
            dimension_semantics=("arbitrary", "arbitrary"), vmem_limit_bytes=VMEM_LIMIT),
        name="ssd_bwd" if reverse else "ssd_fwd",
    )(*args)
    return [o.reshape(B * S, D_MODEL) for o in out] if reverse else out


FFN_COLS = 256


def _ffn_kernel(hp_ref, hm_ref, hn_ref, x1_ref, wup_ref, cw_ref, cb_ref, wdn_ref, fg_ref, o_ref,
                hext_s, u_s, glu_s, d_s, *, tm, n_seq_tiles):
    tile = pl.program_id(0) % n_seq_tiles
    nv = tm // SUBLANE
    prev_row = tm + HALO - 1
    next_row = tm
    hext_s[0:tm, :] = hm_ref[...]
    before = jnp.where(tile > 0, hp_ref[...], jnp.zeros_like(hp_ref))
    after = jnp.where(tile < n_seq_tiles - 1, hn_ref[...], jnp.zeros_like(hn_ref))
    halo_row = lax.broadcasted_iota(jnp.int32, (HALO, D_MODEL), 0)
    hext_s[tm:, :] = jnp.where(halo_row >= HALO // 2, before, after)
    f = FFN_COLS
    nchunks = D_FF // f

    def up(c):
        hext = hext_s[...]
        for half, base in enumerate((0, D_FF)):
            u_s[c % 2, :, half * f:(half + 1) * f] = jnp.dot(
                hext, wup_ref[:, base + c * f:base + (c + 1) * f], preferred_element_type=F32)

    def conv(c, half, base):
        cols = slice(base + c * f, base + (c + 1) * f)
        lanes = slice(half * f, (half + 1) * f)
        u = lambda lo, hi: u_s[c % 2, lo:hi, lanes]
        first = jnp.concatenate([u(prev_row, prev_row + 1), u(tm - SUBLANE, tm - 1)], axis=0)
        last = jnp.concatenate([u(1, SUBLANE), u(next_row, next_row + 1)], axis=0)
        before = jnp.concatenate([first, u(0, tm - SUBLANE)], axis=0)
        after = jnp.concatenate([u(SUBLANE, tm), last], axis=0)
        return (cb_ref[:, cols] + before * cw_ref[0:1, cols] + u(0, tm) * cw_ref[1:2, cols]
                + after * cw_ref[2:3, cols])

    up(0)
    for c in range(nchunks):
        if c + 1 < nchunks:
            up(c + 1)
        glu_s[:, c * f:(c + 1) * f] = (_silu(conv(c, 0, 0)) * conv(c, 1, D_FF)).astype(BF16)
    _store_lane_blocks(d_s, jnp.dot(glu_s[...], wdn_ref[...], preferred_element_type=F32))
    for s in range(SUBLANE):
        rows = slice(s * nv, (s + 1) * nv)
        x2 = x1_ref[rows, :] + _load_strided_rows(d_s, s, nv, SUBLANE)
        o_ref[rows, :] = _rms(x2, fg_ref[...])


def _ffn(h2, x1, w_up, conv_w, conv_b, w_down, fg, S, tm):
    T = h2.shape[0]
    n_seq_tiles = S // tm
    hb = tm // HALO
    nh = T // HALO
    row = lambda i: (i, 0)
    const = lambda i: (0, 0)
    prev = lambda i: (jnp.maximum(i * hb - 1, 0), 0)
    nxt = lambda i: (jnp.minimum((i + 1) * hb, nh - 1), 0)
    return pl.pallas_call(
        functools.partial(_ffn_kernel, tm=tm, n_seq_tiles=n_seq_tiles),
        grid=(T // tm,),
        in_specs=[
            pl.BlockSpec((HALO, D_MODEL), prev),
            pl.BlockSpec((tm, D_MODEL), row),
            pl.BlockSpec((HALO, D_MODEL), nxt),
            pl.BlockSpec((tm, D_MODEL), row),
            pl.BlockSpec((D_MODEL, 2 * D_FF), const),
            pl.BlockSpec((FFN_CONV, 2 * D_FF), const),
            pl.BlockSpec((1, 2 * D_FF), const),
            pl.BlockSpec((D_FF, D_MODEL), const),
            pl.BlockSpec((1, D_MODEL), const),
        ],
        out_specs=pl.BlockSpec((tm, D_MODEL), row),
        out_shape=jax.ShapeDtypeStruct((T, D_MODEL), F32),
        scratch_shapes=[pltpu.VMEM((tm + HALO, D_MODEL), BF16),
                        pltpu.VMEM((2, tm + HALO, 2 * FFN_COLS), F32),
                        pltpu.VMEM((tm, D_FF), BF16),
                        pltpu.VMEM((D_MODEL // LANE, tm, LANE), F32)],
        compiler_params=pltpu.CompilerParams(
            dimension_semantics=("arbitrary",), vmem_limit_bytes=VMEM_LIMIT),
        name="ffn",
    )(h2, h2, h2, x1, w_up, conv_w, conv_b, w_down, fg)


def _layer(x2d, B, S, norm1_g, w_in, attn_sink, attn_out_g, ssd_conv_w, ssd_conv_b,
           dt_bias_f, dt_bias_b, a_log_f, a_log_b, ssd_d, ssd_norm_g, w_out, norm2_g,
           w_up, ffn_conv_w, ffn_conv_b, w_down, out_g, tables, tm, ts, tq):
    row = lambda v: v.reshape(1, -1).astype(F32)
    w_dt = jnp.pad(w_in[:, QKVZ_COLS + XBC_COLS:], ((0, 0), (0, LANE - DT_COLS)))
    w_qkvzd = jnp.concatenate([w_in[:, :QKVZ_COLS], w_dt], axis=1).astype(BF16)
    w_xbc = w_in[:, QKVZ_COLS:QKVZ_COLS + XBC_COLS].astype(BF16)
    q, k, vT, z, xc, dtT = _in_proj(x2d, row(norm1_g), w_qkvzd, w_xbc,
                                    ssd_conv_w.astype(F32), row(ssd_conv_b), tables, S, tm)

    sink_rows = jnp.repeat(attn_sink.astype(F32) * LOG2E, BLOCK).reshape(KV_HEADS, GQA_GROUP * BLOCK)
    attn = _attention(q, k, vT, sink_rows, row(attn_out_g), B, S, tq)

    bias = jnp.concatenate([dt_bias_f, dt_bias_b]).astype(F32)
    a = -jnp.exp(jnp.concatenate([a_log_f, a_log_b]).astype(F32))
    head_of_col = jnp.arange(SSM_INNER) // SSM_HEAD_DIM
    w_o = w_out.astype(BF16)
    out = None
    for reverse in (False, True):
        d0 = SSM_HEADS if reverse else 0
        term_row = jnp.arange(SPLIT_TERMS * DT_COLS) % DT_COLS
        e = (term_row[:, None] == (head_of_col + d0)[None, :]).astype(BF16)
        if reverse:
            extra = (z, out, row(ssd_norm_g), x2d, attn, w_o[:ATTN_WIDTH], w_o[ATTN_WIDTH:], row(norm2_g))
        else:
            extra = (row(jnp.repeat(ssd_d.astype(F32), SSM_HEAD_DIM)),)
        out = _ssd_pass(xc, dtT, bias.reshape(-1, 1), a.reshape(-1, 1), e, extra, B, S,
                        ts if reverse else 2 * ts, tm, SSD_ROWS, reverse)
    x1, h2 = out
    return _ffn(h2, x1, w_up.astype(BF16), ffn_conv_w.astype(F32), row(ffn_conv_b),
                w_down.astype(BF16), out_g, S, tm)


def kernel(x, norm1_g, w_in, attn_sink, attn_out_g, ssd_conv_w, ssd_conv_b, ssd_dt_bias_fwd,
           ssd_dt_bias_bwd, ssd_a_log_fwd, ssd_a_log_bwd, ssd_d, ssd_norm_g, w_out, norm2_g, w_up,
           ffn_conv_w, ffn_conv_b, w_down, final_norm_g, *, tm=512, ts=1024, tq=2048):
    B, S, _ = x.shape
    depth = w_in.shape[0]
    assert depth == 1, "the fused final RMSNorm assumes a single layer"
    tables = _rope_tables(S)
    x2d = x.reshape(B * S, D_MODEL)
    out = _layer(x2d, B, S, norm1_g[0], w_in[0], attn_sink[0], attn_out_g[0], ssd_conv_w[0], ssd_conv_b[0],
                 ssd_dt_bias_fwd[0], ssd_dt_bias_bwd[0], ssd_a_log_fwd[0], ssd_a_log_bwd[0], ssd_d[0],
                 ssd_norm_g[0], w_out[0], norm2_g[0], w_up[0], ffn_conv_w[0], ffn_conv_b[0], w_down[0],
                 final_norm_g.reshape(1, -1).astype(F32), tables, tm, ts, tq)
    return out.reshape(B, S, D_MODEL)
```

```python
import functools
import math

import jax
import jax.numpy as jnp
from jax import lax
from jax.experimental import pallas as pl
from jax.experimental.pallas import tpu as pltpu

D_MODEL = 1024
HEAD_DIM = 64
ATTN_WIDTH = 512
KV_HEADS = 2
GQA_GROUP = 4
ROT_DIM = 16
ROPE_THETA = 500000.0
WINDOW = 128
BLOCK = 128
SSM_INNER = 512
SSM_HEAD_DIM = 64
SSM_HEADS = 8
SSM_GROUPS = 2
SSM_HPG = 4
D_STATE = 128
SSM_CONV = 5
CHUNK = 128
D_FF = 2816
FFN_CONV = 3
EPS = 1e-5
NEG = -1e30
LOG2E = math.log2(math.e)

KV_COLS = KV_HEADS * HEAD_DIM
XBC_COLS = SSM_INNER + 2 * SSM_GROUPS * D_STATE
QKVZ_COLS = ATTN_WIDTH + 2 * KV_COLS + SSM_INNER
DT_COLS = 2 * SSM_HEADS
SPLIT_TERMS = 3
GROUP_W = SSM_HPG * SSM_HEAD_DIM

LANE = 128
SUBLANE = 8
BF16_ROWS = 16
HALO = BF16_ROWS
assert WINDOW == BLOCK, "the banded attention masks assume one key block of reach on each side"
CONV_COLS = 256
SSD_ROWS = 1
VMEM_LIMIT = 56 * 1024 * 1024

F32 = jnp.float32
BF16 = jnp.bfloat16


def _rms(x, g):
    return x * lax.rsqrt(jnp.mean(x * x, axis=-1, keepdims=True) + EPS) * g


def _silu(x):
    return x * (1.0 / (1.0 + jnp.exp(-x)))


def _softplus(x):
    return jnp.maximum(x, 0.0) + jnp.log1p(jnp.exp(-jnp.abs(x)))


def _interleave_tokens(x):
    tm, cols = x.shape
    return jnp.swapaxes(x.reshape(SUBLANE, tm // SUBLANE, cols), 0, 1).reshape(tm, cols)


def _store_lane_blocks(ref, val):
    for j in range(ref.shape[0]):
        ref[j] = val[:, j * LANE:(j + 1) * LANE]


def _load_strided_rows(ref, start, size, stride):
    return jnp.concatenate([ref[j, pl.ds(start, size, stride=stride), :] for j in range(ref.shape[0])], axis=1)


def _inproj_kernel(xp_ref, x_ref, xn_ref, g_ref, w_ref, wx_ref, cw_ref, cb_ref,
                   cos_ref, sa_ref, sb_ref,
                   q_ref, k_ref, vT_ref, z_ref, xc_ref, dtT_ref, hext_s, xbc_s, proj_s, xcp_s,
                   *, tm, n_seq_tiles):
    tile = pl.program_id(0) % n_seq_tiles
    nv = tm // SUBLANE
    norm = lambda r: _rms(r[...], g_ref[...]).astype(BF16)
    hf = _rms(x_ref[...], g_ref[...])
    hb = hf.astype(BF16)
    hext_s[0:tm, :] = _interleave_tokens(hf).astype(BF16)
    before = jnp.where(tile > 0, norm(xp_ref), jnp.zeros((HALO, D_MODEL), BF16))
    after = jnp.where(tile < n_seq_tiles - 1, norm(xn_ref), jnp.zeros((HALO, D_MODEL), BF16))
    halo_row = lax.broadcasted_iota(jnp.int32, (HALO, D_MODEL), 0)
    hext_s[tm:, :] = jnp.where(halo_row >= HALO // 2, before, after)

    f = CONV_COLS
    nchunks = XBC_COLS // f

    def xbc_proj(c):
        xbc_s[c % 2] = jnp.dot(hext_s[...], wx_ref[:, c * f:(c + 1) * f], preferred_element_type=F32)

    def conv(c):
        cols = slice(c * f, (c + 1) * f)
        u = lambda lo, hi: xbc_s[c % 2, lo:hi, :]
        tok = lambda t: u(tm + t % HALO, tm + t % HALO + 1)
        cat = lambda *parts: jnp.concatenate(parts, axis=0)
        wrap_m1 = cat(tok(-1), u(tm - SUBLANE, tm - 1))
        wrap_m2 = cat(tok(-2), u(tm - 2 * SUBLANE, tm - SUBLANE - 1))
        wrap_p1 = cat(u(1, SUBLANE), tok(0))
        wrap_p2 = cat(u(SUBLANE + 1, 2 * SUBLANE), tok(1))
        taps = [cat(wrap_m2, wrap_m1, u(0, tm - 2 * SUBLANE)),
                cat(wrap_m1, u(0, tm - SUBLANE)),
                u(0, tm),
                cat(u(SUBLANE, tm), wrap_p1),
                cat(u(2 * SUBLANE, tm), wrap_p1, wrap_p2)]
        acc = cb_ref[:, cols]
        for kk in range(SSM_CONV):
            acc = acc + taps[kk] * cw_ref[kk:kk + 1, cols]
        act = _silu(acc)
        for j in range(f // LANE):
            xcp_s[c * (f // LANE) + j] = act[:, j * LANE:(j + 1) * LANE]

    cos, sa, sb = cos_ref[...], sa_ref[...], sb_ref[...]

    def rope(t):
        n = t.shape[1]
        reps = n // LANE
        cc = jnp.concatenate([cos] * reps, axis=1) if reps > 1 else cos
        aa = jnp.concatenate([sa] * reps, axis=1) if reps > 1 else sa
        bb = jnp.concatenate([sb] * reps, axis=1) if reps > 1 else sb
        half = ROT_DIM // 2
        return t * cc + pltpu.roll(t, n - half, 1) * aa + pltpu.roll(t, half, 1) * bb

    assert nchunks == 4
    qk = ATTN_WIDTH + KV_COLS
    xbc_proj(0)
    proj_s[:, :qk] = jnp.dot(hb, w_ref[:, :qk], preferred_element_type=F32)
    xbc_proj(1)
    conv(0)
    q_ref[...] = (rope(proj_s[:, :ATTN_WIDTH]) * (LOG2E / math.sqrt(HEAD_DIM))).astype(BF16)
    k_ref[...] = rope(proj_s[:, ATTN_WIDTH:qk]).astype(BF16)
    xbc_proj(2)
    conv(1)
    proj_s[:, qk:] = jnp.dot(hb, w_ref[:, qk:], preferred_element_type=F32)
    xbc_proj(3)
    conv(2)
    vT_ref[...] = proj_s[:, qk:qk + KV_COLS].T.astype(BF16)
    z_ref[...] = proj_s[:, qk + KV_COLS:QKVZ_COLS].astype(BF16)
    dtT_ref[...] = proj_s[:, QKVZ_COLS:].T[:DT_COLS, :]
    conv(3)
    for s in range(SUBLANE):
        xc_ref[s * nv:(s + 1) * nv, :] = _load_strided_rows(xcp_s, s, nv, SUBLANE).astype(BF16)


def _rope_tables(S):
    half = ROT_DIM // 2
    pos = jnp.arange(S, dtype=F32)
    inv = ROPE_THETA ** (-jnp.arange(0, ROT_DIM, 2, dtype=F32) / ROT_DIM)
    ang = pos[:, None] * inv[None, :]
    cos, sin = jnp.cos(ang), jnp.sin(ang)
    rest = HEAD_DIM - ROT_DIM
    ones, zeros, zh = jnp.ones((S, rest), F32), jnp.zeros((S, rest), F32), jnp.zeros((S, half), F32)
    c = jnp.concatenate([cos, cos, ones], axis=1)
    sa = jnp.concatenate([-sin, zh, zeros], axis=1)
    sb = jnp.concatenate([zh, sin, zeros], axis=1)
    rep = LANE // HEAD_DIM
    return tuple(jnp.tile(t, (1, rep)) for t in (c, sa, sb))


def _in_proj(x2d, g1, w_qkvzd, w_xbc, conv_w, conv_b, tables, S, tm):
    T = x2d.shape[0]
    n_seq_tiles = S // tm
    hb = tm // HALO
    nh = T // HALO
    row = lambda i: (i, 0)
    col = lambda i: (0, i)
    const = lambda i: (0, 0)
    prev = lambda i: (jnp.maximum(i * hb - 1, 0), 0)
    nxt = lambda i: (jnp.minimum((i + 1) * hb, nh - 1), 0)
    tab = lambda i: (i % n_seq_tiles, 0)
    return pl.pallas_call(
        functools.partial(_inproj_kernel, tm=tm, n_seq_tiles=n_seq_tiles),
        grid=(T // tm,),
        in_specs=[
            pl.BlockSpec((HALO, D_MODEL), prev),
            pl.BlockSpec((tm, D_MODEL), row),
            pl.BlockSpec((HALO, D_MODEL), nxt),
            pl.BlockSpec((1, D_MODEL), const),
            pl.BlockSpec((D_MODEL, QKVZ_COLS + LANE), const),
            pl.BlockSpec((D_MODEL, XBC_COLS), const),
            pl.BlockSpec((SSM_CONV, XBC_COLS), const),
            pl.BlockSpec((1, XBC_COLS), const),
            pl.BlockSpec((tm, LANE), tab),
            pl.BlockSpec((tm, LANE), tab),
            pl.BlockSpec((tm, LANE), tab),
        ],
        out_specs=[
            pl.BlockSpec((tm, ATTN_WIDTH), row),
            pl.BlockSpec((tm, KV_COLS), row),
            pl.BlockSpec((KV_COLS, tm), col),
            pl.BlockSpec((tm, SSM_INNER), row),
            pl.BlockSpec((tm, XBC_COLS), row),
            pl.BlockSpec((None, DT_COLS, tm), lambda i: (i // n_seq_tiles, 0, i % n_seq_tiles)),
        ],
        out_shape=[
            jax.ShapeDtypeStruct((T, ATTN_WIDTH), BF16),
            jax.ShapeDtypeStruct((T, KV_COLS), BF16),
            jax.ShapeDtypeStruct((KV_COLS, T), BF16),
            jax.ShapeDtypeStruct((T, SSM_INNER), BF16),
            jax.ShapeDtypeStruct((T, XBC_COLS), BF16),
            jax.ShapeDtypeStruct((T // S, DT_COLS, S), F32),
        ],
        scratch_shapes=[pltpu.VMEM((tm + HALO, D_MODEL), BF16),
                        pltpu.VMEM((2, tm + HALO, CONV_COLS), F32),
                        pltpu.VMEM((tm, QKVZ_COLS + LANE), F32),
                        pltpu.VMEM((XBC_COLS // LANE, tm, LANE), F32)],
        compiler_params=pltpu.CompilerParams(
            dimension_semantics=("arbitrary",), vmem_limit_bytes=VMEM_LIMIT),
        name="in_proj",
    )(x2d, x2d, x2d, g1, w_qkvzd, w_xbc, conv_w, conv_b, *tables)


def _attn_kernel(sink_ref, q_ref, kp_ref, km_ref, kn_ref, vp_ref, vm_ref, vn_ref, g_ref, o_ref,
                 k_s, vT_s, sT_s, *, tq):
    i = pl.program_id(1)
    n = pl.num_programs(1)
    k_s[0:BLOCK, :] = kp_ref[...]
    k_s[BLOCK:BLOCK + tq, :] = km_ref[...]
    k_s[BLOCK + tq:, :] = kn_ref[...]
    vT_s[:, 0:BLOCK] = vp_ref[...]
    vT_s[:, BLOCK:BLOCK + tq] = vm_ref[...]
    vT_s[:, BLOCK + tq:] = vn_ref[...]

    band = 3 * BLOCK
    nq = GQA_GROUP * BLOCK
    key = lax.broadcasted_iota(jnp.int32, (BLOCK, nq), 0)
    qry = lax.broadcasted_iota(jnp.int32, (BLOCK, nq), 1) % BLOCK
    no_prev = jnp.where(i > 0, 0, BLOCK)
    no_next = jnp.where(i < n - 1, 0, BLOCK)
    cap = lambda ok: jnp.where(ok, jnp.inf, NEG).astype(F32)
    cap_prev, cap_prev_edge = cap(key >= qry), cap(key >= qry + no_prev)
    cap_next, cap_next_edge = cap(key <= qry), cap(key <= qry - no_next)
    ones = jnp.ones((BF16_ROWS, band), BF16)

    nsub = tq // BLOCK
    units = [(j, kh) for j in range(nsub) for kh in range(KV_HEADS)]

    def scores(u):
        j, kh = units[u]
        qj = q_ref[j * BLOCK:(j + 1) * BLOCK, :]
        kk = k_s[j * BLOCK:j * BLOCK + band, kh * HEAD_DIM:(kh + 1) * HEAD_DIM]
        qs = jnp.concatenate([qj[:, (kh * GQA_GROUP + g) * HEAD_DIM:(kh * GQA_GROUP + g + 1) * HEAD_DIM]
                              for g in range(GQA_GROUP)], axis=0)
        sT_s[u % 2] = lax.dot_general(kk, qs, (((1,), (1,)), ((), ())), preferred_element_type=F32)

    scores(0)
    for j in range(nsub):
        prev_cap = cap_prev_edge if j == 0 else cap_prev
        next_cap = cap_next_edge if j == nsub - 1 else cap_next
        vTj = vT_s[:, j * BLOCK:j * BLOCK + band]
        outs = []
        for kh in range(KV_HEADS):
            u = j * KV_HEADS + kh
            if u + 1 < len(units):
                scores(u + 1)
            s_blocks = [jnp.minimum(sT_s[u % 2, :BLOCK, :], prev_cap), sT_s[u % 2, BLOCK:2 * BLOCK, :],
                        jnp.minimum(sT_s[u % 2, 2 * BLOCK:, :], next_cap)]
            sink = sink_ref[kh:kh + 1, :]
            s_max = jnp.maximum(jnp.maximum(s_blocks[0], s_blocks[1]), s_blocks[2])
            m = jnp.maximum(jnp.max(s_max, axis=0, keepdims=True), sink)
            p = jnp.concatenate([jnp.exp2(sb - m) for sb in s_blocks], axis=0).astype(BF16)
            v_ext = jnp.concatenate([vTj[kh * HEAD_DIM:(kh + 1) * HEAD_DIM, :], ones], axis=0)
            o_ext = jnp.dot(v_ext, p, preferred_element_type=F32)
            denom = o_ext[HEAD_DIM:HEAD_DIM + 1, :] + jnp.exp2(sink - m)
            oT = o_ext[:HEAD_DIM, :] / denom
            outs += [oT[:, g * BLOCK:(g + 1) * BLOCK] for g in range(GQA_GROUP)]
        oT_all = jnp.concatenate(outs, axis=0)
        inv = lax.rsqrt(jnp.mean(oT_all * oT_all, axis=0, keepdims=True) + EPS)
        o_ref[j * BLOCK:(j + 1) * BLOCK, :] = ((oT_all * inv).T * g_ref[...]).astype(BF16)


def _attention(q, k, vT, sink_rows, g, B, S, tq):
    T = q.shape[0]
    nq = S // tq
    sub = tq // BLOCK
    nblk = S // BLOCK
    main = lambda b, i: (b * nq + i, 0)
    prev = lambda b, i: (b * nblk + jnp.maximum(i * sub - 1, 0), 0)
    nxt = lambda b, i: (b * nblk + jnp.minimum((i + 1) * sub, nblk - 1), 0)
    swap = lambda f: (lambda b, i: f(b, i)[::-1])
    const = lambda b, i: (0, 0)
    return pl.pallas_call(
        functools.partial(_attn_kernel, tq=tq),
        grid=(B, nq),
        in_specs=[
            pl.BlockSpec((KV_HEADS, GQA_GROUP * BLOCK), const),
            pl.BlockSpec((tq, ATTN_WIDTH), main),
            pl.BlockSpec((BLOCK, KV_COLS), prev),
            pl.BlockSpec((tq, KV_COLS), main),
            pl.BlockSpec((BLOCK, KV_COLS), nxt),
            pl.BlockSpec((KV_COLS, BLOCK), swap(prev)),
            pl.BlockSpec((KV_COLS, tq), swap(main)),
            pl.BlockSpec((KV_COLS, BLOCK), swap(nxt)),
            pl.BlockSpec((1, ATTN_WIDTH), const),
        ],
        out_specs=pl.BlockSpec((tq, ATTN_WIDTH), main),
        out_shape=jax.ShapeDtypeStruct((T, ATTN_WIDTH), BF16),
        scratch_shapes=[pltpu.VMEM((tq + 2 * BLOCK, KV_COLS), BF16),
                        pltpu.VMEM((KV_COLS, tq + 2 * BLOCK), BF16),
                        pltpu.VMEM((2, 3 * BLOCK, GQA_GROUP * BLOCK), F32)],
        compiler_params=pltpu.CompilerParams(
            dimension_semantics=("arbitrary", "arbitrary"), vmem_limit_bytes=VMEM_LIMIT),
        name="attention",
    )(sink_rows, q, k, k, k, vT, vT, vT, g)


def _split_terms(v):
    hi = v.astype(BF16).astype(F32)
    r1 = v - hi
    mid = r1.astype(BF16).astype(F32)
    lo = (r1 - mid).astype(BF16).astype(F32)
    return jnp.concatenate([hi, mid, lo], axis=0)


def _ssd_kernel(*refs, reverse, ts, tm, nb):
    if reverse:
        (xc_all, dtT_all_ref, biasT_ref, aT_ref, e_ref, z_all, yf_all, ng_ref,
         x_all, attn_all, wa_ref, ws_ref, g2_ref, perm_ref, x1_all, hout_all, h_all, o_all) = refs
    else:
        (xc_all, dtT_all_ref, biasT_ref, aT_ref, e_ref, dskip_ref, o_all, h_all) = refs
    t = pl.program_id(1)
    d0 = SSM_HEADS if reverse else 0

    @pl.when(t == 0)
    def _():
        h_all[...] = jnp.zeros_like(h_all)

    li = lax.broadcasted_iota(jnp.int32, (CHUNK, CHUNK), 0)
    si = lax.broadcasted_iota(jnp.int32, (CHUNK, CHUNK), 1)
    causal = (li <= si) if reverse else (li >= si)
    scan_op = jnp.where((li >= si) if reverse else (li <= si), 1.0, 0.0).astype(BF16)
    last = 0 if reverse else CHUNK - 1
    head_block = (lax.broadcasted_iota(jnp.int32, (SSM_HPG * CHUNK, GROUP_W), 0) // CHUNK
                  == lax.broadcasted_iota(jnp.int32, (SSM_HPG * CHUNK, GROUP_W), 1) // SSM_HEAD_DIM)
    nchunk = ts // CHUNK
    nterm = SPLIT_TERMS * DT_COLS

    def chunks_of(vT):
        return [vT[:, c * CHUNK:(c + 1) * CHUNK] for c in range(nchunk)]

    def sum_terms(m):
        return m[0:DT_COLS] + m[DT_COLS:2 * DT_COLS] + m[2 * DT_COLS:nterm]

    def expand(vT_c):
        terms = jnp.concatenate([_split_terms(v).T for v in vT_c], axis=0)
        return jnp.dot(terms.astype(BF16), e_ref[...], preferred_element_type=F32)

    def per_head_terms(dtT_ref):
        dtT_all = _softplus(dtT_ref[...] + biasT_ref[...])
        a_terms = jnp.concatenate([_split_terms(v) for v in chunks_of(dtT_all * aT_ref[...])], axis=0)
        cs = jnp.dot(a_terms.astype(BF16), scan_op, preferred_element_type=F32)
        dtT_c = chunks_of(dtT_all)
        acsT_c = [sum_terms(cs[c * nterm:(c + 1) * nterm]) for c in range(nchunk)]
        tgt_c = [(v * LOG2E).T for v in acsT_c]
        src_c = [acsT_c[c] * LOG2E - jnp.log2(dtT_c[c]) for c in range(nchunk)]
        decayT_c = [dtT_c[c] * jnp.exp(acsT_c[c][:, last:last + 1] - acsT_c[c]) for c in range(nchunk)]
        exp_acs_all = expand([jnp.exp(v) for v in acsT_c])
        decay_all = expand(decayT_c)
        return tgt_c, src_c, exp_acs_all, decay_all

    terms = [per_head_terms(dtT_all_ref.at[bi]) for bi in range(nb)]

    def project_block(bi, k):
        x_ref, attn_ref, x1_ref, hout_ref, o_ref = (r.at[bi] for r in (x_all, attn_all, x1_all, hout_all, o_all))
        rows = slice(k * tm, (k + 1) * tm)
        x1 = (x_ref[rows, :]
              + jnp.dot(attn_ref[rows, :], wa_ref[...], preferred_element_type=F32)
              + jnp.dot(o_ref[rows, :], ws_ref[...], preferred_element_type=F32))
        x1_ref[rows, :] = x1
        h = _rms(x1, g2_ref[...]).astype(BF16)
        hout_ref[rows, :] = jnp.dot(perm_ref[...], h, preferred_element_type=F32).astype(BF16)

    def scan_chunk(bi, c):
        xc_ref, o_ref, h_s = xc_all.at[bi], o_all.at[bi], h_all.at[bi]
        tgt_c, src_c, exp_acs_all, decay_all = terms[bi]
        r0 = c * CHUNK
        xc = xc_ref[r0:r0 + CHUNK, :]
        x_b = xc[:, :SSM_INNER]
        xs = x_b.astype(F32)
        bm = xc[:, SSM_INNER:SSM_INNER + SSM_GROUPS * D_STATE]
        cm = xc[:, SSM_INNER + SSM_GROUPS * D_STATE:]

        tgt, src = tgt_c[c], src_c[c]
        exp_acs = exp_acs_all[r0:r0 + CHUNK, :]
        x_decay = (xs * decay_all[r0:r0 + CHUNK, :]).astype(BF16)
        chunk_decay = exp_acs[last:last + 1, :]

        ys = []
        for g in range(SSM_GROUPS):
            bg = bm[:, g * D_STATE:(g + 1) * D_STATE]
            cg = cm[:, g * D_STATE:(g + 1) * D_STATE]
            cb = lax.dot_general(cg, bg, (((1,), (1,)), ((), ())), preferred_element_type=F32)
            gs = slice(g * GROUP_W, (g + 1) * GROUP_W)
            bgT = bg.astype(F32).T.astype(BF16)
            st = jnp.dot(bgT, x_decay[:, gs], preferred_element_type=F32)
            h_prev = h_s[g]
            y_off = jnp.dot(cg, h_prev.astype(BF16), preferred_element_type=F32)
            h_s[g] = h_prev * chunk_decay[:, gs] + st
            weights = []
            for rr in range(SSM_HPG):
                r = g * SSM_HPG + rr
                seg = tgt[:, d0 + r:d0 + r + 1] - src[d0 + r:d0 + r + 1, :]
                lmat = jnp.exp2(jnp.where(causal, seg, -jnp.inf))
                weights.append((cb * lmat).astype(BF16))
            x_heads = jnp.where(head_block, jnp.concatenate([x_b[:, gs]] * SSM_HPG, axis=0), 0)
            y_diag = jnp.dot(jnp.concatenate(weights, axis=1), x_heads, preferred_element_type=F32)
            ys.append(y_diag + y_off * exp_acs[:, gs])
        y = jnp.concatenate(ys, axis=1)

        if not reverse:
            o_ref[r0:r0 + CHUNK, :] = (y + xs * dskip_ref[...]).astype(BF16)
        else:
            yf_ref, z_ref = yf_all.at[bi], z_all.at[bi]
            y = (y + yf_ref[r0:r0 + CHUNK, :].astype(F32)) * _silu(z_ref[r0:r0 + CHUNK, :].astype(F32))
            parts = []
            for g in range(SSM_GROUPS):
                yg = y[:, g * GROUP_W:(g + 1) * GROUP_W]
                parts.append(yg * lax.rsqrt(jnp.mean(yg * yg, axis=-1, keepdims=True) + EPS))
            o_ref[r0:r0 + CHUNK, :] = (jnp.concatenate(parts, axis=1) * ng_ref[...]).astype(BF16)

            if r0 % tm == 0:
                project_block(bi, r0 // tm)

    for c in (range(nchunk - 1, -1, -1) if reverse else range(nchunk)):
        for bi in range(nb):
            scan_chunk(bi, c)


def _ssd_pass(xc, dtT, biasT, aT, e, extra, B, S, ts, tm, nb, reverse):
    nt = S // ts
    seq = (lambda t: nt - 1 - t) if reverse else (lambda t: t)
    main = lambda b, t: (b, seq(t), 0)
    const = lambda b, t: (0, 0)
    tok = lambda cols: pl.BlockSpec((nb, ts, cols), main)
    rows3 = lambda a: a.reshape(B, S, a.shape[-1])
    state = pltpu.VMEM((nb, SSM_GROUPS, D_STATE, GROUP_W), F32)
    in_specs = [
        tok(XBC_COLS),
        pl.BlockSpec((nb, DT_COLS, ts), lambda b, t: (b, 0, seq(t))),
        pl.BlockSpec((DT_COLS, 1), const),
        pl.BlockSpec((DT_COLS, 1), const),
        pl.BlockSpec((SPLIT_TERMS * DT_COLS, SSM_INNER), const),
    ]
    args = [rows3(xc), dtT, biasT, aT, e]
    if reverse:
        z, yf, ng, x2d, attn, w_attn, w_ssd, g2 = extra
        in_specs += [tok(SSM_INNER), tok(SSM_INNER), pl.BlockSpec((1, SSM_INNER), const),
                     tok(D_MODEL), tok(ATTN_WIDTH),
                     pl.BlockSpec((ATTN_WIDTH, D_MODEL), const), pl.BlockSpec((SSM_INNER, D_MODEL), const),
                     pl.BlockSpec((1, D_MODEL), const), pl.BlockSpec((tm, tm), const)]
        perm = _interleave_tokens(jnp.eye(tm, dtype=F32)).astype(BF16)
        args += [rows3(z), yf, ng, rows3(x2d), rows3(attn), w_attn, w_ssd, g2, perm]
        out_specs = [tok(D_MODEL), tok(D_MODEL)]
        out_shape = [jax.ShapeDtypeStruct((B, S, D_MODEL), F32), jax.ShapeDtypeStruct((B, S, D_MODEL), BF16)]
        scratch = [state, pltpu.VMEM((nb, ts, SSM_INNER), BF16)]
    else:
        (dskip,) = extra
        in_specs += [pl.BlockSpec((1, SSM_INNER), const)]
        args += [dskip]
        out_specs = tok(SSM_INNER)
        out_shape = jax.ShapeDtypeStruct((B, S, SSM_INNER), BF16)
        scratch = [state]
    out = pl.pallas_call(
        functools.partial(_ssd_kernel, reverse=reverse, ts=ts, tm=tm, nb=nb),
        grid=(B // nb, nt),
        in_specs=in_specs,
        out_specs=out_specs,
        out_shape=out_shape,
        scratch_shapes=scratch,
        compiler_params=pltpu.CompilerParams(
            dimension_semantics=("arbitrary", "arbitrary"), vmem_limit_bytes=VMEM_LIMIT),
        name="ssd_bwd" if reverse else "ssd_fwd",
    )(*args)
    return [o.reshape(B * S, D_MODEL) for o in out] if reverse else out


FFN_COLS = 256


def _ffn_kernel(hp_ref, hm_ref, hn_ref, x1_ref, wup_ref, cw_ref, cb_ref, wdn_ref, fg_ref, o_ref,
                hext_s, u_s, glu_s, d_s, *, tm, n_seq_tiles):
    tile = pl.program_id(0) % n_seq_tiles
    nv = tm // SUBLANE
    prev_row = tm + HALO - 1
    next_row = tm
    hext_s[0:tm, :] = hm_ref[...]
    before = jnp.where(tile > 0, hp_ref[...], jnp.zeros_like(hp_ref))
    after = jnp.where(tile < n_seq_tiles - 1, hn_ref[...], jnp.zeros_like(hn_ref))
    halo_row = lax.broadcasted_iota(jnp.int32, (HALO, D_MODEL), 0)
    hext_s[tm:, :] = jnp.where(halo_row >= HALO // 2, before, after)
    f = FFN_COLS
    nchunks = D_FF // f

    def up(c):
        hext = hext_s[...]
        for half, base in enumerate((0, D_FF)):
            u_s[c % 2, :, half * f:(half + 1) * f] = jnp.dot(
                hext, wup_ref[:, base + c * f:base + (c + 1) * f], preferred_element_type=F32)

    def conv(c, half, base):
        cols = slice(base + c * f, base + (c + 1) * f)
        lanes = slice(half * f, (half + 1) * f)
        u = lambda lo, hi: u_s[c % 2, lo:hi, lanes]
        first = jnp.concatenate([u(prev_row, prev_row + 1), u(tm - SUBLANE, tm - 1)], axis=0)
        last = jnp.concatenate([u(1, SUBLANE), u(next_row, next_row + 1)], axis=0)
        before = jnp.concatenate([first, u(0, tm - SUBLANE)], axis=0)
        after = jnp.concatenate([u(SUBLANE, tm), last], axis=0)
        return (cb_ref[:, cols] + before * cw_ref[0:1, cols] + u(0, tm) * cw_ref[1:2, cols]
                + after * cw_ref[2:3, cols])

    up(0)
    for c in range(nchunks):
        if c + 1 < nchunks:
            up(c + 1)
        glu_s[:, c * f:(c + 1) * f] = (_silu(conv(c, 0, 0)) * conv(c, 1, D_FF)).astype(BF16)
    _store_lane_blocks(d_s, jnp.dot(glu_s[...], wdn_ref[...], preferred_element_type=F32))
    for s in range(SUBLANE):
        rows = slice(s * nv, (s + 1) * nv)
        x2 = x1_ref[rows, :] + _load_strided_rows(d_s, s, nv, SUBLANE)
        o_ref[rows, :] = _rms(x2, fg_ref[...])


def _ffn(h2, x1, w_up, conv_w, conv_b, w_down, fg, S, tm):
    T = h2.shape[0]
    n_seq_tiles = S // tm
    hb = tm // HALO
    nh = T // HALO
    row = lambda i: (i, 0)
    const = lambda i: (0, 0)
    prev = lambda i: (jnp.maximum(i * hb - 1, 0), 0)
    nxt = lambda i: (jnp.minimum((i + 1) * hb, nh - 1), 0)
    return pl.pallas_call(
        functools.partial(_ffn_kernel, tm=tm, n_seq_tiles=n_seq_tiles),
        grid=(T // tm,),
        in_specs=[
            pl.BlockSpec((HALO, D_MODEL), prev),
            pl.BlockSpec((tm, D_MODEL), row),
            pl.BlockSpec((HALO, D_MODEL), nxt),
            pl.BlockSpec((tm, D_MODEL), row),
            pl.BlockSpec((D_MODEL, 2 * D_FF), const),
            pl.BlockSpec((FFN_CONV, 2 * D_FF), const),
            pl.BlockSpec((1, 2 * D_FF), const),
            pl.BlockSpec((D_FF, D_MODEL), const),
            pl.BlockSpec((1, D_MODEL), const),
        ],
        out_specs=pl.BlockSpec((tm, D_MODEL), row),
        out_shape=jax.ShapeDtypeStruct((T, D_MODEL), F32),
        scratch_shapes=[pltpu.VMEM((tm + HALO, D_MODEL), BF16),
                        pltpu.VMEM((2, tm + HALO, 2 * FFN_COLS), F32),
                        pltpu.VMEM((tm, D_FF), BF16),
                        pltpu.VMEM((D_MODEL // LANE, tm, LANE), F32)],
        compiler_params=pltpu.CompilerParams(
            dimension_semantics=("arbitrary",), vmem_limit_bytes=VMEM_LIMIT),
        name="ffn",
    )(h2, h2, h2, x1, w_up, conv_w, conv_b, w_down, fg)


def _layer(x2d, B, S, norm1_g, w_in, attn_sink, attn_out_g, ssd_conv_w, ssd_conv_b,
           dt_bias_f, dt_bias_b, a_log_f, a_log_b, ssd_d, ssd_norm_g, w_out, norm2_g,
           w_up, ffn_conv_w, ffn_conv_b, w_down, out_g, tables, tm, ts, tq):
    row = lambda v: v.reshape(1, -1).astype(F32)
    w_dt = jnp.pad(w_in[:, QKVZ_COLS + XBC_COLS:], ((0, 0), (0, LANE - DT_COLS)))
    w_qkvzd = jnp.concatenate([w_in[:, :QKVZ_COLS], w_dt], axis=1).astype(BF16)
    w_xbc = w_in[:, QKVZ_COLS:QKVZ_COLS + XBC_COLS].astype(BF16)
    q, k, vT, z, xc, dtT = _in_proj(x2d, row(norm1_g), w_qkvzd, w_xbc,
                                    ssd_conv_w.astype(F32), row(ssd_conv_b), tables, S, tm)

    sink_rows = jnp.repeat(attn_sink.astype(F32) * LOG2E, BLOCK).reshape(KV_HEADS, GQA_GROUP * BLOCK)
    attn = _attention(q, k, vT, sink_rows, row(attn_out_g), B, S, tq)

    bias = jnp.concatenate([dt_bias_f, dt_bias_b]).astype(F32)
    a = -jnp.exp(jnp.concatenate([a_log_f, a_log_b]).astype(F32))
    head_of_col = jnp.arange(SSM_INNER) // SSM_HEAD_DIM
    w_o = w_out.astype(BF16)
    out = None
    for reverse in (False, True):
        d0 = SSM_HEADS if reverse else 0
        term_row = jnp.arange(SPLIT_TERMS * DT_COLS) % DT_COLS
        e = (term_row[:, None] == (head_of_col + d0)[None, :]).astype(BF16)
        if reverse:
            extra = (z, out, row(ssd_norm_g), x2d, attn, w_o[:ATTN_WIDTH], w_o[ATTN_WIDTH:], row(norm2_g))
        else:
            extra = (row(jnp.repeat(ssd_d.astype(F32), SSM_HEAD_DIM)),)
        out = _ssd_pass(xc, dtT, bias.reshape(-1, 1), a.reshape(-1, 1), e, extra, B, S,
                        ts if reverse else 2 * ts, tm, SSD_ROWS, reverse)
    x1, h2 = out
    return _ffn(h2, x1, w_up.astype(BF16), ffn_conv_w.astype(F32), row(ffn_conv_b),
                w_down.astype(BF16), out_g, S, tm)


def kernel(x, norm1_g, w_in, attn_sink, attn_out_g, ssd_conv_w, ssd_conv_b, ssd_dt_bias_fwd,
           ssd_dt_bias_bwd, ssd_a_log_fwd, ssd_a_log_bwd, ssd_d, ssd_norm_g, w_out, norm2_g, w_up,
           ffn_conv_w, ffn_conv_b, w_down, final_norm_g, *, tm=512, ts=1024, tq=2048):
    B, S, _ = x.shape
    depth = w_in.shape[0]
    assert depth == 1, "the fused final RMSNorm assumes a single layer"
    tables = _rope_tables(S)
    x2d = x.reshape(B * S, D_MODEL)
    out = _layer(x2d, B, S, norm1_g[0], w_in[0], attn_sink[0], attn_out_g[0], ssd_conv_w[0], ssd_conv_b[0],
                 ssd_dt_bias_fwd[0], ssd_dt_bias_bwd[0], ssd_a_log_fwd[0], ssd_a_log_bwd[0], ssd_d[0],
                 ssd_norm_g[0], w_out[0], norm2_g[0], w_up[0], ffn_conv_w[0], ffn_conv_b[0], w_down[0],
                 final_norm_g.reshape(1, -1).astype(F32), tables, tm, ts, tq)
    return out.reshape(B, S, D_MODEL)
```
